```python
import jax
import jax.numpy as jnp
from jax import lax
import numpy as np

D_MODEL = 2048
BATCH = 8
SEQ = 2048
DEPTH = 2

GRID_W = 64
CTX_LEN = 256
NORM_EPS = 1e-6

GROUP_WIDTH = D_MODEL // 4

MLA_NOPE = 128
MLA_ROPE = 64
MLA_V = 128
MLA_HEADS = GROUP_WIDTH // MLA_V
MLA_Q_RANK = 384
MLA_KV_RANK = 128
ROPE_BASE = 10000.0
Q_BLOCK = 128

RWKV_HEAD = 64
RWKV_HEADS = GROUP_WIDTH // RWKV_HEAD
RWKV_WIDTH = RWKV_HEADS * RWKV_HEAD
DECAY_RANK = 96
ICL_RANK = 96
GATE_RANK = 256
RWKV_GN_EPS = 64e-5

POOL_WINDOWS = (2, 4, 8, 16)
POOL_WIDTH = GROUP_WIDTH
POOL_GROUP = POOL_WIDTH // len(POOL_WINDOWS)

CONV_WIDTH = GROUP_WIDTH
CONV_TAPS = 3

FF_HIDDEN = 4 * D_MODEL

MLA_COLS = MLA_Q_RANK + MLA_KV_RANK + MLA_ROPE
RWKV_COLS = 3 * RWKV_WIDTH + 2 * DECAY_RANK + 2 * ICL_RANK + GATE_RANK
POOL_COLS = POOL_WIDTH
CONV_COLS = 3 * CONV_WIDTH
IN_COLS = MLA_COLS + RWKV_COLS + POOL_COLS + CONV_COLS
MIX_WIDTH = MLA_HEADS * MLA_V + RWKV_WIDTH + POOL_WIDTH + CONV_WIDTH

kernel_name = "hybrid_mla_rwkv7_pool_conv_dit_trunk"


def split_cols(x, sizes):
    return jnp.split(x, np.cumsum(sizes)[:-1].tolist(), axis=-1)


def rmsnorm(x, g):
    xf = x.astype(jnp.float32)
    y = xf * lax.rsqrt(jnp.mean(jnp.square(xf), axis=-1, keepdims=True) + NORM_EPS)
    return (y * g.astype(jnp.float32)).astype(x.dtype)


def neighbours(u):
    zero = jnp.zeros_like(u[:, :1])
    prev = jnp.concatenate([zero, u[:, :-1]], axis=1)
    nxt = jnp.concatenate([u[:, 1:], zero], axis=1)
    return prev, nxt


def axial_rope_tables(n_tokens):
    rows = n_tokens // GRID_W
    row = jnp.repeat(jnp.arange(rows), GRID_W)
    col = jnp.tile(jnp.arange(GRID_W), rows)
    pos = jnp.stack([row, col], axis=-1).astype(jnp.float32)
    axis_dim = MLA_ROPE // 2
    inv_freq = ROPE_BASE ** (-jnp.arange(0, axis_dim, 2, dtype=jnp.float32) / axis_dim)
    ang = pos[:, :, None] * inv_freq
    return jnp.cos(ang), jnp.sin(ang)


def apply_rope(x, cos, sin):
    xf = x.astype(jnp.float32).reshape(x.shape[:-1] + (2, 2, MLA_ROPE // 4))
    x1, x2 = xf[..., 0, :], xf[..., 1, :]
    cs, sn = cos[None, :, None], sin[None, :, None]
    out = jnp.stack([x1 * cs - x2 * sn, x2 * cs + x1 * sn], axis=-2)
    return out.reshape(x.shape).astype(x.dtype)


def mla_keys(f_kv, f_kr, kv_norm_g, w_ukv, rope):
    b, n, _ = f_kv.shape
    kv = (rmsnorm(f_kv, kv_norm_g) @ w_ukv).reshape(b, n, MLA_HEADS, MLA_NOPE + MLA_V)
    k_nope, v = kv[..., :MLA_NOPE], kv[..., MLA_NOPE:]
    k_pe = f_kr[:, :, None, :]
    if rope is not None:
        k_pe = apply_rope(k_pe, *rope)
    return k_nope, k_pe[:, :, 0], v


def mla_queries(f_q, q_norm_g, w_uq, rope):
    b, n, _ = f_q.shape
    q = (rmsnorm(f_q, q_norm_g) @ w_uq).reshape(b, n, MLA_HEADS, MLA_NOPE + MLA_ROPE)
    q_nope, q_pe = q[..., :MLA_NOPE], q[..., MLA_NOPE:]
    if rope is not None:
        q_pe = apply_rope(q_pe, *rope)
    return q_nope, q_pe


def mla_attend(q_nope, q_pe, k_nope, k_pe, v):
    scale = (MLA_NOPE + MLA_ROPE) ** -0.5
    s = (jnp.einsum('bqhd,bkhd->bhqk', q_nope, k_nope)
         + jnp.einsum('bqhr,bkr->bhqk', q_pe, k_pe)).astype(jnp.float32) * scale
    p = jax.nn.softmax(s, axis=-1).astype(v.dtype)
    o = jnp.einsum('bhqk,bkhd->bqhd', p, v)
    return o.reshape(o.shape[0], o.shape[1], MLA_HEADS * MLA_V)


def mla_attend_blocked(q_nope, q_pe, k_nope, k_pe, v):
    b, n = q_nope.shape[:2]
    nb = n // Q_BLOCK
    qn = q_nope.reshape(b, nb, Q_BLOCK, MLA_HEADS, MLA_NOPE).swapaxes(0, 1)
    qp = q_pe.reshape(b, nb, Q_BLOCK, MLA_HEADS, MLA_ROPE).swapaxes(0, 1)
    o = lax.map(lambda qs: mla_attend(qs[0], qs[1], k_nope, k_pe, v), (qn, qp))
    return o.swapaxes(0, 1).reshape(b, n, MLA_HEADS * MLA_V)


def mla_mixer(fa_lat, fa_ctx, p, rope, need_ctx):
    q_l, kv_l, kr_l = split_cols(fa_lat, (MLA_Q_RANK, MLA_KV_RANK, MLA_ROPE))
    q_c, kv_c, kr_c = split_cols(fa_ctx, (MLA_Q_RANK, MLA_KV_RANK, MLA_ROPE))
    kn_c, kp_c, v_c = mla_keys(kv_c, kr_c, p['mla_kv_norm_g'], p['mla_w_ukv'], None)
    kn_l, kp_l, v_l = mla_keys(kv_l, kr_l, p['mla_kv_norm_g'], p['mla_w_ukv'], rope)
    qn_l, qp_l = mla_queries(q_l, p['mla_q_norm_g'], p['mla_w_uq'], rope)
    k_nope = jnp.concatenate([kn_c, kn_l], axis=1)
    k_pe = jnp.concatenate([kp_c, kp_l], axis=1)
    v = jnp.concatenate([v_c, v_l], axis=1)
    out_l = mla_attend_blocked(qn_l, qp_l, k_nope, k_pe, v)
    out_c = None
    if need_ctx:
        qn_c, qp_c = mla_queries(q_c, p['mla_q_norm_g'], p['mla_w_uq'], None)
        out_c = mla_attend(qn_c, qp_c, kn_c, kp_c, v_c)
    return out_l, out_c


def rwkv_prepare(f, p):
    f = f.astype(jnp.float32)
    b, n, _ = f.shape
    prev, nxt = neighbours(f)
    f = f + p['rwkv_mu'] * (0.5 * (prev + nxt) - f)
    r, k, v, wl, al, gl = split_cols(
        f, (RWKV_WIDTH, RWKV_WIDTH, RWKV_WIDTH, 2 * DECAY_RANK, 2 * ICL_RANK, GATE_RANK))
    wl = wl.reshape(b, n, 2, DECAY_RANK)
    al = al.reshape(b, n, 2, ICL_RANK)
    w = -jax.nn.softplus(-(p['rwkv_w0'] + jnp.einsum('bndr,drc->bndc', jnp.tanh(wl), p['rwkv_w2']))) - 0.5
    decay = jnp.exp(-jnp.exp(w))
    a = jax.nn.sigmoid(p['rwkv_a0'] + jnp.einsum('bndr,drc->bndc', al, p['rwkv_a2']))
    g = jax.nn.sigmoid(gl) @ p['rwkv_g2']

    def heads(t):
        return t.reshape(t.shape[:-1] + (RWKV_HEADS, RWKV_HEAD))

    kk = heads(k * p['rwkv_k_k'])
    kk = kk / jnp.maximum(jnp.sqrt(jnp.sum(kk * kk, axis=-1, keepdims=True)), 1e-12)
    k_dir = heads(k[:, :, None] * (1.0 + (a - 1.0) * p['rwkv_k_a']))
    r_h, v_h = heads(r), heads(v)
    b_dir = kk[:, :, None] * heads(a)
    bonus = jnp.einsum('bnhk,bndhk,hk->bnh', r_h, k_dir, p['rwkv_r_k'])[..., None] * v_h
    return {'r': r_h, 'w': heads(decay), 'k': k_dir, 'v': v_h, 'a': -kk, 'b': b_dir,
            'g': g, 'bonus': bonus}


def dir_inputs(feats, d):
    return (feats['r'], feats['w'][:, :, d], feats['k'][:, :, d], feats['v'], feats['a'], feats['b'][:, :, d])


def wkv_scan(state, r, w, k, v, a, b, reverse, with_out):
    xs = tuple(t.swapaxes(0, 1) for t in (r, w, k, v, a, b))

    def step(S, inp):
        r_t, w_t, k_t, v_t, a_t, b_t = inp
        sa = jnp.einsum('bhvk,bhk->bhv', S, a_t)
        S = S * w_t[:, :, None, :] + sa[..., None] * b_t[:, :, None, :] + v_t[..., None] * k_t[:, :, None, :]
        y = jnp.einsum('bhvk,bhk->bhv', S, r_t) if with_out else None
        return S, y

    state, ys = lax.scan(step, state, xs, reverse=reverse)
    return state, (ys.swapaxes(0, 1) if with_out else None)


def rwkv_output(y, feats, p, dtype):
    b, n = y.shape[:2]
    mean = jnp.mean(y, axis=-1, keepdims=True)
    var = jnp.mean(jnp.square(y - mean), axis=-1, keepdims=True)
    yn = ((y - mean) * lax.rsqrt(var + RWKV_GN_EPS)).reshape(b, n, RWKV_WIDTH)
    yn = yn * p['rwkv_ln_g'] + p['rwkv_ln_b']
    out = (yn + feats['bonus'].reshape(b, n, RWKV_WIDTH)) * feats['g']
    return out.astype(dtype)


def rwkv_mixer(fb_lat, fb_ctx, p, need_ctx):
    dtype = fb_lat.dtype
    fl = rwkv_prepare(fb_lat, p)
    fc = rwkv_prepare(fb_ctx, p)
    zero = jnp.zeros((fb_lat.shape[0], RWKV_HEADS, RWKV_HEAD, RWKV_HEAD), jnp.float32)
    s_cf, y_cf = wkv_scan(zero, *dir_inputs(fc, 0), reverse=False, with_out=need_ctx)
    s_cb, y_cb = wkv_scan(zero, *dir_inputs(fc, 1), reverse=True, with_out=need_ctx)
    _, y_lf = wkv_scan(s_cf, *dir_inputs(fl, 0), reverse=False, with_out=True)
    _, y_lb = wkv_scan(s_cb, *dir_inputs(fl, 1), reverse=True, with_out=True)
    out_l = rwkv_output(y_lf + y_lb, fl, p, dtype)
    out_c = rwkv_output(y_cf + y_cb, fc, p, dtype) if need_ctx else None
    return out_l, out_c


def pool_mixer(u, p):
    b, n, _ = u.shape
    uf = u.astype(jnp.float32)
    cs = jnp.concatenate([jnp.zeros_like(uf[:, :1]), jnp.cumsum(uf, axis=1)], axis=1)
    t = jnp.arange(n)
    groups = []
    for gi, win in enumerate(POOL_WINDOWS):
        sl = slice(gi * POOL_GROUP, (gi + 1) * POOL_GROUP)
        lo = jnp.clip(t - win // 2, 0, n)
        hi = jnp.clip(t + win // 2, 0, n)
        cnt = (hi - lo).astype(jnp.float32)[:, None]
        csg = cs[..., sl]
        mean = (jnp.take(csg, hi, axis=1) - jnp.take(csg, lo, axis=1)) / cnt
        groups.append(mean - uf[..., sl])
    z = jnp.stack(groups, axis=2)
    z = jnp.einsum('bngi,gio->bngo', z, p['pool_w'].astype(jnp.float32)).reshape(b, n, POOL_WIDTH)
    return (z * p['pool_scale']).astype(u.dtype)


def conv_mixer(fd, p):
    gb, gc, hx = split_cols(fd, (CONV_WIDTH, CONV_WIDTH, CONV_WIDTH))
    u = gc * hx
    prev, nxt = neighbours(u)
    w = p['conv_w']
    z = w[0] * prev + w[1] * u + w[2] * nxt
    return gb * z


def sq_relu_mlp(x, w1, w2):
    return jnp.square(jax.nn.relu(x @ w1)) @ w2


def trunk_layer(h, hc, c, c_ctx, p, rope, need_ctx):
    mod = jax.nn.silu(c) @ p['ada_w'] + p['ada_b']
    mod_c = jax.nn.silu(c_ctx) @ p['ada_w'] + p['ada_b']
    sh1, sc1, gt1, sh2, sc2, gt2 = [m[:, None, :] for m in jnp.split(mod, 6, axis=-1)]
    sh1c, sc1c, gt1c, sh2c, sc2c, gt2c = jnp.split(mod_c, 6, axis=-1)

    xn = rmsnorm(h, p['norm1_g']) * (1.0 + sc1) + sh1
    xc = rmsnorm(hc, p['norm1_g']) * (1.0 + sc1c) + sh1c
    col_sizes = (MLA_COLS, RWKV_COLS, POOL_COLS, CONV_COLS)
    fl = split_cols(xn @ p['w_in'], col_sizes)
    fc = split_cols(xc @ p['w_in'], col_sizes)

    att_l, att_c = mla_mixer(fl[0], fc[0], p, rope, need_ctx)
    rw_l, rw_c = rwkv_mixer(fl[1], fc[1], p, need_ctx)
    pool_l = pool_mixer(fl[2], p)
    conv_l = conv_mixer(fl[3], p)
    mixed = jnp.concatenate([att_l, rw_l, pool_l, conv_l], axis=-1) @ p['w_out']
    h = h + gt1 * mixed
    h = h + gt2 * sq_relu_mlp(rmsnorm(h, p['norm2_g']) * (1.0 + sc2) + sh2, p['mlp_w1'], p['mlp_w2'])

    if need_ctx:
        pool_c = pool_mixer(fc[2], p)
        conv_c = conv_mixer(fc[3], p)
        mixed_c = jnp.concatenate([att_c, rw_c, pool_c, conv_c], axis=-1) @ p['w_out']
        hc = hc + gt1c * mixed_c
        hc = hc + gt2c * sq_relu_mlp(rmsnorm(hc, p['norm2_g']) * (1.0 + sc2c) + sh2c, p['mlp_w1'], p['mlp_w2'])
    return h, hc


def setup_inputs(seed: int = 0) -> dict:
    key = jax.random.key(seed)
    ks = jax.random.split(key, 32)
    f32 = jnp.float32

    def nrm(k, shape, scale):
        return scale * jax.random.normal(k, shape, f32)

    L = DEPTH
    return {
        "x": nrm(ks[0], (BATCH, SEQ, D_MODEL), 1.0),
        "c": nrm(ks[1], (BATCH, D_MODEL), 1.0),
        "ctx": nrm(ks[2], (BATCH, CTX_LEN, D_MODEL), 1.0),
        "c_ctx": nrm(ks[3], (D_MODEL,), 1.0),
        "ada_w": nrm(ks[4], (L, D_MODEL, 6 * D_MODEL), 0.5 * D_MODEL ** -0.5),
        "ada_b": nrm(ks[5], (L, 6 * D_MODEL), 0.01),
        "norm1_g": 1.0 + nrm(ks[6], (L, D_MODEL), 0.02),
        "norm2_g": 1.0 + nrm(ks[7], (L, D_MODEL), 0.02),
        "w_in": nrm(ks[8], (L, D_MODEL, IN_COLS), D_MODEL ** -0.5),
        "mla_q_norm_g": 1.0 + nrm(ks[9], (L, MLA_Q_RANK), 0.02),
        "mla_w_uq": nrm(ks[10], (L, MLA_Q_RANK, MLA_HEADS * (MLA_NOPE + MLA_ROPE)), MLA_Q_RANK ** -0.5),
        "mla_kv_norm_g": 1.0 + nrm(ks[11], (L, MLA_KV_RANK), 0.02),
        "mla_w_ukv": nrm(ks[12], (L, MLA_KV_RANK, MLA_HEADS * (MLA_NOPE + MLA_V)), MLA_KV_RANK ** -0.5),
        "rwkv_mu": jax.random.uniform(ks[13], (L, RWKV_COLS), f32),
        "rwkv_w0": jax.random.uniform(ks[14], (L, 2, RWKV_WIDTH), f32, -6.0, -1.0),
        "rwkv_w2": nrm(ks[15], (L, 2, DECAY_RANK, RWKV_WIDTH), 0.5 * DECAY_RANK ** -0.5),
        "rwkv_a0": nrm(ks[16], (L, 2, RWKV_WIDTH), 0.1),
        "rwkv_a2": nrm(ks[17], (L, 2, ICL_RANK, RWKV_WIDTH), 0.5 * ICL_RANK ** -0.5),
        "rwkv_g2": nrm(ks[18], (L, GATE_RANK, RWKV_WIDTH), GATE_RANK ** -0.5),
        "rwkv_k_k": 1.0 + nrm(ks[19], (L, RWKV_WIDTH), 0.1),
        "rwkv_k_a": 1.0 + nrm(ks[20], (L, RWKV_WIDTH), 0.1),
        "rwkv_r_k": nrm(ks[21], (L, RWKV_HEADS, RWKV_HEAD), 0.1),
        "rwkv_ln_g": 1.0 + nrm(ks[22], (L, RWKV_WIDTH), 0.02),
        "rwkv_ln_b": nrm(ks[23], (L, RWKV_WIDTH), 0.01),
        "pool_w": nrm(ks[24], (L, len(POOL_WINDOWS), POOL_GROUP, POOL_GROUP), POOL_GROUP ** -0.5),
        "pool_scale": 1.0 + nrm(ks[25], (L, POOL_WIDTH), 0.1),
        "conv_w": nrm(ks[26], (L, CONV_TAPS, CONV_WIDTH), CONV_TAPS ** -0.5),
        "w_out": nrm(ks[27], (L, MIX_WIDTH, D_MODEL), MIX_WIDTH ** -0.5),
        "mlp_w1": nrm(ks[28], (L, D_MODEL, FF_HIDDEN), D_MODEL ** -0.5),
        "mlp_w2": nrm(ks[29], (L, FF_HIDDEN, D_MODEL), FF_HIDDEN ** -0.5),
        "final_norm_g": 1.0 + nrm(ks[30], (D_MODEL,), 0.02),
    }


def reference(x, c, ctx, c_ctx, ada_w, ada_b, norm1_g, norm2_g, w_in, mla_q_norm_g, mla_w_uq,
              mla_kv_norm_g, mla_w_ukv, rwkv_mu, rwkv_w0, rwkv_w2, rwkv_a0, rwkv_a2, rwkv_g2,
              rwkv_k_k, rwkv_k_a, rwkv_r_k, rwkv_ln_g, rwkv_ln_b, pool_w, pool_scale, conv_w,
              w_out, mlp_w1, mlp_w2, final_norm_g):
    rope = axial_rope_tables(x.shape[1])
    h, hc = x, ctx
    for l in range(DEPTH):
        p = {
            'ada_w': ada_w[l], 'ada_b': ada_b[l], 'norm1_g': norm1_g[l], 'norm2_g': norm2_g[l],
            'w_in': w_in[l], 'mla_q_norm_g': mla_q_norm_g[l], 'mla_w_uq': mla_w_uq[l],
            'mla_kv_norm_g': mla_kv_norm_g[l], 'mla_w_ukv': mla_w_ukv[l],
            'rwkv_mu': rwkv_mu[l], 'rwkv_w0': rwkv_w0[l], 'rwkv_w2': rwkv_w2[l],
            'rwkv_a0': rwkv_a0[l], 'rwkv_a2': rwkv_a2[l], 'rwkv_g2': rwkv_g2[l],
            'rwkv_k_k': rwkv_k_k[l], 'rwkv_k_a': rwkv_k_a[l], 'rwkv_r_k': rwkv_r_k[l],
            'rwkv_ln_g': rwkv_ln_g[l], 'rwkv_ln_b': rwkv_ln_b[l],
            'pool_w': pool_w[l], 'pool_scale': pool_scale[l], 'conv_w': conv_w[l],
            'w_out': w_out[l], 'mlp_w1': mlp_w1[l], 'mlp_w2': mlp_w2[l],
        }
        h, hc = trunk_layer(h, hc, c, c_ctx, p, rope, l < DEPTH - 1)
    return rmsnorm(h, final_norm_g)
```

```python
import functools

import numpy as np
import jax
import jax.numpy as jnp
from jax import lax
from jax.experimental import pallas as pl
from jax.experimental.pallas import tpu as pltpu

F32 = jnp.float32
BF16 = jnp.bfloat16

D_MODEL = 2048
GRID_W = 64
NORM_EPS = 1e-6
GROUP_WIDTH = D_MODEL // 4

MLA_NOPE = 128
MLA_ROPE = 64
MLA_V = 128
MLA_HEADS = GROUP_WIDTH // MLA_V
MLA_Q_RANK = 384
MLA_KV_RANK = 128
ROPE_BASE = 10000.0
MLA_QK_PAD = 256
MLA_FEAT = 640

RWKV_HEAD = 64
RWKV_HEADS = GROUP_WIDTH // RWKV_HEAD
RWKV_WIDTH = RWKV_HEADS * RWKV_HEAD
DECAY_RANK = 96
ICL_RANK = 96
GATE_RANK = 256
RWKV_GN_EPS = 64e-5
RWKV_COLS = 3 * RWKV_WIDTH + 2 * DECAY_RANK + 2 * ICL_RANK + GATE_RANK
WKV_CHUNK = 64
RWKV_PACK_BLOCKS = 11

POOL_WINDOWS = (2, 4, 8, 16)
POOL_WIDTH = GROUP_WIDTH
POOL_GROUP = POOL_WIDTH // len(POOL_WINDOWS)
CONV_WIDTH = GROUP_WIDTH
CONV_COLS = 3 * CONV_WIDTH
FF_HIDDEN = 4 * D_MODEL
MLA_COLS = MLA_Q_RANK + MLA_KV_RANK + MLA_ROPE

ROW_TILE = 256
HALO = 8
MOD_ROWS = 16
VMEM_LIMIT = 56 * 1024 * 1024


def _cparams(sem):
    return pltpu.CompilerParams(dimension_semantics=sem, vmem_limit_bytes=VMEM_LIMIT)


def _resident(shape, index_map):
    return pl.BlockSpec(shape, index_map, pipeline_mode=pl.Buffered(1))


def _rms(x, g):
    return x * lax.rsqrt(jnp.mean(x * x, axis=-1, keepdims=True) + NORM_EPS) * g


def _dot(a, b):
    return jnp.dot(a.astype(BF16), b.astype(BF16), preferred_element_type=F32)


def _split3(x):
    h1 = x.astype(BF16)
    r1 = x - h1.astype(F32)
    h2 = r1.astype(BF16)
    h3 = (r1 - h2.astype(F32)).astype(BF16)
    return h1, h2, h3


def _dot_exact_lhs(a01, x):
    h1, h2, h3 = _split3(x)
    d = lambda h: jnp.dot(a01, h, preferred_element_type=F32)
    return d(h1) + d(h2) + d(h3)


def _dot_exact_rhs(x, b01):
    h1, h2, h3 = _split3(x)
    d = lambda h: jnp.dot(h, b01, preferred_element_type=F32)
    return d(h1) + d(h2) + d(h3)


def _dot_x3(a, b):
    a1 = a.astype(BF16)
    a2 = (a - a1.astype(F32)).astype(BF16)
    b1 = b.astype(BF16)
    b2 = (b - b1.astype(F32)).astype(BF16)
    d = lambda p, q: jnp.dot(p, q, preferred_element_type=F32)
    return d(a1, b1) + (d(a1, b2) + d(a2, b1))


def _mod_kernel(c_ref, w_ref, b_ref, o_ref):
    c = c_ref[...]
    s = c * jax.nn.sigmoid(c)
    o_ref[...] = _dot(s, w_ref[...]) + b_ref[...]


def _modulation(cc, ada_w, ada_b):
    depth = ada_w.shape[0]
    tn = 1024
    return pl.pallas_call(
        _mod_kernel,
        grid=(depth, 6 * D_MODEL // tn),
        in_specs=[
            pl.BlockSpec((MOD_ROWS, D_MODEL), lambda l, j: (0, 0)),
            pl.BlockSpec((None, D_MODEL, tn), lambda l, j: (l, 0, j)),
            pl.BlockSpec((None, 1, tn), lambda l, j: (l, 0, j)),
        ],
        out_specs=pl.BlockSpec((None, MOD_ROWS, tn), lambda l, j: (l, 0, j)),
        out_shape=jax.ShapeDtypeStruct((depth, MOD_ROWS, 6 * D_MODEL), F32),
        compiler_params=_cparams(("parallel", "parallel")),
        name="modulation",
    )(cc, ada_w, ada_b.reshape(depth, 1, 6 * D_MODEL))


def _mod_row(mod_ref, row, k):
    return mod_ref[pl.ds(row, 1), k * D_MODEL:(k + 1) * D_MODEL]


IN_SEGMENTS = (MLA_FEAT, RWKV_COLS, POOL_WIDTH, CONV_COLS)
IN_COLS_PADDED = sum(IN_SEGMENTS)


def _win_kernel(x_ref, mod_ref, g_ref, w_ref, o_mla, o_rw, o_pool, o_conv, *, nb, nct):
    b = pl.program_id(0)
    t = pl.program_id(1)
    row = jnp.where(t < nct, nb, b)
    xn = _rms(x_ref[...], g_ref[...]) * (1.0 + _mod_row(mod_ref, row, 1)) + _mod_row(mod_ref, row, 0)
    xn = xn.astype(BF16)
    base = 0
    for o_ref, width in zip((o_mla, o_rw, o_pool, o_conv), IN_SEGMENTS):
        c0 = 0
        while c0 < width:
            c1 = min(c0 + 512, width)
            o_ref[:, c0:c1] = jnp.dot(xn, w_ref[:, base + c0:base + c1], preferred_element_type=F32)
            c0 = c1
        base += width


def _in_proj(h_all, mod, g, w_p, nb, nct):
    b_, n_, _ = h_all.shape
    nt = n_ // ROW_TILE
    outs = tuple(jax.ShapeDtypeStruct((b_, n_, w), F32) for w in IN_SEGMENTS)
    return pl.pallas_call(
        functools.partial(_win_kernel, nb=nb, nct=nct),
        grid=(b_, nt),
        in_specs=[
            pl.BlockSpec((None, ROW_TILE, D_MODEL), lambda b, t: (b, t, 0)),
            _resident((MOD_ROWS, 6 * D_MODEL), lambda b, t: (0, 0)),
            _resident((1, D_MODEL), lambda b, t: (0, 0)),
            _resident((D_MODEL, IN_COLS_PADDED), lambda b, t: (0, 0)),
        ],
        out_specs=tuple(pl.BlockSpec((None, ROW_TILE, w), lambda b, t: (b, t, 0)) for w in IN_SEGMENTS),
        out_shape=outs,
        compiler_params=_cparams(("parallel", "parallel")),
        name="in_proj",
    )(h_all, mod, g.reshape(1, D_MODEL), w_p)


def _rope(x, cs, sn):
    return x * cs + pltpu.roll(x, 64, 1) * sn


def _mla_prep_kernel(f_ref, cs_ref, sn_ref, gq_ref, gkv_ref, wq_ref, wkv_ref, q_ref, k_ref, v_ref):
    scale = (MLA_NOPE + MLA_ROPE) ** -0.5
    cs = cs_ref[...]
    sn = sn_ref[...]
    q = _dot(_rms(f_ref[:, 0:MLA_Q_RANK], gq_ref[...]), wq_ref[...])
    kv = _dot(_rms(f_ref[:, MLA_Q_RANK:MLA_Q_RANK + MLA_KV_RANK], gkv_ref[...]), wkv_ref[...])
    k_pe = _rope(f_ref[:, 512:640], cs, sn).astype(BF16)
    for h in range(MLA_HEADS):
        o = h * MLA_QK_PAD
        q_ref[:, o:o + 128] = (q[:, o:o + 128] * scale).astype(BF16)
        q_ref[:, o + 128:o + 256] = (_rope(q[:, o + 128:o + 256], cs, sn) * scale).astype(BF16)
        k_ref[:, o:o + 128] = kv[:, o:o + 128].astype(BF16)
        k_ref[:, o + 128:o + 256] = k_pe
        v_ref[:, h * MLA_V:(h + 1) * MLA_V] = kv[:, o + 128:o + 256].astype(BF16)


def _mla_prep(f_mla, cs, sn, gq, gkv, wq, wkv):
    b_, n_, _ = f_mla.shape
    nt = n_ // ROW_TILE
    qk_w = MLA_HEADS * MLA_QK_PAD
    row = lambda w: pl.BlockSpec((None, ROW_TILE, w), lambda b, t: (b, t, 0))
    return pl.pallas_call(
        _mla_prep_kernel,
        grid=(b_, nt),
        in_specs=[
            row(MLA_FEAT),
            pl.BlockSpec((ROW_TILE, 128), lambda b, t: (t, 0)),
            pl.BlockSpec((ROW_TILE, 128), lambda b, t: (t, 0)),
            _resident((1, MLA_Q_RANK), lambda b, t: (0, 0)),
            _resident((1, MLA_KV_RANK), lambda b, t: (0, 0)),
            _resident((MLA_Q_RANK, qk_w), lambda b, t: (0, 0)),
            _resident((MLA_KV_RANK, qk_w), lambda b, t: (0, 0)),
        ],
        out_specs=(row(qk_w), row(qk_w), row(MLA_HEADS * MLA_V)),
        out_shape=(
            jax.ShapeDtypeStruct((b_, n_, qk_w), BF16),
            jax.ShapeDtypeStruct((b_, n_, qk_w), BF16),
            jax.ShapeDtypeStruct((b_, n_, MLA_HEADS * MLA_V), BF16),
        ),
        compiler_params=_cparams(("parallel", "parallel")),
        name="mla_prep",
    )(f_mla, cs, sn, gq.reshape(1, -1), gkv.reshape(1, -1), wq, wkv)


def _attn_kernel(q_ref, k_ref, v_ref, o_ref, *, t_off, nct, n_all):
    t = pl.program_id(1) + t_off

    def attend(nk):
        for h in range(MLA_HEADS):
            q = q_ref[:, h * MLA_QK_PAD:(h + 1) * MLA_QK_PAD]
            k = k_ref[0:nk, h * MLA_QK_PAD:(h + 1) * MLA_QK_PAD]
            s = lax.dot_general(q, k, (((1,), (1,)), ((), ())), preferred_element_type=F32)
            p = jnp.exp(s - jnp.max(s, axis=-1, keepdims=True))
            l = jnp.sum(p, axis=-1, keepdims=True)
            o = jnp.dot(p.astype(BF16), v_ref[0:nk, h * MLA_V:(h + 1) * MLA_V], preferred_element_type=F32)
            o_ref[:, h * MLA_V:(h + 1) * MLA_V] = o / l

    if t_off < nct:
        @pl.when(t < nct)
        def _():
            attend(nct * ROW_TILE)

        @pl.when(t >= nct)
        def _():
            attend(n_all)
    else:
        attend(n_all)


def _attention(q, k, v, nct, with_ctx):
    b_, n_, qk_w = q.shape
    nt = n_ // ROW_TILE
    t_off = 0 if with_ctx else nct
    n_out = n_ - t_off * ROW_TILE
    return pl.pallas_call(
        functools.partial(_attn_kernel, t_off=t_off, nct=nct, n_all=n_),
        grid=(b_, nt - t_off),
        in_specs=[
            pl.BlockSpec((None, ROW_TILE, qk_w), lambda b, t: (b, t + t_off, 0)),
            pl.BlockSpec((None, n_, qk_w), lambda b, t: (b, 0, 0)),
            pl.BlockSpec((None, n_, MLA_HEADS * MLA_V), lambda b, t: (b, 0, 0)),
        ],
        out_specs=pl.BlockSpec((None, ROW_TILE, MLA_HEADS * MLA_V), lambda b, t: (b, t, 0)),
        out_shape=jax.ShapeDtypeStruct((b_, n_out, MLA_HEADS * MLA_V), F32),
        compiler_params=_cparams(("parallel", "arbitrary")),
        name="mla_attention",
    )(q, k, v)


def _halo_specs(width, n_rows, t_off=0):
    per = ROW_TILE // HALO
    last = n_rows // HALO - 1
    cur = pl.BlockSpec((None, ROW_TILE, width), lambda b, t: (b, t + t_off, 0))
    prev = pl.BlockSpec((None, HALO, width), lambda b, t: (b, jnp.maximum((t + t_off) * per - 1, 0), 0))
    nxt = pl.BlockSpec((None, HALO, width), lambda b, t: (b, jnp.minimum((t + t_off + 1) * per, last), 0))
    return cur, prev, nxt


def _seq_edges(t, nct, nt):
    has_prev = jnp.logical_and(t != 0, t != nct)
    has_next = jnp.logical_and(t != nct - 1, t != nt - 1)
    return has_prev, has_next


def _seg_sum(x, e_ref):
    h1 = x.astype(BF16)
    h2 = (x - h1.astype(F32)).astype(BF16)
    e = e_ref[...]
    return jnp.dot(h1, e, preferred_element_type=F32) + jnp.dot(h2, e, preferred_element_type=F32)


def _rwkv_prep_kernel(f_ref, fp_ref, fn_ref, mu_ref, w0_ref, a0_ref, w2_ref, a2_ref, g2_ref,
                      kk_ref, ka_ref, rk_ref, e_ref, o_ref, buf, *, nct, nt):
    t = pl.program_id(1)
    has_prev, has_next = _seq_edges(t, nct, nt)
    buf[0:HALO] = jnp.where(has_prev, fp_ref[...], 0.0)
    buf[HALO:HALO + ROW_TILE] = f_ref[...]
    buf[HALO + ROW_TILE:2 * HALO + ROW_TILE] = jnp.where(has_next, fn_ref[...], 0.0)
    f = f_ref[...]
    prev = buf[HALO - 1:HALO - 1 + ROW_TILE]
    nxt = buf[HALO + 1:HALO + 1 + ROW_TILE]
    f = f + mu_ref[...] * (0.5 * (prev + nxt) - f)

    w = RWKV_WIDTH
    r, k, v = f[:, 0:w], f[:, w:2 * w], f[:, 2 * w:3 * w]
    xw = w0_ref[...] + _dot(jnp.tanh(f[:, 1536:1792]), w2_ref[...])
    lw = -float(np.exp(-0.5)) * jax.nn.sigmoid(xw)
    a = jax.nn.sigmoid(a0_ref[...] + _dot(f[:, 1664:1920], a2_ref[...]))
    g = _dot(jax.nn.sigmoid(f[:, 1920:2176]), g2_ref[...])

    kkv = k * kk_ref[...]
    kk = kkv / jnp.maximum(jnp.sqrt(_seg_sum(kkv * kkv, e_ref)), 1e-12)
    rk = r * rk_ref[...]
    ka = ka_ref[...]
    k_f = k * (1.0 + (a[:, 0:w] - 1.0) * ka)
    k_b = k * (1.0 + (a[:, w:2 * w] - 1.0) * ka)
    bonus = _seg_sum(rk * k_f + rk * k_b, e_ref) * v

    for i, val in enumerate((r, v, kk, g, bonus, lw[:, 0:w], k_f, a[:, 0:w], lw[:, w:2 * w], k_b, a[:, w:2 * w])):
        o_ref[:, i * w:(i + 1) * w] = val


def _rwkv_prep(f_rw, p, nct):
    b_, n_, _ = f_rw.shape
    nt = n_ // ROW_TILE
    cur, prev, nxt = _halo_specs(RWKV_COLS, n_)
    full = lambda a: _resident(a.shape, lambda b, t: (0,) * a.ndim)
    consts = (p["mu"], p["w0"], p["a0"], p["w2"], p["a2"], p["g2"], p["k_k"], p["k_a"], p["r_k"], p["seg"])
    pack_w = RWKV_PACK_BLOCKS * RWKV_WIDTH
    return pl.pallas_call(
        functools.partial(_rwkv_prep_kernel, nct=nct, nt=nt),
        grid=(b_, nt),
        in_specs=[cur, prev, nxt] + [full(a) for a in consts],
        out_specs=pl.BlockSpec((None, ROW_TILE, pack_w), lambda b, t: (b, t, 0)),
        out_shape=jax.ShapeDtypeStruct((b_, n_, pack_w), F32),
        scratch_shapes=[pltpu.VMEM((ROW_TILE + 2 * HALO, RWKV_COLS), F32)],
        compiler_params=_cparams(("parallel", "parallel")),
        name="rwkv_prep",
    )(f_rw, f_rw, f_rw, *consts)


def _bd(x, m0):
    return jnp.concatenate([jnp.where(m0, x, 0.0), jnp.where(m0, 0.0, x)], axis=0)


def _wkv_kernel(r_ref, v_ref, kk_ref, lw_ref, kd_ref, a_ref, y_ref, s_ref, rw_s, y0_s, m_s, n_s, *, rev):
    c_ = WKV_CHUNK
    nsub = ROW_TILE // c_
    npair = RWKV_WIDTH // 128

    @pl.when(pl.program_id(1) == 0)
    def _():
        s_ref[...] = jnp.zeros_like(s_ref)

    ri = lax.broadcasted_iota(jnp.int32, (c_, c_), 0)
    ci = lax.broadcasted_iota(jnp.int32, (c_, c_), 1)
    tmat = ((ri <= ci) if rev else (ri >= ci)).astype(BF16)
    rc = lax.broadcasted_iota(jnp.int32, (c_, 128), 0)
    cc = lax.broadcasted_iota(jnp.int32, (c_, 128), 1) % c_
    strict = (rc < cc) if rev else (rc > cc)
    incl = (rc <= cc) if rev else (rc >= cc)
    eye = rc == cc
    m0_128 = lax.broadcasted_iota(jnp.int32, (1, 128), 1) < c_
    m0_256 = (lax.broadcasted_iota(jnp.int32, (1, 256), 1) % 128) < c_
    zeros_cv = jnp.zeros((c_, 128), F32)

    def phase1(c, carry):
        rows = pl.ds(pl.multiple_of(c * c_, c_), c_)
        lw = lw_ref[rows, :]
        lc = _dot_exact_lhs(tmat, lw)
        ltot = jnp.sum(lw, axis=0, keepdims=True)
        e_ex = jnp.exp(lc - lw)
        e_in = jnp.exp(lc)
        e_neg = jnp.exp(-lc)
        e_end = jnp.exp(ltot - lc)
        e_tot = jnp.exp(ltot)
        kk = kk_ref[rows, :]
        kd = kd_ref[rows, :]
        bv = kk * a_ref[rows, :]
        v = v_ref[rows, :]
        at = -kk * e_ex
        rt = r_ref[rows, :] * e_in
        bt = bv * e_neg
        kt = kd * e_neg
        bh = bv * e_end
        kh = kd * e_end
        for p in range(npair):
            sl = slice(p * 128, (p + 1) * 128)
            ar = jnp.concatenate([at[:, sl], rt[:, sl]], axis=0).astype(BF16)
            bd = jnp.concatenate([_bd(bt[:, sl], m0_128), _bd(kt[:, sl], m0_128)], axis=0).astype(BF16)
            g = lax.dot_general(ar, bd, (((1,), (1,)), ((), ())), preferred_element_type=F32)
            a_ab = jnp.where(strict, g[0:c_, 0:128], 0.0)
            a_ak = jnp.where(strict, g[0:c_, 128:256], 0.0)
            a_rb = jnp.where(incl, g[c_:2 * c_, 0:128], 0.0)
            a_rk = jnp.where(incl, g[c_:2 * c_, 128:256], 0.0)
            vp = v[:, sl]
            av = _dot(jnp.concatenate([a_ak, a_rk], axis=0), _bd(vp, m0_128))
            x = jnp.concatenate([at[:, sl], av[0:c_]], axis=1)
            pw = a_ab
            for it in range(6):
                x = x + _dot(pw, _bd(x, m0_256))
                if it < 5:
                    pw = _dot(pw, _bd(pw, m0_128))
            ry = _dot(a_rb, _bd(x, m0_256))
            rw_s[c, p] = rt[:, sl] + ry[:, 0:128]
            y0_s[c, p] = ry[:, 128:256] + av[c_:2 * c_]
            bkt = jnp.concatenate([bh[:, sl], kh[:, sl]], axis=0).T
            rhs = jnp.concatenate([x, jnp.concatenate([zeros_cv, vp], axis=1)], axis=0)
            z = _dot(bkt, rhs)
            m_s[c, p] = (jnp.where(m0_128, z[0:c_, 0:128], z[c_:2 * c_, 0:128])
                         + jnp.where(eye, e_tot[:, sl], 0.0))
            n_s[c, p] = jnp.where(m0_128, z[0:c_, 128:256], z[c_:2 * c_, 128:256])
        return carry

    lax.fori_loop(0, nsub, phase1, 0)

    def phase2(i, carry):
        c = (nsub - 1 - i) if rev else i
        rows = pl.ds(pl.multiple_of(c * c_, c_), c_)
        for p in range(npair):
            lhs = jnp.concatenate([rw_s[c, p], m_s[c, p]], axis=0)
            o = _dot_x3(lhs, _bd(s_ref[p], m0_128))
            y_ref[rows, p * 128:(p + 1) * 128] = o[0:c_] + y0_s[c, p]
            s_ref[p] = o[c_:2 * c_] + n_s[c, p]
        return carry

    lax.fori_loop(0, nsub, phase2, 0)


def _wkv(pack, nct, rev):
    b_, n_, _ = pack.shape
    nt = n_ // ROW_TILE
    w = RWKV_WIDTH
    nsub = ROW_TILE // WKV_CHUNK
    npair = w // 128
    if rev:
        order = lambda s: jnp.where(s < nct, nct - 1 - s, nt + nct - 1 - s)
    else:
        order = lambda s: s
    col = lambda j: pl.BlockSpec((None, ROW_TILE, w), lambda b, s: (b, order(s), j))
    d = 3 if rev else 0
    scr = lambda: pltpu.VMEM((nsub, npair, WKV_CHUNK, 128), F32)
    return pl.pallas_call(
        functools.partial(_wkv_kernel, rev=rev),
        grid=(b_, nt),
        in_specs=[col(0), col(1), col(2), col(5 + d), col(6 + d), col(7 + d)],
        out_specs=pl.BlockSpec((None, ROW_TILE, w), lambda b, s: (b, order(s), 0)),
        out_shape=jax.ShapeDtypeStruct((b_, n_, w), F32),
        scratch_shapes=[pltpu.VMEM((npair, WKV_CHUNK, 128), F32), scr(), scr(), scr(), scr()],
        compiler_params=_cparams(("parallel", "arbitrary")),
        name="wkv_bwd" if rev else "wkv_fwd",
    )(pack, pack, pack, pack, pack, pack)


def _rwkv_out_kernel(yf_ref, yb_ref, g_ref, bonus_ref, lg_ref, lb_ref, e_ref, o_ref):
    y = yf_ref[...] + yb_ref[...]
    mean = _seg_sum(y, e_ref) * (1.0 / RWKV_HEAD)
    d = y - mean
    var = _seg_sum(d * d, e_ref) * (1.0 / RWKV_HEAD)
    yn = d * lax.rsqrt(var + RWKV_GN_EPS) * lg_ref[...] + lb_ref[...]
    o_ref[...] = (yn + bonus_ref[...]) * g_ref[...]


def _rwkv_out(yf, yb, pack, p, nct, with_ctx):
    b_, n_, w = yf.shape
    nt = n_ // ROW_TILE
    t_off = 0 if with_ctx else nct
    n_out = n_ - t_off * ROW_TILE
    blk = lambda j: pl.BlockSpec((None, ROW_TILE, w), lambda b, t: (b, t + t_off, j))
    full = lambda a: _resident(a.shape, lambda b, t: (0,) * a.ndim)
    return pl.pallas_call(
        _rwkv_out_kernel,
        grid=(b_, nt - t_off),
        in_specs=[blk(0), blk(0), blk(3), blk(4), full(p["ln_g"]), full(p["ln_b"]), full(p["seg"])],
        out_specs=pl.BlockSpec((None, ROW_TILE, w), lambda b, t: (b, t, 0)),
        out_shape=jax.ShapeDtypeStruct((b_, n_out, w), F32),
        compiler_params=_cparams(("parallel", "parallel")),
        name="rwkv_out",
    )(yf, yb, pack, pack, p["ln_g"], p["ln_b"], p["seg"])


def _pool_conv_kernel(fp_ref, fpp_ref, fpn_ref, fc_ref, fcp_ref, fcn_ref, pw_ref, ps_ref, cw_ref,
                      op_ref, oc_ref, pbuf, ubuf, *, t_off, nct, nt, n_ctx, n_seq):
    t = pl.program_id(1) + t_off
    has_prev, has_next = _seq_edges(t, nct, nt)
    lo, hi = HALO, HALO + ROW_TILE

    pbuf[0:lo] = jnp.where(has_prev, fpp_ref[...], 0.0)
    pbuf[lo:hi] = fp_ref[...]
    pbuf[hi:hi + HALO] = jnp.where(has_next, fpn_ref[...], 0.0)
    in_ctx = t < nct
    pos = lax.broadcasted_iota(jnp.int32, (ROW_TILE, 1), 0) + jnp.where(in_ctx, t, t - nct) * ROW_TILE
    n_own = jnp.where(in_ctx, n_ctx, n_seq)
    for gi, win in enumerate(POOL_WINDOWS):
        hw = win // 2
        cols = slice(gi * POOL_GROUP, (gi + 1) * POOL_GROUP)
        s = pbuf[lo - hw:hi - hw, cols]
        for o in range(-hw + 1, hw):
            s = s + pbuf[lo + o:hi + o, cols]
        cnt = (jnp.minimum(pos + hw, n_own) - jnp.maximum(pos - hw, 0)).astype(F32)
        z = s / cnt - fp_ref[:, cols]
        op_ref[:, cols] = _dot(z, pw_ref[gi]) * ps_ref[:, cols]

    cw = CONV_WIDTH
    prod = lambda ref: ref[:, cw:2 * cw] * ref[:, 2 * cw:3 * cw]
    u = prod(fc_ref)
    ubuf[0:lo] = jnp.where(has_prev, prod(fcp_ref), 0.0)
    ubuf[lo:hi] = u
    ubuf[hi:hi + HALO] = jnp.where(has_next, prod(fcn_ref), 0.0)
    z = cw_ref[0:1, :] * ubuf[lo - 1:hi - 1] + cw_ref[1:2, :] * u + cw_ref[2:3, :] * ubuf[lo + 1:hi + 1]
    oc_ref[...] = fc_ref[:, 0:cw] * z


def _pool_conv(f_pool, f_conv, pool_w, pool_scale, conv_w, nct, n_ctx, with_ctx):
    b_, n_, _ = f_pool.shape
    nt = n_ // ROW_TILE
    t_off = 0 if with_ctx else nct
    n_out = n_ - t_off * ROW_TILE
    full = lambda a: _resident(a.shape, lambda b, t: (0,) * a.ndim)
    out = lambda w: pl.BlockSpec((None, ROW_TILE, w), lambda b, t: (b, t, 0))
    return pl.pallas_call(
        functools.partial(_pool_conv_kernel, t_off=t_off, nct=nct, nt=nt, n_ctx=n_ctx, n_seq=n_ - n_ctx),
        grid=(b_, nt - t_off),
        in_specs=[*_halo_specs(POOL_WIDTH, n_, t_off), *_halo_specs(CONV_COLS, n_, t_off),
                  full(pool_w), full(pool_scale), full(conv_w)],
        out_specs=(out(POOL_WIDTH), out(CONV_WIDTH)),
        out_shape=(jax.ShapeDtypeStruct((b_, n_out, POOL_WIDTH), F32),
                   jax.ShapeDtypeStruct((b_, n_out, CONV_WIDTH), F32)),
        scratch_shapes=[pltpu.VMEM((ROW_TILE + 2 * HALO, POOL_WIDTH), F32),
                        pltpu.VMEM((ROW_TILE + 2 * HALO, CONV_WIDTH), F32)],
        compiler_params=_cparams(("parallel", "parallel")),
        name="pool_conv",
    )(f_pool, f_pool, f_pool, f_conv, f_conv, f_conv, pool_w, pool_scale, conv_w)


def _wout_kernel(att_ref, rw_ref, pool_ref, conv_ref, h_ref, mod_ref, w_ref, o_ref, *, nb, nct, t_off):
    b = pl.program_id(0)
    t = pl.program_id(1) + t_off
    row = jnp.where(t < nct, nb, b)
    gw = GROUP_WIDTH
    acc = _dot(att_ref[...], w_ref[0:gw, :])
    acc = acc + _dot(rw_ref[...], w_ref[gw:2 * gw, :])
    acc = acc + _dot(pool_ref[...], w_ref[2 * gw:3 * gw, :])
    acc = acc + _dot(conv_ref[...], w_ref[3 * gw:4 * gw, :])
    o_ref[...] = h_ref[...] + _mod_row(mod_ref, row, 2) * acc


def _out_proj(att, rw, pool, conv, h_all, mod, w_out, nb, nct, with_ctx):
    b_, n_out, gw = att.shape
    t_off = 0 if with_ctx else nct
    mix = pl.BlockSpec((None, ROW_TILE, gw), lambda b, t: (b, t, 0))
    return pl.pallas_call(
        functools.partial(_wout_kernel, nb=nb, nct=nct, t_off=t_off),
        grid=(b_, n_out // ROW_TILE),
        in_specs=[mix, mix, mix, mix,
                  pl.BlockSpec((None, ROW_TILE, D_MODEL), lambda b, t: (b, t + t_off, 0)),
                  _resident((MOD_ROWS, 6 * D_MODEL), lambda b, t: (0, 0)),
                  _resident((D_MODEL, D_MODEL), lambda b, t: (0, 0))],
        out_specs=pl.BlockSpec((None, ROW_TILE, D_MODEL), lambda b, t: (b, t, 0)),
        out_shape=jax.ShapeDtypeStruct((b_, n_out, D_MODEL), F32),
        compiler_params=_cparams(("parallel", "parallel")),
        name="out_proj",
    )(att, rw, pool, conv, h_all, mod, w_out)


MLP_HIDDEN_TILE = 512


def _mlp_kernel(h_ref, mod_ref, g_ref, w1_ref, w2_ref, gf_ref, o_ref, xn_s, acc_s, *, nb, ctx_rows, final):
    b = pl.program_id(0)
    t = pl.program_id(1)
    j = pl.program_id(2)
    first_row = jnp.where(t == 0, nb, b) if ctx_rows else b

    def per_rows(fn):
        if ctx_rows == 0:
            fn(slice(None), b)
        elif ctx_rows == h_ref.shape[0]:
            fn(slice(None), first_row)
        else:
            fn(slice(0, ctx_rows), first_row)
            fn(slice(ctx_rows, None), b)

    @pl.when(j == 0)
    def _():
        y = _rms(h_ref[...], g_ref[...])

        def put(rows, mrow):
            xn_s[rows, :] = (y[rows, :] * (1.0 + _mod_row(mod_ref, mrow, 4)) + _mod_row(mod_ref, mrow, 3)).astype(BF16)

        per_rows(put)
        acc_s[...] = jnp.zeros_like(acc_s)

    hid = jnp.dot(xn_s[...], w1_ref[...], preferred_element_type=F32)
    hid = jnp.square(jnp.maximum(hid, 0.0))
    acc_s[...] += jnp.dot(hid.astype(BF16), w2_ref[...], preferred_element_type=F32)

    @pl.when(j == pl.num_programs(2) - 1)
    def _():
        def put(rows, mrow):
            o = h_ref[rows, :] + _mod_row(mod_ref, mrow, 5) * acc_s[rows, :]
            if final:
                o = _rms(o, gf_ref[...])
            o_ref[rows, :] = o

        per_rows(put)


def _mlp(h, mod, g, w1, w2, g_final, nb, tm, ctx_rows, final):
    b_, n_, _ = h.shape
    th = MLP_HIDDEN_TILE
    return pl.pallas_call(
        functools.partial(_mlp_kernel, nb=nb, ctx_rows=ctx_rows, final=final),
        grid=(b_, n_ // tm, FF_HIDDEN // th),
        in_specs=[
            pl.BlockSpec((None, tm, D_MODEL), lambda b, t, j: (b, t, 0)),
            _resident((MOD_ROWS, 6 * D_MODEL), lambda b, t, j: (0, 0)),
            _resident((1, D_MODEL), lambda b, t, j: (0, 0)),
            pl.BlockSpec((D_MODEL, th), lambda b, t, j: (0, j)),
            pl.BlockSpec((th, D_MODEL), lambda b, t, j: (j, 0)),
            _resident((1, D_MODEL), lambda b, t, j: (0, 0)),
        ],
        out_specs=pl.BlockSpec((None, tm, D_MODEL), lambda b, t, j: (b, t, 0)),
        out_shape=jax.ShapeDtypeStruct((b_, n_, D_MODEL), F32),
        scratch_shapes=[pltpu.VMEM((tm, D_MODEL), BF16), pltpu.VMEM((tm, D_MODEL), F32)],
        compiler_params=_cparams(("parallel", "parallel", "arbitrary")),
        name="mlp",
    )(h, mod, g.reshape(1, D_MODEL), w1, w2, g_final.reshape(1, D_MODEL))


def _rope_partner_perm():
    q = MLA_ROPE // 4
    i = np.arange(MLA_ROPE)
    return np.where((i // q) % 2 == 0, i + q, i - q)


def _rope_tables(n_ctx, n_seq):
    rows = n_seq // GRID_W
    row = jnp.repeat(jnp.arange(rows), GRID_W)
    col = jnp.tile(jnp.arange(GRID_W), rows)
    pos = jnp.stack([row, col], axis=-1).astype(F32)
    axis_dim = MLA_ROPE // 2
    inv_freq = ROPE_BASE ** (-jnp.arange(0, axis_dim, 2, dtype=F32) / axis_dim)
    ang = pos[:, :, None] * inv_freq
    cos, sin = jnp.cos(ang), jnp.sin(ang)
    cos64 = jnp.concatenate([cos[:, 0], cos[:, 0], cos[:, 1], cos[:, 1]], axis=-1)
    sin64 = jnp.concatenate([-sin[:, 0], sin[:, 0], -sin[:, 1], sin[:, 1]], axis=-1)
    pad = jnp.zeros((n_seq, 128 - MLA_ROPE), F32)
    cs_lat = jnp.concatenate([cos64, pad], axis=-1)
    sn_lat = jnp.concatenate([sin64, pad], axis=-1)
    cs_ctx = jnp.concatenate([jnp.ones((n_ctx, MLA_ROPE), F32), jnp.zeros((n_ctx, 128 - MLA_ROPE), F32)], axis=-1)
    return jnp.concatenate([cs_ctx, cs_lat], axis=0), jnp.concatenate([jnp.zeros((n_ctx, 128), F32), sn_lat], axis=0)


def _layer_params(l, w_in, mla_w_uq, mla_w_ukv, rwkv_mu, rwkv_w0, rwkv_w2, rwkv_a0, rwkv_a2, rwkv_g2,
                  rwkv_k_k, rwkv_k_a, rwkv_r_k, rwkv_ln_g, rwkv_ln_b, seg):
    perm = _rope_partner_perm()
    w = w_in[l]
    kr0 = MLA_Q_RANK + MLA_KV_RANK
    w_in_p = jnp.concatenate([w[:, :MLA_COLS], w[:, kr0:MLA_COLS][:, perm], w[:, MLA_COLS:]], axis=1).astype(BF16)

    wq = mla_w_uq[l].reshape(MLA_Q_RANK, MLA_HEADS, MLA_NOPE + MLA_ROPE)
    wq = jnp.concatenate([wq, wq[:, :, MLA_NOPE:][:, :, perm]], axis=-1).reshape(MLA_Q_RANK, MLA_HEADS * MLA_QK_PAD)

    wd = RWKV_WIDTH
    z = lambda r: jnp.zeros((r, wd), F32)
    w2 = jnp.concatenate([
        jnp.concatenate([rwkv_w2[l, 0], z(DECAY_RANK), z(256 - 2 * DECAY_RANK)], axis=0),
        jnp.concatenate([z(DECAY_RANK), rwkv_w2[l, 1], z(256 - 2 * DECAY_RANK)], axis=0)], axis=1)
    lead = 256 - 2 * ICL_RANK
    a2 = jnp.concatenate([
        jnp.concatenate([z(lead), rwkv_a2[l, 0], z(ICL_RANK)], axis=0),
        jnp.concatenate([z(lead), z(ICL_RANK), rwkv_a2[l, 1]], axis=0)], axis=1)
    rw = {
        "mu": rwkv_mu[l].reshape(1, RWKV_COLS),
        "w0": rwkv_w0[l].reshape(1, 2 * wd),
        "a0": rwkv_a0[l].reshape(1, 2 * wd),
        "w2": w2.astype(BF16),
        "a2": a2.astype(BF16),
        "g2": rwkv_g2[l].astype(BF16),
        "k_k": rwkv_k_k[l].reshape(1, wd),
        "k_a": rwkv_k_a[l].reshape(1, wd),
        "r_k": rwkv_r_k[l].reshape(1, wd),
        "ln_g": rwkv_ln_g[l].reshape(1, wd),
        "ln_b": rwkv_ln_b[l].reshape(1, wd),
        "seg": seg,
    }
    return w_in_p, wq.astype(BF16), mla_w_ukv[l].astype(BF16), rw


def kernel(x, c, ctx, c_ctx, ada_w, ada_b, norm1_g, norm2_g, w_in, mla_q_norm_g, mla_w_uq, mla_kv_norm_g, mla_w_ukv, rwkv_mu, rwkv_w0, rwkv_w2, rwkv_a0, rwkv_a2, rwkv_g2, rwkv_k_k, rwkv_k_a, rwkv_r_k, rwkv_ln_g, rwkv_ln_b, pool_w, pool_scale, conv_w, w_out, mlp_w1, mlp_w2, final_norm_g):
    nb, n_seq, d = x.shape
    n_ctx = ctx.shape[1]
    depth = ada_w.shape[0]
    assert d == D_MODEL and nb < MOD_ROWS
    assert n_ctx % ROW_TILE == 0 and n_seq % ROW_TILE == 0 and n_seq % GRID_W == 0
    nct = n_ctx // ROW_TILE

    cc = jnp.zeros((MOD_ROWS, D_MODEL), F32).at[:nb].set(c).at[nb].set(c_ctx)
    mod = _modulation(cc, ada_w, ada_b)
    cs, sn = _rope_tables(n_ctx, n_seq)
    head = np.arange(RWKV_WIDTH) // RWKV_HEAD
    seg = jnp.asarray(head[:, None] == head[None, :], dtype=BF16)

    h_all = jnp.concatenate([ctx, x], axis=1)
    for l in range(depth):
        last = l == depth - 1
        with_ctx = not last
        w_in_p, wq, wkv, rw_p = _layer_params(
            l, w_in, mla_w_uq, mla_w_ukv, rwkv_mu, rwkv_w0, rwkv_w2, rwkv_a0, rwkv_a2, rwkv_g2,
            rwkv_k_k, rwkv_k_a, rwkv_r_k, rwkv_ln_g, rwkv_ln_b, seg)

        f_mla, f_rw, f_pool, f_conv = _in_proj(h_all, mod[l], norm1_g[l], w_in_p, nb, nct)

        q, k, v = _mla_prep(f_mla, cs, sn, mla_q_norm_g[l], mla_kv_norm_g[l], wq, wkv)
        att = _attention(q, k, v, nct, with_ctx)

        pack = _rwkv_prep(f_rw, rw_p, nct)
        y_f = _wkv(pack, nct, rev=False)
        y_b = _wkv(pack, nct, rev=True)
        rw = _rwkv_out(y_f, y_b, pack, rw_p, nct, with_ctx)

        pool, conv = _pool_conv(f_pool, f_conv, pool_w[l].astype(BF16), pool_scale[l].reshape(1, POOL_WIDTH),
                                conv_w[l], nct, n_ctx, with_ctx)

        h_mid = _out_proj(att, rw, pool, conv, h_all, mod[l], w_out[l].astype(BF16), nb, nct, with_ctx)
        n_rows = h_mid.shape[1]
        if with_ctx:
            tm = 3 * ROW_TILE if n_rows % (3 * ROW_TILE) == 0 else ROW_TILE
            ctx_rows = n_ctx
            assert n_ctx <= tm, "context longer than one MLP row tile"
        else:
            tm = 2 * ROW_TILE if n_rows % (2 * ROW_TILE) == 0 else ROW_TILE
            ctx_rows = 0
        h_all = _mlp(h_mid, mod[l], norm2_g[l], mlp_w1[l].astype(BF16), mlp_w2[l].astype(BF16),
                     final_norm_g, nb, tm, ctx_rows, last)
    return h_all
```

```python
import functools

import numpy as np
import jax
import jax.numpy as jnp
from jax import lax
from jax.experimental import pallas as pl
from jax.experimental.pallas import tpu as pltpu

F32 = jnp.float32
BF16 = jnp.bfloat16

D_MODEL = 2048
GRID_W = 64
NORM_EPS = 1e-6
GROUP_WIDTH = D_MODEL // 4

MLA_NOPE = 128
MLA_ROPE = 64
MLA_V = 128
MLA_HEADS = GROUP_WIDTH // MLA_V
MLA_Q_RANK = 384
MLA_KV_RANK = 128
ROPE_BASE = 10000.0
MLA_QK_PAD = 256
MLA_FEAT = 640

RWKV_HEAD = 64
RWKV_HEADS = GROUP_WIDTH // RWKV_HEAD
RWKV_WIDTH = RWKV_HEADS * RWKV_HEAD
DECAY_RANK = 96
ICL_RANK = 96
GATE_RANK = 256
RWKV_GN_EPS = 64e-5
RWKV_COLS = 3 * RWKV_WIDTH + 2 * DECAY_RANK + 2 * ICL_RANK + GATE_RANK
WKV_CHUNK = 64
RWKV_PACK_BLOCKS = 11

POOL_WINDOWS = (2, 4, 8, 16)
POOL_WIDTH = GROUP_WIDTH
POOL_GROUP = POOL_WIDTH // len(POOL_WINDOWS)
CONV_WIDTH = GROUP_WIDTH
CONV_COLS = 3 * CONV_WIDTH
FF_HIDDEN = 4 * D_MODEL
MLA_COLS = MLA_Q_RANK + MLA_KV_RANK + MLA_ROPE

ROW_TILE = 256
HALO = 8
MOD_ROWS = 16
VMEM_LIMIT = 56 * 1024 * 1024


def _cparams(sem):
    return pltpu.CompilerParams(dimension_semantics=sem, vmem_limit_bytes=VMEM_LIMIT)


def _resident(shape, index_map):
    return pl.BlockSpec(shape, index_map, pipeline_mode=pl.Buffered(1))


def _rms(x, g):
    return x * lax.rsqrt(jnp.mean(x * x, axis=-1, keepdims=True) + NORM_EPS) * g


def _dot(a, b):
    return jnp.dot(a.astype(BF16), b.astype(BF16), preferred_element_type=F32)


def _dot_nt(a, b):
    return lax.dot_general(a.astype(BF16), b.astype(BF16), (((1,), (1,)), ((), ())), preferred_element_type=F32)


def _split3(x):
    h1 = x.astype(BF16)
    r1 = x - h1.astype(F32)
    h2 = r1.astype(BF16)
    h3 = (r1 - h2.astype(F32)).astype(BF16)
    return h1, h2, h3


def _dot_exact_lhs(a01, x):
    h1, h2, h3 = _split3(x)
    d = lambda h: jnp.dot(a01, h, preferred_element_type=F32)
    return d(h1) + d(h2) + d(h3)


def _dot_exact_rhs(x, b01):
    h1, h2, h3 = _split3(x)
    d = lambda h: jnp.dot(h, b01, preferred_element_type=F32)
    return d(h1) + d(h2) + d(h3)


def _dot_x3(a, b):
    a1 = a.astype(BF16)
    a2 = (a - a1.astype(F32)).astype(BF16)
    b1 = b.astype(BF16)
    b2 = (b - b1.astype(F32)).astype(BF16)
    m = a.shape[0]
    t = jnp.dot(jnp.concatenate([a1, a2], axis=0), b1, preferred_element_type=F32)
    return t[0:m] + (t[m:2 * m] + jnp.dot(a1, b2, preferred_element_type=F32))


def _mod_kernel(c_ref, w_ref, b_ref, o_ref):
    c = c_ref[...]
    s = c * jax.nn.sigmoid(c)
    o_ref[...] = _dot(s, w_ref[...]) + b_ref[...]


def _modulation(cc, ada_w, ada_b):
    depth = ada_w.shape[0]
    tn = 1024
    return pl.pallas_call(
        _mod_kernel,
        grid=(depth, 6 * D_MODEL // tn),
        in_specs=[
            pl.BlockSpec((MOD_ROWS, D_MODEL), lambda l, j: (0, 0)),
            pl.BlockSpec((None, D_MODEL, tn), lambda l, j: (l, 0, j)),
            pl.BlockSpec((None, 1, tn), lambda l, j: (l, 0, j)),
        ],
        out_specs=pl.BlockSpec((None, MOD_ROWS, tn), lambda l, j: (l, 0, j)),
        out_shape=jax.ShapeDtypeStruct((depth, MOD_ROWS, 6 * D_MODEL), F32),
        compiler_params=_cparams(("parallel", "parallel")),
        name="modulation",
    )(cc, ada_w, ada_b.reshape(depth, 1, 6 * D_MODEL))


def _mod_row(mod_ref, row, k):
    return mod_ref[pl.ds(row, 1), k * D_MODEL:(k + 1) * D_MODEL]


IN_SEGMENTS = (MLA_FEAT, RWKV_COLS, POOL_WIDTH, CONV_COLS)
IN_COLS_PADDED = sum(IN_SEGMENTS)


def _win_kernel(x_ref, mod_ref, g_ref, w_ref, o_mla, o_rw, o_pool, o_conv, *, nb, nct):
    b = pl.program_id(0)
    t = pl.program_id(1)
    row = jnp.where(t < nct, nb, b)
    xn = _rms(x_ref[...], g_ref[...]) * (1.0 + _mod_row(mod_ref, row, 1)) + _mod_row(mod_ref, row, 0)
    xn = xn.astype(BF16)
    base = 0
    for o_ref, width in zip((o_mla, o_rw, o_pool, o_conv), IN_SEGMENTS):
        c0 = 0
        while c0 < width:
            c1 = min(c0 + 512, width)
            o_ref[:, c0:c1] = jnp.dot(xn, w_ref[:, base + c0:base + c1], preferred_element_type=F32)
            c0 = c1
        base += width


def _in_proj(h_all, mod, g, w_p, nb, nct):
    b_, n_, _ = h_all.shape
    nt = n_ // ROW_TILE
    outs = tuple(jax.ShapeDtypeStruct((b_, n_, w), F32) for w in IN_SEGMENTS)
    return pl.pallas_call(
        functools.partial(_win_kernel, nb=nb, nct=nct),
        grid=(b_, nt),
        in_specs=[
            pl.BlockSpec((None, ROW_TILE, D_MODEL), lambda b, t: (b, t, 0)),
            _resident((MOD_ROWS, 6 * D_MODEL), lambda b, t: (0, 0)),
            _resident((1, D_MODEL), lambda b, t: (0, 0)),
            _resident((D_MODEL, IN_COLS_PADDED), lambda b, t: (0, 0)),
        ],
        out_specs=tuple(pl.BlockSpec((None, ROW_TILE, w), lambda b, t: (b, t, 0)) for w in IN_SEGMENTS),
        out_shape=outs,
        compiler_params=_cparams(("parallel", "parallel")),
        name="in_proj",
    )(h_all, mod, g.reshape(1, D_MODEL), w_p)


def _rope(x, cs, sn):
    return x * cs + pltpu.roll(x, 64, 1) * sn


def _mla_prep_kernel(f_ref, cs_ref, sn_ref, gq_ref, gkv_ref, wq_ref, wkv_ref, q_ref, k_ref, v_ref):
    scale = (MLA_NOPE + MLA_ROPE) ** -0.5
    cs = cs_ref[...]
    sn = sn_ref[...]
    q = _dot(_rms(f_ref[:, 0:MLA_Q_RANK], gq_ref[...]), wq_ref[...])
    kv = _dot(_rms(f_ref[:, MLA_Q_RANK:MLA_Q_RANK + MLA_KV_RANK], gkv_ref[...]), wkv_ref[...])
    k_pe = _rope(f_ref[:, 512:640], cs, sn).astype(BF16)
    for h in range(MLA_HEADS):
        o = h * MLA_QK_PAD
        q_ref[:, o:o + 128] = (q[:, o:o + 128] * scale).astype(BF16)
        q_ref[:, o + 128:o + 256] = (_rope(q[:, o + 128:o + 256], cs, sn) * scale).astype(BF16)
        k_ref[:, o:o + 128] = kv[:, o:o + 128].astype(BF16)
        k_ref[:, o + 128:o + 256] = k_pe
        v_ref[:, h * MLA_V:(h + 1) * MLA_V] = kv[:, o + 128:o + 256].astype(BF16)


def _mla_prep(f_mla, cs, sn, gq, gkv, wq, wkv):
    b_, n_, _ = f_mla.shape
    nt = n_ // ROW_TILE
    qk_w = MLA_HEADS * MLA_QK_PAD
    row = lambda w: pl.BlockSpec((None, ROW_TILE, w), lambda b, t: (b, t, 0))
    return pl.pallas_call(
        _mla_prep_kernel,
        grid=(b_, nt),
        in_specs=[
            row(MLA_FEAT),
            pl.BlockSpec((ROW_TILE, 128), lambda b, t: (t, 0)),
            pl.BlockSpec((ROW_TILE, 128), lambda b, t: (t, 0)),
            _resident((1, MLA_Q_RANK), lambda b, t: (0, 0)),
            _resident((1, MLA_KV_RANK), lambda b, t: (0, 0)),
            _resident((MLA_Q_RANK, qk_w), lambda b, t: (0, 0)),
            _resident((MLA_KV_RANK, qk_w), lambda b, t: (0, 0)),
        ],
        out_specs=(row(qk_w), row(qk_w), row(MLA_HEADS * MLA_V)),
        out_shape=(
            jax.ShapeDtypeStruct((b_, n_, qk_w), BF16),
            jax.ShapeDtypeStruct((b_, n_, qk_w), BF16),
            jax.ShapeDtypeStruct((b_, n_, MLA_HEADS * MLA_V), BF16),
        ),
        compiler_params=_cparams(("parallel", "parallel")),
        name="mla_prep",
    )(f_mla, cs, sn, gq.reshape(1, -1), gkv.reshape(1, -1), wq, wkv)


def _attn_kernel(q_ref, k_ref, v_ref, o_ref, *, t_off, nct, n_all):
    t = pl.program_id(1) + t_off

    def attend(nk):
        for h in range(MLA_HEADS):
            q = q_ref[:, h * MLA_QK_PAD:(h + 1) * MLA_QK_PAD]
            k = k_ref[0:nk, h * MLA_QK_PAD:(h + 1) * MLA_QK_PAD]
            s = lax.dot_general(q, k, (((1,), (1,)), ((), ())), preferred_element_type=F32)
            p = jnp.exp(s - jnp.max(s, axis=-1, keepdims=True))
            l = jnp.sum(p, axis=-1, keepdims=True)
            o = jnp.dot(p.astype(BF16), v_ref[0:nk, h * MLA_V:(h + 1) * MLA_V], preferred_element_type=F32)
            o_ref[:, h * MLA_V:(h + 1) * MLA_V] = o / l

    if t_off < nct:
        @pl.when(t < nct)
        def _():
            attend(nct * ROW_TILE)

        @pl.when(t >= nct)
        def _():
            attend(n_all)
    else:
        attend(n_all)


def _attention(q, k, v, nct, with_ctx):
    b_, n_, qk_w = q.shape
    nt = n_ // ROW_TILE
    t_off = 0 if with_ctx else nct
    n_out = n_ - t_off * ROW_TILE
    return pl.pallas_call(
        functools.partial(_attn_kernel, t_off=t_off, nct=nct, n_all=n_),
        grid=(b_, nt - t_off),
        in_specs=[
            pl.BlockSpec((None, ROW_TILE, qk_w), lambda b, t: (b, t + t_off, 0)),
            pl.BlockSpec((None, n_, qk_w), lambda b, t: (b, 0, 0)),
            pl.BlockSpec((None, n_, MLA_HEADS * MLA_V), lambda b, t: (b, 0, 0)),
        ],
        out_specs=pl.BlockSpec((None, ROW_TILE, MLA_HEADS * MLA_V), lambda b, t: (b, t, 0)),
        out_shape=jax.ShapeDtypeStruct((b_, n_out, MLA_HEADS * MLA_V), F32),
        compiler_params=_cparams(("parallel", "arbitrary")),
        name="mla_attention",
    )(q, k, v)


def _halo_specs(width, n_rows, t_off=0):
    per = ROW_TILE // HALO
    last = n_rows // HALO - 1
    cur = pl.BlockSpec((None, ROW_TILE, width), lambda b, t: (b, t + t_off, 0))
    prev = pl.BlockSpec((None, HALO, width), lambda b, t: (b, jnp.maximum((t + t_off) * per - 1, 0), 0))
    nxt = pl.BlockSpec((None, HALO, width), lambda b, t: (b, jnp.minimum((t + t_off + 1) * per, last), 0))
    return cur, prev, nxt


def _seq_edges(t, nct, nt):
    has_prev = jnp.logical_and(t != 0, t != nct)
    has_next = jnp.logical_and(t != nct - 1, t != nt - 1)
    return has_prev, has_next


def _seg_sum(x, e_ref):
    h1 = x.astype(BF16)
    h2 = (x - h1.astype(F32)).astype(BF16)
    e = e_ref[...]
    return jnp.dot(h1, e, preferred_element_type=F32) + jnp.dot(h2, e, preferred_element_type=F32)


def _rwkv_prep_kernel(f_ref, fp_ref, fn_ref, mu_ref, w0_ref, a0_ref, w2_ref, a2_ref, g2_ref,
                      kk_ref, ka_ref, rk_ref, e_ref, o_ref, buf, *, nct, nt):
    t = pl.program_id(1)
    has_prev, has_next = _seq_edges(t, nct, nt)
    buf[0:HALO] = jnp.where(has_prev, fp_ref[...], 0.0)
    buf[HALO:HALO + ROW_TILE] = f_ref[...]
    buf[HALO + ROW_TILE:2 * HALO + ROW_TILE] = jnp.where(has_next, fn_ref[...], 0.0)
    f = f_ref[...]
    prev = buf[HALO - 1:HALO - 1 + ROW_TILE]
    nxt = buf[HALO + 1:HALO + 1 + ROW_TILE]
    f = f + mu_ref[...] * (0.5 * (prev + nxt) - f)

    w = RWKV_WIDTH
    r, k, v = f[:, 0:w], f[:, w:2 * w], f[:, 2 * w:3 * w]
    xw = w0_ref[...] + _dot(jnp.tanh(f[:, 1536:1792]), w2_ref[...])
    lw = -float(np.exp(-0.5)) * jax.nn.sigmoid(xw)
    a = jax.nn.sigmoid(a0_ref[...] + _dot(f[:, 1664:1920], a2_ref[...]))
    g = _dot(jax.nn.sigmoid(f[:, 1920:2176]), g2_ref[...])

    kkv = k * kk_ref[...]
    kk = kkv / jnp.maximum(jnp.sqrt(_seg_sum(kkv * kkv, e_ref)), 1e-12)
    rk = r * rk_ref[...]
    ka = ka_ref[...]
    k_f = k * (1.0 + (a[:, 0:w] - 1.0) * ka)
    k_b = k * (1.0 + (a[:, w:2 * w] - 1.0) * ka)
    bonus = _seg_sum(rk * k_f + rk * k_b, e_ref) * v

    for i, val in enumerate((r, v, kk, g, bonus, lw[:, 0:w], k_f, a[:, 0:w], lw[:, w:2 * w], k_b, a[:, w:2 * w])):
        o_ref[:, i * w:(i + 1) * w] = val


def _rwkv_prep(f_rw, p, nct):
    b_, n_, _ = f_rw.shape
    nt = n_ // ROW_TILE
    cur, prev, nxt = _halo_specs(RWKV_COLS, n_)
    full = lambda a: _resident(a.shape, lambda b, t: (0,) * a.ndim)
    consts = (p["mu"], p["w0"], p["a0"], p["w2"], p["a2"], p["g2"], p["k_k"], p["k_a"], p["r_k"], p["seg"])
    pack_w = RWKV_PACK_BLOCKS * RWKV_WIDTH
    return pl.pallas_call(
        functools.partial(_rwkv_prep_kernel, nct=nct, nt=nt),
        grid=(b_, nt),
        in_specs=[cur, prev, nxt] + [full(a) for a in consts],
        out_specs=pl.BlockSpec((None, ROW_TILE, pack_w), lambda b, t: (b, t, 0)),
        out_shape=jax.ShapeDtypeStruct((b_, n_, pack_w), F32),
        scratch_shapes=[pltpu.VMEM((ROW_TILE + 2 * HALO, RWKV_COLS), F32)],
        compiler_params=_cparams(("parallel", "parallel")),
        name="rwkv_prep",
    )(f_rw, f_rw, f_rw, *consts)


def _bd(x, m0):
    return jnp.concatenate([jnp.where(m0, x, 0.0), jnp.where(m0, 0.0, x)], axis=0)


def _wkv_kernel(r_ref, v_ref, kk_ref, lw_ref, kd_ref, a_ref, y_ref, s_ref, rw_s, y0_s, m_s, n_s, *, rev):
    c_ = WKV_CHUNK
    nsub = ROW_TILE // c_
    npair = RWKV_WIDTH // 128

    @pl.when(pl.program_id(1) == 0)
    def _():
        s_ref[...] = jnp.zeros_like(s_ref)

    ri = lax.broadcasted_iota(jnp.int32, (c_, c_), 0)
    ci = lax.broadcasted_iota(jnp.int32, (c_, c_), 1)
    tmat = ((ri <= ci) if rev else (ri >= ci)).astype(BF16)
    rc = lax.broadcasted_iota(jnp.int32, (c_, 128), 0)
    cc = lax.broadcasted_iota(jnp.int32, (c_, 128), 1) % c_
    strict = (rc < cc) if rev else (rc > cc)
    incl = (rc <= cc) if rev else (rc >= cc)
    eye = rc == cc
    m0_128 = lax.broadcasted_iota(jnp.int32, (1, 128), 1) < c_
    m0_256 = (lax.broadcasted_iota(jnp.int32, (1, 256), 1) % 128) < c_
    zeros_cv = jnp.zeros((c_, 128), F32)

    items = [(c, p) for c in range(nsub) for p in range(npair)]
    at, rt, bt, kt, bh, kh, vv, e_tot = {}, {}, {}, {}, {}, {}, {}, {}
    for c in range(nsub):
        rows = slice(c * c_, (c + 1) * c_)
        lw = lw_ref[rows, :]
        lc = _dot_exact_lhs(tmat, lw)
        ltot = jnp.sum(lw, axis=0, keepdims=True)
        e_neg = jnp.exp(-lc)
        e_end = jnp.exp(ltot - lc)
        kk = kk_ref[rows, :]
        kd = kd_ref[rows, :]
        bv = kk * a_ref[rows, :]
        vv[c] = v_ref[rows, :]
        at[c] = -kk * jnp.exp(lc - lw)
        rt[c] = r_ref[rows, :] * jnp.exp(lc)
        bt[c], kt[c] = bv * e_neg, kd * e_neg
        bh[c], kh[c] = bv * e_end, kd * e_end
        e_tot[c] = jnp.exp(ltot)
    sl = lambda p: slice(p * 128, (p + 1) * 128)

    g = {}
    for c, p in items:
        ar = jnp.concatenate([at[c][:, sl(p)], rt[c][:, sl(p)]], axis=0)
        bd = jnp.concatenate([_bd(bt[c][:, sl(p)], m0_128), _bd(kt[c][:, sl(p)], m0_128)], axis=0)
        g[c, p] = _dot_nt(ar, bd)
    av, x, pw, a_rb = {}, {}, {}, {}
    for c, p in items:
        a_ak = jnp.where(strict, g[c, p][0:c_, 128:256], 0.0)
        a_rk = jnp.where(incl, g[c, p][c_:2 * c_, 128:256], 0.0)
        av[c, p] = _dot(jnp.concatenate([a_ak, a_rk], axis=0), _bd(vv[c][:, sl(p)], m0_128))
        pw[c, p] = jnp.where(strict, g[c, p][0:c_, 0:128], 0.0)
        a_rb[c, p] = jnp.where(incl, g[c, p][c_:2 * c_, 0:128], 0.0)
    for c, p in items:
        x[c, p] = jnp.concatenate([at[c][:, sl(p)], av[c, p][0:c_]], axis=1)
    for it in range(6):
        dot = _dot_x3 if it < 2 else _dot
        for c, p in items:
            x[c, p] = x[c, p] + dot(pw[c, p], _bd(x[c, p], m0_256))
        if it < 5:
            for c, p in items:
                pw[c, p] = dot(pw[c, p], _bd(pw[c, p], m0_128))
    for c, p in items:
        ry = _dot(a_rb[c, p], _bd(x[c, p], m0_256))
        rw_s[c, p] = rt[c][:, sl(p)] + ry[:, 0:128]
        y0_s[c, p] = ry[:, 128:256] + av[c, p][c_:2 * c_]
    for c, p in items:
        bkt = jnp.concatenate([bh[c][:, sl(p)], kh[c][:, sl(p)]], axis=0).T
        rhs = jnp.concatenate([x[c, p], jnp.concatenate([zeros_cv, vv[c][:, sl(p)]], axis=1)], axis=0)
        z = _dot(bkt, rhs)
        m_s[c, p] = (jnp.where(m0_128, z[0:c_, 0:128], z[c_:2 * c_, 0:128])
                     + jnp.where(eye, e_tot[c][:, sl(p)], 0.0))
        n_s[c, p] = jnp.where(m0_128, z[0:c_, 128:256], z[c_:2 * c_, 128:256])

    for i in range(nsub):
        c = (nsub - 1 - i) if rev else i
        for p in range(npair):
            lhs = jnp.concatenate([rw_s[c, p], m_s[c, p]], axis=0)
            o = _dot_x3(lhs, _bd(s_ref[p], m0_128))
            y_ref[c * c_:(c + 1) * c_, sl(p)] = o[0:c_] + y0_s[c, p]
            s_ref[p] = o[c_:2 * c_] + n_s[c, p]


def _wkv(pack, nct, rev):
    b_, n_, _ = pack.shape
    nt = n_ // ROW_TILE
    w = RWKV_WIDTH
    nsub = ROW_TILE // WKV_CHUNK
    npair = w // 128
    if rev:
        order = lambda s: jnp.where(s < nct, nct - 1 - s, nt + nct - 1 - s)
    else:
        order = lambda s: s
    col = lambda j: pl.BlockSpec((None, ROW_TILE, w), lambda b, s: (b, order(s), j))
    d = 3 if rev else 0
    scr = lambda: pltpu.VMEM((nsub, npair, WKV_CHUNK, 128), F32)
    return pl.pallas_call(
        functools.partial(_wkv_kernel, rev=rev),
        grid=(b_, nt),
        in_specs=[col(0), col(1), col(2), col(5 + d), col(6 + d), col(7 + d)],
        out_specs=pl.BlockSpec((None, ROW_TILE, w), lambda b, s: (b, order(s), 0)),
        out_shape=jax.ShapeDtypeStruct((b_, n_, w), F32),
        scratch_shapes=[pltpu.VMEM((npair, WKV_CHUNK, 128), F32), scr(), scr(), scr(), scr()],
        compiler_params=_cparams(("parallel", "arbitrary")),
        name="wkv_bwd" if rev else "wkv_fwd",
    )(pack, pack, pack, pack, pack, pack)


def _rwkv_out_kernel(yf_ref, yb_ref, g_ref, bonus_ref, lg_ref, lb_ref, e_ref, o_ref):
    y = yf_ref[...] + yb_ref[...]
    mean = _seg_sum(y, e_ref) * (1.0 / RWKV_HEAD)
    d = y - mean
    var = _seg_sum(d * d, e_ref) * (1.0 / RWKV_HEAD)
    yn = d * lax.rsqrt(var + RWKV_GN_EPS) * lg_ref[...] + lb_ref[...]
    o_ref[...] = (yn + bonus_ref[...]) * g_ref[...]


def _rwkv_out(yf, yb, pack, p, nct, with_ctx):
    b_, n_, w = yf.shape
    nt = n_ // ROW_TILE
    t_off = 0 if with_ctx else nct
    n_out = n_ - t_off * ROW_TILE
    blk = lambda j: pl.BlockSpec((None, ROW_TILE, w), lambda b, t: (b, t + t_off, j))
    full = lambda a: _resident(a.shape, lambda b, t: (0,) * a.ndim)
    return pl.pallas_call(
        _rwkv_out_kernel,
        grid=(b_, nt - t_off),
        in_specs=[blk(0), blk(0), blk(3), blk(4), full(p["ln_g"]), full(p["ln_b"]), full(p["seg"])],
        out_specs=pl.BlockSpec((None, ROW_TILE, w), lambda b, t: (b, t, 0)),
        out_shape=jax.ShapeDtypeStruct((b_, n_out, w), F32),
        compiler_params=_cparams(("parallel", "parallel")),
        name="rwkv_out",
    )(yf, yb, pack, pack, p["ln_g"], p["ln_b"], p["seg"])


def _pool_conv_kernel(fp_ref, fpp_ref, fpn_ref, fc_ref, fcp_ref, fcn_ref, pw_ref, ps_ref, cw_ref,
                      op_ref, oc_ref, pbuf, ubuf, *, t_off, nct, nt, n_ctx, n_seq):
    t = pl.program_id(1) + t_off
    has_prev, has_next = _seq_edges(t, nct, nt)
    lo, hi = HALO, HALO + ROW_TILE

    pbuf[0:lo] = jnp.where(has_prev, fpp_ref[...], 0.0)
    pbuf[lo:hi] = fp_ref[...]
    pbuf[hi:hi + HALO] = jnp.where(has_next, fpn_ref[...], 0.0)
    in_ctx = t < nct
    pos = lax.broadcasted_iota(jnp.int32, (ROW_TILE, 1), 0) + jnp.where(in_ctx, t, t - nct) * ROW_TILE
    n_own = jnp.where(in_ctx, n_ctx, n_seq)
    for gi, win in enumerate(POOL_WINDOWS):
        hw = win // 2
        cols = slice(gi * POOL_GROUP, (gi + 1) * POOL_GROUP)
        s = pbuf[lo - hw:hi - hw, cols]
        for o in range(-hw + 1, hw):
            s = s + pbuf[lo + o:hi + o, cols]
        cnt = (jnp.minimum(pos + hw, n_own) - jnp.maximum(pos - hw, 0)).astype(F32)
        z = s / cnt - fp_ref[:, cols]
        op_ref[:, cols] = _dot(z, pw_ref[gi]) * ps_ref[:, cols]

    cw = CONV_WIDTH
    prod = lambda ref: ref[:, cw:2 * cw] * ref[:, 2 * cw:3 * cw]
    u = prod(fc_ref)
    ubuf[0:lo] = jnp.where(has_prev, prod(fcp_ref), 0.0)
    ubuf[lo:hi] = u
    ubuf[hi:hi + HALO] = jnp.where(has_next, prod(fcn_ref), 0.0)
    z = cw_ref[0:1, :] * ubuf[lo - 1:hi - 1] + cw_ref[1:2, :] * u + cw_ref[2:3, :] * ubuf[lo + 1:hi + 1]
    oc_ref[...] = fc_ref[:, 0:cw] * z


def _pool_conv(f_pool, f_conv, pool_w, pool_scale, conv_w, nct, n_ctx, with_ctx):
    b_, n_, _ = f_pool.shape
    nt = n_ // ROW_TILE
    t_off = 0 if with_ctx else nct
    n_out = n_ - t_off * ROW_TILE
    full = lambda a: _resident(a.shape, lambda b, t: (0,) * a.ndim)
    out = lambda w: pl.BlockSpec((None, ROW_TILE, w), lambda b, t: (b, t, 0))
    return pl.pallas_call(
        functools.partial(_pool_conv_kernel, t_off=t_off, nct=nct, nt=nt, n_ctx=n_ctx, n_seq=n_ - n_ctx),
        grid=(b_, nt - t_off),
        in_specs=[*_halo_specs(POOL_WIDTH, n_, t_off), *_halo_specs(CONV_COLS, n_, t_off),
                  full(pool_w), full(pool_scale), full(conv_w)],
        out_specs=(out(POOL_WIDTH), out(CONV_WIDTH)),
        out_shape=(jax.ShapeDtypeStruct((b_, n_out, POOL_WIDTH), F32),
                   jax.ShapeDtypeStruct((b_, n_out, CONV_WIDTH), F32)),
        scratch_shapes=[pltpu.VMEM((ROW_TILE + 2 * HALO, POOL_WIDTH), F32),
                        pltpu.VMEM((ROW_TILE + 2 * HALO, CONV_WIDTH), F32)],
        compiler_params=_cparams(("parallel", "parallel")),
        name="pool_conv",
    )(f_pool, f_pool, f_pool, f_conv, f_conv, f_conv, pool_w, pool_scale, conv_w)


def _wout_kernel(att_ref, rw_ref, pool_ref, conv_ref, h_ref, mod_ref, w_ref, o_ref, *, nb, nct, t_off):
    b = pl.program_id(0)
    t = pl.program_id(1) + t_off
    row = jnp.where(t < nct, nb, b)
    gw = GROUP_WIDTH
    acc = _dot(att_ref[...], w_ref[0:gw, :])
    acc = acc + _dot(rw_ref[...], w_ref[gw:2 * gw, :])
    acc = acc + _dot(pool_ref[...], w_ref[2 * gw:3 * gw, :])
    acc = acc + _dot(conv_ref[...], w_ref[3 * gw:4 * gw, :])
    o_ref[...] = h_ref[...] + _mod_row(mod_ref, row, 2) * acc


def _out_proj(att, rw, pool, conv, h_all, mod, w_out, nb, nct, with_ctx):
    b_, n_out, gw = att.shape
    t_off = 0 if with_ctx else nct
    mix = pl.BlockSpec((None, ROW_TILE, gw), lambda b, t: (b, t, 0))
    return pl.pallas_call(
        functools.partial(_wout_kernel, nb=nb, nct=nct, t_off=t_off),
        grid=(b_, n_out // ROW_TILE),
        in_specs=[mix, mix, mix, mix,
                  pl.BlockSpec((None, ROW_TILE, D_MODEL), lambda b, t: (b, t + t_off, 0)),
                  _resident((MOD_ROWS, 6 * D_MODEL), lambda b, t: (0, 0)),
                  _resident((D_MODEL, D_MODEL), lambda b, t: (0, 0))],
        out_specs=pl.BlockSpec((None, ROW_TILE, D_MODEL), lambda b, t: (b, t, 0)),
        out_shape=jax.ShapeDtypeStruct((b_, n_out, D_MODEL), F32),
        compiler_params=_cparams(("parallel", "parallel")),
        name="out_proj",
    )(att, rw, pool, conv, h_all, mod, w_out)


MLP_HIDDEN_TILE = 512


def _mlp_kernel(h_ref, mod_ref, g_ref, w1_ref, w2_ref, gf_ref, o_ref, xn_s, acc_s, *, nb, ctx_rows, final):
    b = pl.program_id(0)
    t = pl.program_id(1)
    j = pl.program_id(2)
    first_row = jnp.where(t == 0, nb, b) if ctx_rows else b

    def per_rows(fn):
        if ctx_rows == 0:
            fn(slice(None), b)
        elif ctx_rows == h_ref.shape[0]:
            fn(slice(None), first_row)
        else:
            fn(slice(0, ctx_rows), first_row)
            fn(slice(ctx_rows, None), b)

    @pl.when(j == 0)
    def _():
        y = _rms(h_ref[...], g_ref[...])

        def put(rows, mrow):
            xn_s[rows, :] = (y[rows, :] * (1.0 + _mod_row(mod_ref, mrow, 4)) + _mod_row(mod_ref, mrow, 3)).astype(BF16)

        per_rows(put)
        acc_s[...] = jnp.zeros_like(acc_s)

    hid = jnp.dot(xn_s[...], w1_ref[...], preferred_element_type=F32)
    hid = jnp.square(jnp.maximum(hid, 0.0))
    acc_s[...] += jnp.dot(hid.astype(BF16), w2_ref[...], preferred_element_type=F32)

    @pl.when(j == pl.num_programs(2) - 1)
    def _():
        def put(rows, mrow):
            o = h_ref[rows, :] + _mod_row(mod_ref, mrow, 5) * acc_s[rows, :]
            if final:
                o = _rms(o, gf_ref[...])
            o_ref[rows, :] = o

        per_rows(put)


def _mlp(h, mod, g, w1, w2, g_final, nb, tm, ctx_rows, final):
    b_, n_, _ = h.shape
    th = MLP_HIDDEN_TILE
    return pl.pallas_call(
        functools.partial(_mlp_kernel, nb=nb, ctx_rows=ctx_rows, final=final),
        grid=(b_, n_ // tm, FF_HIDDEN // th),
        in_specs=[
            pl.BlockSpec((None, tm, D_MODEL), lambda b, t, j: (b, t, 0)),
            _resident((MOD_ROWS, 6 * D_MODEL), lambda b, t, j: (0, 0)),
            _resident((1, D_MODEL), lambda b, t, j: (0, 0)),
            pl.BlockSpec((D_MODEL, th), lambda b, t, j: (0, j)),
            pl.BlockSpec((th, D_MODEL), lambda b, t, j: (j, 0)),
            _resident((1, D_MODEL), lambda b, t, j: (0, 0)),
        ],
        out_specs=pl.BlockSpec((None, tm, D_MODEL), lambda b, t, j: (b, t, 0)),
        out_shape=jax.ShapeDtypeStruct((b_, n_, D_MODEL), F32),
        scratch_shapes=[pltpu.VMEM((tm, D_MODEL), BF16), pltpu.VMEM((tm, D_MODEL), F32)],
        compiler_params=_cparams(("parallel", "parallel", "arbitrary")),
        name="mlp",
    )(h, mod, g.reshape(1, D_MODEL), w1, w2, g_final.reshape(1, D_MODEL))


def _rope_partner_perm():
    q = MLA_ROPE // 4
    i = np.arange(MLA_ROPE)
    return np.where((i // q) % 2 == 0, i + q, i - q)


def _rope_tables(n_ctx, n_seq):
    rows = n_seq // GRID_W
    row = jnp.repeat(jnp.arange(rows), GRID_W)
    col = jnp.tile(jnp.arange(GRID_W), rows)
    pos = jnp.stack([row, col], axis=-1).astype(F32)
    axis_dim = MLA_ROPE // 2
    inv_freq = ROPE_BASE ** (-jnp.arange(0, axis_dim, 2, dtype=F32) / axis_dim)
    ang = pos[:, :, None] * inv_freq
    cos, sin = jnp.cos(ang), jnp.sin(ang)
    cos64 = jnp.concatenate([cos[:, 0], cos[:, 0], cos[:, 1], cos[:, 1]], axis=-1)
    sin64 = jnp.concatenate([-sin[:, 0], sin[:, 0], -sin[:, 1], sin[:, 1]], axis=-1)
    pad = jnp.zeros((n_seq, 128 - MLA_ROPE), F32)
    cs_lat = jnp.concatenate([cos64, pad], axis=-1)
    sn_lat = jnp.concatenate([sin64, pad], axis=-1)
    cs_ctx = jnp.concatenate([jnp.ones((n_ctx, MLA_ROPE), F32), jnp.zeros((n_ctx, 128 - MLA_ROPE), F32)], axis=-1)
    return jnp.concatenate([cs_ctx, cs_lat], axis=0), jnp.concatenate([jnp.zeros((n_ctx, 128), F32), sn_lat], axis=0)


def _layer_params(l, w_in, mla_w_uq, mla_w_ukv, rwkv_mu, rwkv_w0, rwkv_w2, rwkv_a0, rwkv_a2, rwkv_g2,
                  rwkv_k_k, rwkv_k_a, rwkv_r_k, rwkv_ln_g, rwkv_ln_b, seg):
    perm = _rope_partner_perm()
    w = w_in[l]
    kr0 = MLA_Q_RANK + MLA_KV_RANK
    w_in_p = jnp.concatenate([w[:, :MLA_COLS], w[:, kr0:MLA_COLS][:, perm], w[:, MLA_COLS:]], axis=1).astype(BF16)

    wq = mla_w_uq[l].reshape(MLA_Q_RANK, MLA_HEADS, MLA_NOPE + MLA_ROPE)
    wq = jnp.concatenate([wq, wq[:, :, MLA_NOPE:][:, :, perm]], axis=-1).reshape(MLA_Q_RANK, MLA_HEADS * MLA_QK_PAD)

    wd = RWKV_WIDTH
    z = lambda r: jnp.zeros((r, wd), F32)
    w2 = jnp.concatenate([
        jnp.concatenate([rwkv_w2[l, 0], z(DECAY_RANK), z(256 - 2 * DECAY_RANK)], axis=0),
        jnp.concatenate([z(DECAY_RANK), rwkv_w2[l, 1], z(256 - 2 * DECAY_RANK)], axis=0)], axis=1)
    lead = 256 - 2 * ICL_RANK
    a2 = jnp.concatenate([
        jnp.concatenate([z(lead), rwkv_a2[l, 0], z(ICL_RANK)], axis=0),
        jnp.concatenate([z(lead), z(ICL_RANK), rwkv_a2[l, 1]], axis=0)], axis=1)
    rw = {
        "mu": rwkv_mu[l].reshape(1, RWKV_COLS),
        "w0": rwkv_w0[l].reshape(1, 2 * wd),
        "a0": rwkv_a0[l].reshape(1, 2 * wd),
        "w2": w2.astype(BF16),
        "a2": a2.astype(BF16),
        "g2": rwkv_g2[l].astype(BF16),
        "k_k": rwkv_k_k[l].reshape(1, wd),
        "k_a": rwkv_k_a[l].reshape(1, wd),
        "r_k": rwkv_r_k[l].reshape(1, wd),
        "ln_g": rwkv_ln_g[l].reshape(1, wd),
        "ln_b": rwkv_ln_b[l].reshape(1, wd),
        "seg": seg,
    }
    return w_in_p, wq.astype(BF16), mla_w_ukv[l].astype(BF16), rw


def kernel(x, c, ctx, c_ctx, ada_w, ada_b, norm1_g, norm2_g, w_in, mla_q_norm_g, mla_w_uq, mla_kv_norm_g, mla_w_ukv, rwkv_mu, rwkv_w0, rwkv_w2, rwkv_a0, rwkv_a2, rwkv_g2, rwkv_k_k, rwkv_k_a, rwkv_r_k, rwkv_ln_g, rwkv_ln_b, pool_w, pool_scale, conv_w, w_out, mlp_w1, mlp_w2, final_norm_g):
    nb, n_seq, d = x.shape
    n_ctx = ctx.shape[1]
    depth = ada_w.shape[0]
    assert d == D_MODEL and nb < MOD_ROWS
    assert n_ctx % ROW_TILE == 0 and n_seq % ROW_TILE == 0 and n_seq % GRID_W == 0
    nct = n_ctx // ROW_TILE

    cc = jnp.zeros((MOD_ROWS, D_MODEL), F32).at[:nb].set(c).at[nb].set(c_ctx)
    mod = _modulation(cc, ada_w, ada_b)
    cs, sn = _rope_tables(n_ctx, n_seq)
    head = np.arange(RWKV_WIDTH) // RWKV_HEAD
    seg = jnp.asarray(head[:, None] == head[None, :], dtype=BF16)

    h_all = jnp.concatenate([ctx, x], axis=1)
    for l in range(depth):
        last = l == depth - 1
        with_ctx = not last
        w_in_p, wq, wkv, rw_p = _layer_params(
            l, w_in, mla_w_uq, mla_w_ukv, rwkv_mu, rwkv_w0, rwkv_w2, rwkv_a0, rwkv_a2, rwkv_g2,
            rwkv_k_k, rwkv_k_a, rwkv_r_k, rwkv_ln_g, rwkv_ln_b, seg)

        f_mla, f_rw, f_pool, f_conv = _in_proj(h_all, mod[l], norm1_g[l], w_in_p, nb, nct)

        q, k, v = _mla_prep(f_mla, cs, sn, mla_q_norm_g[l], mla_kv_norm_g[l], wq, wkv)
        att = _attention(q, k, v, nct, with_ctx)

        pack = _rwkv_prep(f_rw, rw_p, nct)
        y_f = _wkv(pack, nct, rev=False)
        y_b = _wkv(pack, nct, rev=True)
        rw = _rwkv_out(y_f, y_b, pack, rw_p, nct, with_ctx)

        pool, conv = _pool_conv(f_pool, f_conv, pool_w[l].astype(BF16), pool_scale[l].reshape(1, POOL_WIDTH),
                                conv_w[l], nct, n_ctx, with_ctx)

        h_mid = _out_proj(att, rw, pool, conv, h_all, mod[l], w_out[l].astype(BF16), nb, nct, with_ctx)
        n_rows = h_mid.shape[1]
        if with_ctx:
            tm = 3 * ROW_TILE if n_rows % (3 * ROW_TILE) == 0 else ROW_TILE
            ctx_rows = n_ctx
            assert n_ctx <= tm, "context longer than one MLP row tile"
        else:
            tm = 2 * ROW_TILE if n_rows % (2 * ROW_TILE) == 0 else ROW_TILE
            ctx_rows = 0
        h_all = _mlp(h_mid, mod[l], norm2_g[l], mlp_w1[l].astype(BF16), mlp_w2[l].astype(BF16),
                     final_norm_g, nb, tm, ctx_rows, last)
    return h_all
```

```python
import functools

import numpy as np
import jax
import jax.numpy as jnp
from jax import lax
from jax.experimental import pallas as pl
from jax.experimental.pallas import tpu as pltpu

F32 = jnp.float32
BF16 = jnp.bfloat16

D_MODEL = 2048
GRID_W = 64
NORM_EPS = 1e-6
GROUP_WIDTH = D_MODEL // 4

MLA_NOPE = 128
MLA_ROPE = 64
MLA_V = 128
MLA_HEADS = GROUP_WIDTH // MLA_V
MLA_Q_RANK = 384
MLA_KV_RANK = 128
ROPE_BASE = 10000.0
MLA_QK_PAD = 256
MLA_FEAT = 640

RWKV_HEAD = 64
RWKV_HEADS = GROUP_WIDTH // RWKV_HEAD
RWKV_WIDTH = RWKV_HEADS * RWKV_HEAD
DECAY_RANK = 96
ICL_RANK = 96
GATE_RANK = 256
RWKV_GN_EPS = 64e-5
RWKV_COLS = 3 * RWKV_WIDTH + 2 * DECAY_RANK + 2 * ICL_RANK + GATE_RANK
WKV_CHUNK = 64
RWKV_PACK_BLOCKS = 11

POOL_WINDOWS = (2, 4, 8, 16)
POOL_WIDTH = GROUP_WIDTH
POOL_GROUP = POOL_WIDTH // len(POOL_WINDOWS)
CONV_WIDTH = GROUP_WIDTH
CONV_COLS = 3 * CONV_WIDTH
FF_HIDDEN = 4 * D_MODEL
MLA_COLS = MLA_Q_RANK + MLA_KV_RANK + MLA_ROPE

ROW_TILE = 256
HALO = 8
MOD_ROWS = 16
VMEM_LIMIT = 56 * 1024 * 1024


def _cparams(sem):
    return pltpu.CompilerParams(dimension_semantics=sem, vmem_limit_bytes=VMEM_LIMIT)


def _resident(shape, index_map):
    return pl.BlockSpec(shape, index_map, pipeline_mode=pl.Buffered(1))


def _rms(x, g):
    return x * lax.rsqrt(jnp.mean(x * x, axis=-1, keepdims=True) + NORM_EPS) * g


def _dot(a, b):
    return jnp.dot(a.astype(BF16), b.astype(BF16), preferred_element_type=F32)


def _dot_nt(a, b):
    return lax.dot_general(a.astype(BF16), b.astype(BF16), (((1,), (1,)), ((), ())), preferred_element_type=F32)


def _split3(x):
    h1 = x.astype(BF16)
    r1 = x - h1.astype(F32)
    h2 = r1.astype(BF16)
    h3 = (r1 - h2.astype(F32)).astype(BF16)
    return h1, h2, h3


def _dot_exact_lhs(a01, x):
    h1, h2, h3 = _split3(x)
    d = lambda h: jnp.dot(a01, h, preferred_element_type=F32)
    return d(h1) + d(h2) + d(h3)


def _dot_x3(a, b):
    a1 = a.astype(BF16)
    a2 = (a - a1.astype(F32)).astype(BF16)
    b1 = b.astype(BF16)
    b2 = (b - b1.astype(F32)).astype(BF16)
    m = a.shape[0]
    t = jnp.dot(jnp.concatenate([a1, a2], axis=0), b1, preferred_element_type=F32)
    return t[0:m] + (t[m:2 * m] + jnp.dot(a1, b2, preferred_element_type=F32))


def _mod_kernel(c_ref, w_ref, b_ref, o_ref):
    c = c_ref[...]
    s = c * jax.nn.sigmoid(c)
    o_ref[...] = _dot(s, w_ref[...]) + b_ref[...]


def _modulation(cc, ada_w, ada_b):
    depth = ada_w.shape[0]
    tn = 1024
    return pl.pallas_call(
        _mod_kernel,
        grid=(depth, 6 * D_MODEL // tn),
        in_specs=[
            pl.BlockSpec((MOD_ROWS, D_MODEL), lambda l, j: (0, 0)),
            pl.BlockSpec((None, D_MODEL, tn), lambda l, j: (l, 0, j)),
            pl.BlockSpec((None, 1, tn), lambda l, j: (l, 0, j)),
        ],
        out_specs=pl.BlockSpec((None, MOD_ROWS, tn), lambda l, j: (l, 0, j)),
        out_shape=jax.ShapeDtypeStruct((depth, MOD_ROWS, 6 * D_MODEL), F32),
        compiler_params=_cparams(("parallel", "parallel")),
        name="modulation",
    )(cc, ada_w, ada_b.reshape(depth, 1, 6 * D_MODEL))


def _mod_row(mod_ref, row, k):
    return mod_ref[pl.ds(row, 1), k * D_MODEL:(k + 1) * D_MODEL]


def _halo_specs(width, n_rows, t_off=0):
    per = ROW_TILE // HALO
    last = n_rows // HALO - 1
    cur = pl.BlockSpec((None, ROW_TILE, width), lambda b, t: (b, t + t_off, 0))
    prev = pl.BlockSpec((None, HALO, width), lambda b, t: (b, jnp.maximum((t + t_off) * per - 1, 0), 0))
    nxt = pl.BlockSpec((None, HALO, width), lambda b, t: (b, jnp.minimum((t + t_off + 1) * per, last), 0))
    return cur, prev, nxt


def _seq_edges(t, nct, nt):
    has_prev = jnp.logical_and(t != 0, t != nct)
    has_next = jnp.logical_and(t != nct - 1, t != nt - 1)
    return has_prev, has_next


def _seg_sum(x, e_ref):
    h1 = x.astype(BF16)
    h2 = (x - h1.astype(F32)).astype(BF16)
    e = e_ref[...]
    return jnp.dot(h1, e, preferred_element_type=F32) + jnp.dot(h2, e, preferred_element_type=F32)


IN_SEGMENTS = (MLA_FEAT, RWKV_COLS, POOL_WIDTH, CONV_COLS)
IN_COLS_PADDED = sum(IN_SEGMENTS)
HALO_ROWS = ROW_TILE + 2 * HALO
LO, HI = HALO, HALO + ROW_TILE


def _rope(x, cs, sn):
    return x * cs + pltpu.roll(x, 64, 1) * sn


def _store_with_halo(buf, cols, res):
    buf[LO:HI, cols] = res[0:ROW_TILE]
    buf[0:LO, cols] = res[ROW_TILE:ROW_TILE + HALO]
    buf[HI:HI + HALO, cols] = res[ROW_TILE + HALO:ROW_TILE + 2 * HALO]


def _mla_qkv(f, cs, sn, gq_ref, gkv_ref, wq_ref, wkv_ref, q_ref, k_ref, v_ref):
    scale = (MLA_NOPE + MLA_ROPE) ** -0.5
    q = _dot(_rms(f[:, 0:MLA_Q_RANK], gq_ref[...]), wq_ref[...])
    kv = _dot(_rms(f[:, MLA_Q_RANK:MLA_Q_RANK + MLA_KV_RANK], gkv_ref[...]), wkv_ref[...])
    k_pe = _rope(f[:, 512:640], cs, sn).astype(BF16)
    for h in range(MLA_HEADS):
        o = h * MLA_QK_PAD
        q_ref[:, o:o + 128] = (q[:, o:o + 128] * scale).astype(BF16)
        q_ref[:, o + 128:o + 256] = (_rope(q[:, o + 128:o + 256], cs, sn) * scale).astype(BF16)
        k_ref[:, o:o + 128] = kv[:, o:o + 128].astype(BF16)
        k_ref[:, o + 128:o + 256] = k_pe
        v_ref[:, h * MLA_V:(h + 1) * MLA_V] = kv[:, o + 128:o + 256].astype(BF16)


def _rwkv_features(buf, mu_ref, w0_ref, a0_ref, w2_ref, a2_ref, g2_ref, kk_ref, ka_ref, rk_ref, e_ref, o_ref):
    w = RWKV_WIDTH

    def shifted(cols):
        f = buf[LO:HI, cols]
        return f + mu_ref[:, cols] * (0.5 * (buf[LO - 1:HI - 1, cols] + buf[LO + 1:HI + 1, cols]) - f)

    def put(i, val):
        o_ref[:, i * w:(i + 1) * w] = val

    xw = w0_ref[...] + _dot(jnp.tanh(shifted(slice(1536, 1792))), w2_ref[...])
    lw = -float(np.exp(-0.5)) * jax.nn.sigmoid(xw)
    put(5, lw[:, 0:w])
    put(8, lw[:, w:2 * w])
    yield
    a = jax.nn.sigmoid(a0_ref[...] + _dot(shifted(slice(1664, 1920)), a2_ref[...]))
    put(7, a[:, 0:w])
    put(10, a[:, w:2 * w])
    put(3, _dot(jax.nn.sigmoid(shifted(slice(1920, 2176))), g2_ref[...]))
    yield
    k = shifted(slice(w, 2 * w))
    kkv = k * kk_ref[...]
    put(2, kkv / jnp.maximum(jnp.sqrt(_seg_sum(kkv * kkv, e_ref)), 1e-12))
    yield
    r = shifted(slice(0, w))
    put(0, r)
    rk = r * rk_ref[...]
    ka = ka_ref[...]
    k_f = k * (1.0 + (a[:, 0:w] - 1.0) * ka)
    k_b = k * (1.0 + (a[:, w:2 * w] - 1.0) * ka)
    put(6, k_f)
    put(9, k_b)
    yield
    v = shifted(slice(2 * w, 3 * w))
    put(1, v)
    put(4, _seg_sum(rk * k_f + rk * k_b, e_ref) * v)


def _pool_mixer(buf, pw_ref, ps_ref, o_ref, pos, n_own):
    for gi, win in enumerate(POOL_WINDOWS):
        hw = win // 2
        cols = slice(gi * POOL_GROUP, (gi + 1) * POOL_GROUP)
        s = buf[LO - hw:HI - hw, cols]
        for o in range(-hw + 1, hw):
            s = s + buf[LO + o:HI + o, cols]
        cnt = (jnp.minimum(pos + hw, n_own) - jnp.maximum(pos - hw, 0)).astype(F32)
        z = s / cnt - buf[LO:HI, cols]
        o_ref[:, cols] = (_dot(z, pw_ref[gi]) * ps_ref[:, cols]).astype(o_ref.dtype)


def _front_kernel(x_ref, xp_ref, xn_ref, mod_ref, g_ref, w_ref, cs_ref, sn_ref, gq_ref, gkv_ref, wq_ref, wkv_ref,
                  mu_ref, w0_ref, a0_ref, w2_ref, a2_ref, g2_ref, kk_ref, ka_ref, rk_ref, e_ref,
                  pw_ref, ps_ref, cw_ref,
                  q_ref, k_ref, v_ref, pack_ref, pool_ref, conv_ref,
                  rw_buf, pool_buf, u_buf, *, nb, nct, nt, n_ctx, n_seq):
    b = pl.program_id(0)
    t = pl.program_id(1)
    in_ctx = t < nct
    row = jnp.where(in_ctx, nb, b)
    has_prev, has_next = _seq_edges(t, nct, nt)
    shift, scale = _mod_row(mod_ref, row, 0), 1.0 + _mod_row(mod_ref, row, 1)
    norm = lambda ref: _rms(ref[...], g_ref[...]) * scale + shift
    xn = jnp.concatenate([norm(x_ref), jnp.where(has_prev, norm(xp_ref), 0.0), jnp.where(has_next, norm(xn_ref), 0.0)],
                         axis=0).astype(BF16)

    mla0 = 0
    rw0 = IN_SEGMENTS[0]
    pool0 = rw0 + RWKV_COLS
    conv0 = pool0 + POOL_WIDTH
    cw = CONV_WIDTH
    proj = lambda rows, c0, c1: jnp.dot(rows, w_ref[:, c0:c1], preferred_element_type=F32)

    c0 = 0
    while c0 < RWKV_COLS:
        c1 = min(c0 + 512, RWKV_COLS)
        _store_with_halo(rw_buf, slice(c0, c1), proj(xn, rw0 + c0, rw0 + c1))
        c0 = c1
    stages = _rwkv_features(rw_buf, mu_ref, w0_ref, a0_ref, w2_ref, a2_ref, g2_ref, kk_ref, ka_ref, rk_ref, e_ref, pack_ref)
    next(stages)
    f_mla = proj(xn[0:ROW_TILE], mla0, rw0)
    next(stages)
    _store_with_halo(pool_buf, slice(None), proj(xn, pool0, conv0))
    gb = proj(xn[0:ROW_TILE], conv0, conv0 + cw)
    next(stages)
    gc = proj(xn, conv0 + cw, conv0 + 2 * cw)
    next(stages)
    hx = proj(xn, conv0 + 2 * cw, conv0 + 3 * cw)
    for _ in stages:
        pass

    _mla_qkv(f_mla, cs_ref[...], sn_ref[...], gq_ref, gkv_ref, wq_ref, wkv_ref, q_ref, k_ref, v_ref)
    pos = lax.broadcasted_iota(jnp.int32, (ROW_TILE, 1), 0) + jnp.where(in_ctx, t, t - nct) * ROW_TILE
    _pool_mixer(pool_buf, pw_ref, ps_ref, pool_ref, pos, jnp.where(in_ctx, n_ctx, n_seq))
    _store_with_halo(u_buf, slice(None), gc * hx)
    z = cw_ref[0:1, :] * u_buf[LO - 1:HI - 1] + cw_ref[1:2, :] * u_buf[LO:HI] + cw_ref[2:3, :] * u_buf[LO + 1:HI + 1]
    conv_ref[...] = (gb * z).astype(conv_ref.dtype)


def _front(h_all, mod, g, w_p, cs, sn, gq, gkv, wq, wkv, p, pool_w, pool_scale, conv_w, nb, nct, n_ctx):
    b_, n_, _ = h_all.shape
    nt = n_ // ROW_TILE
    qk_w = MLA_HEADS * MLA_QK_PAD
    pack_w = RWKV_PACK_BLOCKS * RWKV_WIDTH
    full = lambda a: _resident(a.shape, lambda b, t: (0,) * a.ndim)
    row = lambda w: pl.BlockSpec((None, ROW_TILE, w), lambda b, t: (b, t, 0))
    tab = pl.BlockSpec((ROW_TILE, 128), lambda b, t: (t, 0))
    consts = (mod, g.reshape(1, D_MODEL), w_p)
    mla = (gq.reshape(1, -1), gkv.reshape(1, -1), wq, wkv)
    rwk = (p["mu"], p["w0"], p["a0"], p["w2"], p["a2"], p["g2"], p["k_k"], p["k_a"], p["r_k"], p["seg"])
    mix = (pool_w, pool_scale, conv_w)
    return pl.pallas_call(
        functools.partial(_front_kernel, nb=nb, nct=nct, nt=nt, n_ctx=n_ctx, n_seq=n_ - n_ctx),
        grid=(b_, nt),
        in_specs=[*_halo_specs(D_MODEL, n_)] + [full(a) for a in consts] + [tab, tab]
                 + [full(a) for a in mla + rwk + mix],
        out_specs=(row(qk_w), row(qk_w), row(MLA_HEADS * MLA_V), row(pack_w), row(POOL_WIDTH), row(CONV_WIDTH)),
        out_shape=(
            jax.ShapeDtypeStruct((b_, n_, qk_w), BF16),
            jax.ShapeDtypeStruct((b_, n_, qk_w), BF16),
            jax.ShapeDtypeStruct((b_, n_, MLA_HEADS * MLA_V), BF16),
            jax.ShapeDtypeStruct((b_, n_, pack_w), F32),
            jax.ShapeDtypeStruct((b_, n_, POOL_WIDTH), BF16),
            jax.ShapeDtypeStruct((b_, n_, CONV_WIDTH), BF16),
        ),
        scratch_shapes=[pltpu.VMEM((HALO_ROWS, RWKV_COLS), F32), pltpu.VMEM((HALO_ROWS, POOL_WIDTH), F32),
                        pltpu.VMEM((HALO_ROWS, CONV_WIDTH), F32)],
        compiler_params=_cparams(("parallel", "parallel")),
        name="front",
    )(h_all, h_all, h_all, *consts, cs, sn, *mla, *rwk, *mix)


ATTN_Q_TILE = 512


def _attn_kernel(q_ref, k_ref, v_ref, o_ref, *, n_ctx, n_all, tq, with_ctx):
    s = pl.program_id(1)

    def attend(q_rows, o_rows, nk):
        for h in range(MLA_HEADS):
            q = q_ref[q_rows, h * MLA_QK_PAD:(h + 1) * MLA_QK_PAD]
            k = k_ref[0:nk, h * MLA_QK_PAD:(h + 1) * MLA_QK_PAD]
            sc = lax.dot_general(q, k, (((1,), (1,)), ((), ())), preferred_element_type=F32)
            p = jnp.exp(sc - jnp.max(sc, axis=-1, keepdims=True))
            l = jnp.sum(p, axis=-1, keepdims=True)
            o = jnp.dot(p.astype(BF16), v_ref[0:nk, h * MLA_V:(h + 1) * MLA_V], preferred_element_type=F32)
            o_ref[o_rows, h * MLA_V:(h + 1) * MLA_V] = (o / l).astype(o_ref.dtype)

    if with_ctx:
        @pl.when(s == 0)
        def _():
            attend(slice(0, n_ctx), slice(0, n_ctx), n_ctx)

        @pl.when(s > 0)
        def _():
            r0 = pl.multiple_of(n_ctx + (s - 1) * tq, ROW_TILE)
            attend(pl.ds(r0, tq), pl.ds(r0, tq), n_all)
    else:
        attend(pl.ds(pl.multiple_of(n_ctx + s * tq, ROW_TILE), tq), pl.ds(pl.multiple_of(s * tq, ROW_TILE), tq), n_all)


def _attention(q, k, v, n_ctx, with_ctx):
    b_, n_, qk_w = q.shape
    n_seq = n_ - n_ctx
    tq = ATTN_Q_TILE if n_seq % ATTN_Q_TILE == 0 else ROW_TILE
    n_out = n_ if with_ctx else n_seq
    whole = lambda rows, w: pl.BlockSpec((None, rows, w), lambda b, s: (b, 0, 0))
    return pl.pallas_call(
        functools.partial(_attn_kernel, n_ctx=n_ctx, n_all=n_, tq=tq, with_ctx=with_ctx),
        grid=(b_, n_seq // tq + (1 if with_ctx else 0)),
        in_specs=[whole(n_, qk_w), whole(n_, qk_w), whole(n_, MLA_HEADS * MLA_V)],
        out_specs=whole(n_out, MLA_HEADS * MLA_V),
        out_shape=jax.ShapeDtypeStruct((b_, n_out, MLA_HEADS * MLA_V), BF16),
        compiler_params=_cparams(("parallel", "arbitrary")),
        name="mla_attention",
    )(q, k, v)


def _bd(x, m0):
    return jnp.concatenate([jnp.where(m0, x, 0.0), jnp.where(m0, 0.0, x)], axis=0)


def _wkv_kernel(r_ref, v_ref, kk_ref, lw_ref, kd_ref, a_ref, y_ref, s_ref, rw_s, y0_s, m_s, n_s, *, rev):
    c_ = WKV_CHUNK
    nsub = ROW_TILE // c_
    npair = RWKV_WIDTH // 128

    @pl.when(pl.program_id(1) == 0)
    def _():
        s_ref[...] = jnp.zeros_like(s_ref)

    ri = lax.broadcasted_iota(jnp.int32, (c_, c_), 0)
    ci = lax.broadcasted_iota(jnp.int32, (c_, c_), 1)
    tmat = ((ri <= ci) if rev else (ri >= ci)).astype(BF16)
    rc = lax.broadcasted_iota(jnp.int32, (c_, 128), 0)
    cc = lax.broadcasted_iota(jnp.int32, (c_, 128), 1) % c_
    strict = (rc < cc) if rev else (rc > cc)
    incl = (rc <= cc) if rev else (rc >= cc)
    eye = rc == cc
    m0_128 = lax.broadcasted_iota(jnp.int32, (1, 128), 1) < c_
    m0_256 = (lax.broadcasted_iota(jnp.int32, (1, 256), 1) % 128) < c_
    zeros_cv = jnp.zeros((c_, 128), F32)

    items = [(c, p) for c in range(nsub) for p in range(npair)]
    at, rt, bt, kt, bh, kh, vv, e_tot = {}, {}, {}, {}, {}, {}, {}, {}
    for c in range(nsub):
        rows = slice(c * c_, (c + 1) * c_)
        lw = lw_ref[rows, :]
        lc = _dot_exact_lhs(tmat, lw)
        ltot = jnp.sum(lw, axis=0, keepdims=True)
        e_neg = jnp.exp(-lc)
        e_end = jnp.exp(ltot - lc)
        kk = kk_ref[rows, :]
        kd = kd_ref[rows, :]
        bv = kk * a_ref[rows, :]
        vv[c] = v_ref[rows, :]
        at[c] = -kk * jnp.exp(lc - lw)
        rt[c] = r_ref[rows, :] * jnp.exp(lc)
        bt[c], kt[c] = bv * e_neg, kd * e_neg
        bh[c], kh[c] = bv * e_end, kd * e_end
        e_tot[c] = jnp.exp(ltot)
    sl = lambda p: slice(p * 128, (p + 1) * 128)

    g = {}
    for c, p in items:
        ar = jnp.concatenate([at[c][:, sl(p)], rt[c][:, sl(p)]], axis=0)
        bd = jnp.concatenate([_bd(bt[c][:, sl(p)], m0_128), _bd(kt[c][:, sl(p)], m0_128)], axis=0)
        g[c, p] = _dot_nt(ar, bd)
    av, x, pw, a_rb = {}, {}, {}, {}
    for c, p in items:
        a_ak = jnp.where(strict, g[c, p][0:c_, 128:256], 0.0)
        a_rk = jnp.where(incl, g[c, p][c_:2 * c_, 128:256], 0.0)
        av[c, p] = _dot(jnp.concatenate([a_ak, a_rk], axis=0), _bd(vv[c][:, sl(p)], m0_128))
        pw[c, p] = jnp.where(strict, g[c, p][0:c_, 0:128], 0.0)
        a_rb[c, p] = jnp.where(incl, g[c, p][c_:2 * c_, 0:128], 0.0)
    for c, p in items:
        x[c, p] = jnp.concatenate([at[c][:, sl(p)], av[c, p][0:c_]], axis=1)
    for it in range(6):
        dot = _dot_x3 if it < 2 else _dot
        for c, p in items:
            x[c, p] = x[c, p] + dot(pw[c, p], _bd(x[c, p], m0_256))
        if it < 5:
            for c, p in items:
                pw[c, p] = dot(pw[c, p], _bd(pw[c, p], m0_128))
    for c, p in items:
        ry = _dot(a_rb[c, p], _bd(x[c, p], m0_256))
        rw_s[c, p] = rt[c][:, sl(p)] + ry[:, 0:128]
        y0_s[c, p] = ry[:, 128:256] + av[c, p][c_:2 * c_]
    for c, p in items:
        bkt = jnp.concatenate([bh[c][:, sl(p)], kh[c][:, sl(p)]], axis=0).T
        rhs = jnp.concatenate([x[c, p], jnp.concatenate([zeros_cv, vv[c][:, sl(p)]], axis=1)], axis=0)
        z = _dot(bkt, rhs)
        m_s[c, p] = (jnp.where(m0_128, z[0:c_, 0:128], z[c_:2 * c_, 0:128])
                     + jnp.where(eye, e_tot[c][:, sl(p)], 0.0))
        n_s[c, p] = jnp.where(m0_128, z[0:c_, 128:256], z[c_:2 * c_, 128:256])

    for i in range(nsub):
        c = (nsub - 1 - i) if rev else i
        for p in range(npair):
            lhs = jnp.concatenate([rw_s[c, p], m_s[c, p]], axis=0)
            o = _dot_x3(lhs, _bd(s_ref[p], m0_128))
            y_ref[c * c_:(c + 1) * c_, sl(p)] = o[0:c_] + y0_s[c, p]
            s_ref[p] = o[c_:2 * c_] + n_s[c, p]


def _wkv(pack, nct, rev):
    b_, n_, _ = pack.shape
    nt = n_ // ROW_TILE
    w = RWKV_WIDTH
    nsub = ROW_TILE // WKV_CHUNK
    npair = w // 128
    if rev:
        order = lambda s: jnp.where(s < nct, nct - 1 - s, nt + nct - 1 - s)
    else:
        order = lambda s: s
    col = lambda j: pl.BlockSpec((None, ROW_TILE, w), lambda b, s: (b, order(s), j))
    d = 3 if rev else 0
    scr = lambda: pltpu.VMEM((nsub, npair, WKV_CHUNK, 128), F32)
    return pl.pallas_call(
        functools.partial(_wkv_kernel, rev=rev),
        grid=(b_, nt),
        in_specs=[col(0), col(1), col(2), col(5 + d), col(6 + d), col(7 + d)],
        out_specs=pl.BlockSpec((None, ROW_TILE, w), lambda b, s: (b, order(s), 0)),
        out_shape=jax.ShapeDtypeStruct((b_, n_, w), F32),
        scratch_shapes=[pltpu.VMEM((npair, WKV_CHUNK, 128), F32), scr(), scr(), scr(), scr()],
        compiler_params=_cparams(("parallel", "arbitrary")),
        name="wkv_bwd" if rev else "wkv_fwd",
    )(pack, pack, pack, pack, pack, pack)


def _mix_out_kernel(att_ref, yf_ref, yb_ref, g_ref, bonus_ref, lg_ref, lb_ref, e_ref, pool_ref, conv_ref,
                    h_ref, mod_ref, w_ref, o_ref, *, nb, nct, t_off):
    b = pl.program_id(0)
    t = pl.program_id(1) + t_off
    row = jnp.where(t < nct, nb, b)
    y = yf_ref[...] + yb_ref[...]
    mean = _seg_sum(y, e_ref) * (1.0 / RWKV_HEAD)
    d = y - mean
    var = _seg_sum(d * d, e_ref) * (1.0 / RWKV_HEAD)
    yn = d * lax.rsqrt(var + RWKV_GN_EPS) * lg_ref[...] + lb_ref[...]
    rw = ((yn + bonus_ref[...]) * g_ref[...]).astype(BF16)
    mixed = jnp.concatenate([att_ref[...], rw, pool_ref[...], conv_ref[...]], axis=1)
    acc = jnp.dot(mixed, w_ref[...], preferred_element_type=F32)
    o_ref[...] = h_ref[...] + _mod_row(mod_ref, row, 2) * acc


def _mix_out(att, yf, yb, pack, pool, conv, h_all, mod, w_out, p, nb, nct, with_ctx):
    b_, n_, gw = yf.shape
    t_off = 0 if with_ctx else nct
    n_out = n_ - t_off * ROW_TILE
    blk = lambda j: pl.BlockSpec((None, ROW_TILE, gw), lambda b, t: (b, t + t_off, j))
    full = lambda a: _resident(a.shape, lambda b, t: (0,) * a.ndim)
    return pl.pallas_call(
        functools.partial(_mix_out_kernel, nb=nb, nct=nct, t_off=t_off),
        grid=(b_, n_out // ROW_TILE),
        in_specs=[pl.BlockSpec((None, ROW_TILE, gw), lambda b, t: (b, t, 0)),
                  blk(0), blk(0), blk(3), blk(4), full(p["ln_g"]), full(p["ln_b"]), full(p["seg"]),
                  blk(0), blk(0),
                  pl.BlockSpec((None, ROW_TILE, D_MODEL), lambda b, t: (b, t + t_off, 0)),
                  full(mod), full(w_out)],
        out_specs=pl.BlockSpec((None, ROW_TILE, D_MODEL), lambda b, t: (b, t, 0)),
        out_shape=jax.ShapeDtypeStruct((b_, n_out, D_MODEL), F32),
        compiler_params=_cparams(("parallel", "parallel")),
        name="mix_out",
    )(att, yf, yb, pack, pack, p["ln_g"], p["ln_b"], p["seg"], pool, conv, h_all, mod, w_out)


MLP_HIDDEN_TILE = 512


def _mlp_kernel(h_ref, mod_ref, g_ref, w1_ref, w2_ref, gf_ref, o_ref, xn_s, acc_s, *, nb, ctx_rows, final):
    b = pl.program_id(0)
    t = pl.program_id(1)
    j = pl.program_id(2)
    first_row = jnp.where(t == 0, nb, b) if ctx_rows else b

    def per_rows(fn):
        if ctx_rows == 0:
            fn(slice(None), b)
        elif ctx_rows == h_ref.shape[0]:
            fn(slice(None), first_row)
        else:
            fn(slice(0, ctx_rows), first_row)
            fn(slice(ctx_rows, None), b)

    @pl.when(j == 0)
    def _():
        y = _rms(h_ref[...], g_ref[...])

        def put(rows, mrow):
            xn_s[rows, :] = (y[rows, :] * (1.0 + _mod_row(mod_ref, mrow, 4)) + _mod_row(mod_ref, mrow, 3)).astype(BF16)

        per_rows(put)
        acc_s[...] = jnp.zeros_like(acc_s)

    hid = jnp.dot(xn_s[...], w1_ref[...], preferred_element_type=F32)
    hid = jnp.square(jnp.maximum(hid, 0.0))
    acc_s[...] += jnp.dot(hid.astype(BF16), w2_ref[...], preferred_element_type=F32)

    @pl.when(j == pl.num_programs(2) - 1)
    def _():
        def put(rows, mrow):
            o = h_ref[rows, :] + _mod_row(mod_ref, mrow, 5) * acc_s[rows, :]
            if final:
                o = _rms(o, gf_ref[...])
            o_ref[rows, :] = o

        per_rows(put)


def _mlp(h, mod, g, w1, w2, g_final, nb, tm, ctx_rows, final):
    b_, n_, _ = h.shape
    th = MLP_HIDDEN_TILE
    return pl.pallas_call(
        functools.partial(_mlp_kernel, nb=nb, ctx_rows=ctx_rows, final=final),
        grid=(b_, n_ // tm, FF_HIDDEN // th),
        in_specs=[
            pl.BlockSpec((None, tm, D_MODEL), lambda b, t, j: (b, t, 0)),
            _resident((MOD_ROWS, 6 * D_MODEL), lambda b, t, j: (0, 0)),
            _resident((1, D_MODEL), lambda b, t, j: (0, 0)),
            pl.BlockSpec((D_MODEL, th), lambda b, t, j: (0, j)),
            pl.BlockSpec((th, D_MODEL), lambda b, t, j: (j, 0)),
            _resident((1, D_MODEL), lambda b, t, j: (0, 0)),
        ],
        out_specs=pl.BlockSpec((None, tm, D_MODEL), lambda b, t, j: (b, t, 0)),
        out_shape=jax.ShapeDtypeStruct((b_, n_, D_MODEL), F32),
        scratch_shapes=[pltpu.VMEM((tm, D_MODEL), BF16), pltpu.VMEM((tm, D_MODEL), F32)],
        compiler_params=_cparams(("parallel", "parallel", "arbitrary")),
        name="mlp",
    )(h, mod, g.reshape(1, D_MODEL), w1, w2, g_final.reshape(1, D_MODEL))


def _rope_partner_perm():
    q = MLA_ROPE // 4
    i = np.arange(MLA_ROPE)
    return np.where((i // q) % 2 == 0, i + q, i - q)


def _rope_tables(n_ctx, n_seq):
    rows = n_seq // GRID_W
    row = jnp.repeat(jnp.arange(rows), GRID_W)
    col = jnp.tile(jnp.arange(GRID_W), rows)
    pos = jnp.stack([row, col], axis=-1).astype(F32)
    axis_dim = MLA_ROPE // 2
    inv_freq = ROPE_BASE ** (-jnp.arange(0, axis_dim, 2, dtype=F32) / axis_dim)
    ang = pos[:, :, None] * inv_freq
    cos, sin = jnp.cos(ang), jnp.sin(ang)
    cos64 = jnp.concatenate([cos[:, 0], cos[:, 0], cos[:, 1], cos[:, 1]], axis=-1)
    sin64 = jnp.concatenate([-sin[:, 0], sin[:, 0], -sin[:, 1], sin[:, 1]], axis=-1)
    pad = jnp.zeros((n_seq, 128 - MLA_ROPE), F32)
    cs_lat = jnp.concatenate([cos64, pad], axis=-1)
    sn_lat = jnp.concatenate([sin64, pad], axis=-1)
    cs_ctx = jnp.concatenate([jnp.ones((n_ctx, MLA_ROPE), F32), jnp.zeros((n_ctx, 128 - MLA_ROPE), F32)], axis=-1)
    return jnp.concatenate([cs_ctx, cs_lat], axis=0), jnp.concatenate([jnp.zeros((n_ctx, 128), F32), sn_lat], axis=0)


def _layer_params(l, w_in, mla_w_uq, mla_w_ukv, rwkv_mu, rwkv_w0, rwkv_w2, rwkv_a0, rwkv_a2, rwkv_g2,
                  rwkv_k_k, rwkv_k_a, rwkv_r_k, rwkv_ln_g, rwkv_ln_b, seg):
    perm = _rope_partner_perm()
    w = w_in[l]
    kr0 = MLA_Q_RANK + MLA_KV_RANK
    w_in_p = jnp.concatenate([w[:, :MLA_COLS], w[:, kr0:MLA_COLS][:, perm], w[:, MLA_COLS:]], axis=1).astype(BF16)

    wq = mla_w_uq[l].reshape(MLA_Q_RANK, MLA_HEADS, MLA_NOPE + MLA_ROPE)
    wq = jnp.concatenate([wq, wq[:, :, MLA_NOPE:][:, :, perm]], axis=-1).reshape(MLA_Q_RANK, MLA_HEADS * MLA_QK_PAD)

    wd = RWKV_WIDTH
    z = lambda r: jnp.zeros((r, wd), F32)
    w2 = jnp.concatenate([
        jnp.concatenate([rwkv_w2[l, 0], z(DECAY_RANK), z(256 - 2 * DECAY_RANK)], axis=0),
        jnp.concatenate([z(DECAY_RANK), rwkv_w2[l, 1], z(256 - 2 * DECAY_RANK)], axis=0)], axis=1)
    lead = 256 - 2 * ICL_RANK
    a2 = jnp.concatenate([
        jnp.concatenate([z(lead), rwkv_a2[l, 0], z(ICL_RANK)], axis=0),
        jnp.concatenate([z(lead), z(ICL_RANK), rwkv_a2[l, 1]], axis=0)], axis=1)
    rw = {
        "mu": rwkv_mu[l].reshape(1, RWKV_COLS),
        "w0": rwkv_w0[l].reshape(1, 2 * wd),
        "a0": rwkv_a0[l].reshape(1, 2 * wd),
        "w2": w2.astype(BF16),
        "a2": a2.astype(BF16),
        "g2": rwkv_g2[l].astype(BF16),
        "k_k": rwkv_k_k[l].reshape(1, wd),
        "k_a": rwkv_k_a[l].reshape(1, wd),
        "r_k": rwkv_r_k[l].reshape(1, wd),
        "ln_g": rwkv_ln_g[l].reshape(1, wd),
        "ln_b": rwkv_ln_b[l].reshape(1, wd),
        "seg": seg,
    }
    return w_in_p, wq.astype(BF16), mla_w_ukv[l].astype(BF16), rw


def kernel(x, c, ctx, c_ctx, ada_w, ada_b, norm1_g, norm2_g, w_in, mla_q_norm_g, mla_w_uq, mla_kv_norm_g, mla_w_ukv, rwkv_mu, rwkv_w0, rwkv_w2, rwkv_a0, rwkv_a2, rwkv_g2, rwkv_k_k, rwkv_k_a, rwkv_r_k, rwkv_ln_g, rwkv_ln_b, pool_w, pool_scale, conv_w, w_out, mlp_w1, mlp_w2, final_norm_g):
    nb, n_seq, d = x.shape
    n_ctx = ctx.shape[1]
    depth = ada_w.shape[0]
    assert d == D_MODEL and nb < MOD_ROWS
    assert n_ctx % ROW_TILE == 0 and n_seq % ROW_TILE == 0 and n_seq % GRID_W == 0
    nct = n_ctx // ROW_TILE

    cc = jnp.zeros((MOD_ROWS, D_MODEL), F32).at[:nb].set(c).at[nb].set(c_ctx)
    mod = _modulation(cc, ada_w, ada_b)
    cs, sn = _rope_tables(n_ctx, n_seq)
    head = np.arange(RWKV_WIDTH) // RWKV_HEAD
    seg = jnp.asarray(head[:, None] == head[None, :], dtype=BF16)

    h_all = jnp.concatenate([ctx, x], axis=1)
    for l in range(depth):
        last = l == depth - 1
        with_ctx = not last
        w_in_p, wq, wkv, rw_p = _layer_params(
            l, w_in, mla_w_uq, mla_w_ukv, rwkv_mu, rwkv_w0, rwkv_w2, rwkv_a0, rwkv_a2, rwkv_g2,
            rwkv_k_k, rwkv_k_a, rwkv_r_k, rwkv_ln_g, rwkv_ln_b, seg)

        q, k, v, pack, pool, conv = _front(
            h_all, mod[l], norm1_g[l], w_in_p, cs, sn, mla_q_norm_g[l], mla_kv_norm_g[l], wq, wkv, rw_p,
            pool_w[l].astype(BF16), pool_scale[l].reshape(1, POOL_WIDTH), conv_w[l], nb, nct, n_ctx)
        att = _attention(q, k, v, n_ctx, with_ctx)
        y_f = _wkv(pack, nct, rev=False)
        y_b = _wkv(pack, nct, rev=True)
        h_mid = _mix_out(att, y_f, y_b, pack, pool, conv, h_all, mod[l], w_out[l].astype(BF16), rw_p, nb, nct, with_ctx)

        n_rows = h_mid.shape[1]
        if with_ctx:
            tm = 3 * ROW_TILE if n_rows % (3 * ROW_TILE) == 0 else ROW_TILE
            ctx_rows = n_ctx
            assert n_ctx <= tm, "context longer than one MLP row tile"
        else:
            tm = 2 * ROW_TILE if n_rows % (2 * ROW_TILE) == 0 else ROW_TILE
            ctx_rows = 0
        h_all = _mlp(h_mid, mod[l], norm2_g[l], mlp_w1[l].astype(BF16), mlp_w2[l].astype(BF16),
                     final_norm_g, nb, tm, ctx_rows, last)
    return h_all
```

```python
import functools

import numpy as np
import jax
import jax.numpy as jnp
from jax import lax
from jax.experimental import pallas as pl
from jax.experimental.pallas import tpu as pltpu

F32 = jnp.float32
BF16 = jnp.bfloat16

D_MODEL = 2048
GRID_W = 64
NORM_EPS = 1e-6
GROUP_WIDTH = D_MODEL // 4

MLA_NOPE = 128
MLA_ROPE = 64
MLA_V = 128
MLA_HEADS = GROUP_WIDTH // MLA_V
MLA_Q_RANK = 384
MLA_KV_RANK = 128
ROPE_BASE = 10000.0
MLA_QK_PAD = 256
MLA_FEAT = 640

RWKV_HEAD = 64
RWKV_HEADS = GROUP_WIDTH // RWKV_HEAD
RWKV_WIDTH = RWKV_HEADS * RWKV_HEAD
DECAY_RANK = 96
ICL_RANK = 96
GATE_RANK = 256
RWKV_GN_EPS = 64e-5
RWKV_COLS = 3 * RWKV_WIDTH + 2 * DECAY_RANK + 2 * ICL_RANK + GATE_RANK
WKV_CHUNK = 64
RWKV_PACK_BLOCKS = 11

POOL_WINDOWS = (2, 4, 8, 16)
POOL_WIDTH = GROUP_WIDTH
POOL_GROUP = POOL_WIDTH // len(POOL_WINDOWS)
CONV_WIDTH = GROUP_WIDTH
CONV_COLS = 3 * CONV_WIDTH
FF_HIDDEN = 4 * D_MODEL
MLA_COLS = MLA_Q_RANK + MLA_KV_RANK + MLA_ROPE

ROW_TILE = 256
HALO = 8
MOD_ROWS = 16
VMEM_LIMIT = 56 * 1024 * 1024


def _cparams(sem):
    return pltpu.CompilerParams(dimension_semantics=sem, vmem_limit_bytes=VMEM_LIMIT)


def _resident(shape, index_map):
    return pl.BlockSpec(shape, index_map, pipeline_mode=pl.Buffered(1))


def _rms(x, g):
    return x * lax.rsqrt(jnp.mean(x * x, axis=-1, keepdims=True) + NORM_EPS) * g


def _dot(a, b):
    return jnp.dot(a.astype(BF16), b.astype(BF16), preferred_element_type=F32)


def _dot_nt(a, b):
    return lax.dot_general(a.astype(BF16), b.astype(BF16), (((1,), (1,)), ((), ())), preferred_element_type=F32)


def _split3(x):
    h1 = x.astype(BF16)
    r1 = x - h1.astype(F32)
    h2 = r1.astype(BF16)
    h3 = (r1 - h2.astype(F32)).astype(BF16)
    return h1, h2, h3


def _dot_exact_lhs(a01, x):
    h1, h2, h3 = _split3(x)
    d = lambda h: jnp.dot(a01, h, preferred_element_type=F32)
    return d(h1) + d(h2) + d(h3)


def _dot_x3(a, b):
    a1 = a.astype(BF16)
    a2 = (a - a1.astype(F32)).astype(BF16)
    b1 = b.astype(BF16)
    b2 = (b - b1.astype(F32)).astype(BF16)
    m = a.shape[0]
    t = jnp.dot(jnp.concatenate([a1, a2], axis=0), b1, preferred_element_type=F32)
    return t[0:m] + (t[m:2 * m] + jnp.dot(a1, b2, preferred_element_type=F32))


def _mod_kernel(c_ref, w_ref, b_ref, o_ref):
    c = c_ref[...]
    s = c * jax.nn.sigmoid(c)
    o_ref[...] = _dot(s, w_ref[...]) + b_ref[...]


def _modulation(cc, ada_w, ada_b):
    depth = ada_w.shape[0]
    tn = 1024
    return pl.pallas_call(
        _mod_kernel,
        grid=(depth, 6 * D_MODEL // tn),
        in_specs=[
            pl.BlockSpec((MOD_ROWS, D_MODEL), lambda l, j: (0, 0)),
            pl.BlockSpec((None, D_MODEL, tn), lambda l, j: (l, 0, j)),
            pl.BlockSpec((None, 1, tn), lambda l, j: (l, 0, j)),
        ],
        out_specs=pl.BlockSpec((None, MOD_ROWS, tn), lambda l, j: (l, 0, j)),
        out_shape=jax.ShapeDtypeStruct((depth, MOD_ROWS, 6 * D_MODEL), F32),
        compiler_params=_cparams(("parallel", "parallel")),
        name="modulation",
    )(cc, ada_w, ada_b.reshape(depth, 1, 6 * D_MODEL))


def _mod_row(mod_ref, row, k):
    return mod_ref[pl.ds(row, 1), k * D_MODEL:(k + 1) * D_MODEL]


def _halo_specs(width, n_rows, tile0=0):
    per = ROW_TILE // HALO
    last_tile = n_rows // ROW_TILE - 1
    last = n_rows // HALO - 1
    clamp = lambda i, hi: jnp.clip(i, 0, hi)
    cur = pl.BlockSpec((None, ROW_TILE, width), lambda b, t: (b, clamp(t - tile0, last_tile), 0))
    prev = pl.BlockSpec((None, HALO, width), lambda b, t: (b, clamp((t - tile0) * per - 1, last), 0))
    nxt = pl.BlockSpec((None, HALO, width), lambda b, t: (b, clamp((t - tile0 + 1) * per, last), 0))
    return cur, prev, nxt


def _token_tile_spec(n_rows, tile0, t_off):
    last_tile = n_rows // ROW_TILE - 1
    return pl.BlockSpec((None, ROW_TILE, D_MODEL), lambda b, t: (b, jnp.clip(t + t_off - tile0, 0, last_tile), 0))


def _seq_edges(t, nct, nt):
    has_prev = jnp.logical_and(t != 0, t != nct)
    has_next = jnp.logical_and(t != nct - 1, t != nt - 1)
    return has_prev, has_next


def _seg_sum(x, e_ref):
    h1 = x.astype(BF16)
    h2 = (x - h1.astype(F32)).astype(BF16)
    e = e_ref[...]
    return jnp.dot(h1, e, preferred_element_type=F32) + jnp.dot(h2, e, preferred_element_type=F32)


IN_SEGMENTS = (MLA_FEAT, RWKV_COLS, POOL_WIDTH, CONV_COLS)
IN_COLS_PADDED = sum(IN_SEGMENTS)
HALO_ROWS = ROW_TILE + 2 * HALO
LO, HI = HALO, HALO + ROW_TILE


def _rope(x, cs, sn):
    return x * cs + pltpu.roll(x, 64, 1) * sn


def _store_with_halo(buf, cols, res):
    buf[LO:HI, cols] = res[0:ROW_TILE]
    buf[0:LO, cols] = res[ROW_TILE:ROW_TILE + HALO]
    buf[HI:HI + HALO, cols] = res[ROW_TILE + HALO:ROW_TILE + 2 * HALO]


def _mla_qkv(f, cs, sn, gq_ref, gkv_ref, wq_ref, wkv_ref, q_ref, k_ref, v_ref):
    scale = (MLA_NOPE + MLA_ROPE) ** -0.5
    q = _dot(_rms(f[:, 0:MLA_Q_RANK], gq_ref[...]), wq_ref[...])
    kv = _dot(_rms(f[:, MLA_Q_RANK:MLA_Q_RANK + MLA_KV_RANK], gkv_ref[...]), wkv_ref[...])
    k_pe = _rope(f[:, 512:640], cs, sn).astype(BF16)
    for h in range(MLA_HEADS):
        o = h * MLA_QK_PAD
        q_ref[:, o:o + 128] = (q[:, o:o + 128] * scale).astype(BF16)
        q_ref[:, o + 128:o + 256] = (_rope(q[:, o + 128:o + 256], cs, sn) * scale).astype(BF16)
        k_ref[:, o:o + 128] = kv[:, o:o + 128].astype(BF16)
        k_ref[:, o + 128:o + 256] = k_pe
        v_ref[:, h * MLA_V:(h + 1) * MLA_V] = kv[:, o + 128:o + 256].astype(BF16)


def _rwkv_features(buf, mu_ref, w0_ref, a0_ref, w2_ref, a2_ref, g2_ref, kk_ref, ka_ref, rk_ref, e_ref, o_ref):
    w = RWKV_WIDTH

    def shifted(cols):
        f = buf[LO:HI, cols]
        return f + mu_ref[:, cols] * (0.5 * (buf[LO - 1:HI - 1, cols] + buf[LO + 1:HI + 1, cols]) - f)

    def put(i, val):
        o_ref[:, i * w:(i + 1) * w] = val

    xw = w0_ref[...] + _dot(jnp.tanh(shifted(slice(1536, 1792))), w2_ref[...])
    lw = -float(np.exp(-0.5)) * jax.nn.sigmoid(xw)
    put(5, lw[:, 0:w])
    put(8, lw[:, w:2 * w])
    yield
    a = jax.nn.sigmoid(a0_ref[...] + _dot(shifted(slice(1664, 1920)), a2_ref[...]))
    put(7, a[:, 0:w])
    put(10, a[:, w:2 * w])
    put(3, _dot(jax.nn.sigmoid(shifted(slice(1920, 2176))), g2_ref[...]))
    yield
    k = shifted(slice(w, 2 * w))
    kkv = k * kk_ref[...]
    put(2, kkv / jnp.maximum(jnp.sqrt(_seg_sum(kkv * kkv, e_ref)), 1e-12))
    yield
    r = shifted(slice(0, w))
    put(0, r)
    rk = r * rk_ref[...]
    ka = ka_ref[...]
    k_f = k * (1.0 + (a[:, 0:w] - 1.0) * ka)
    k_b = k * (1.0 + (a[:, w:2 * w] - 1.0) * ka)
    put(6, k_f)
    put(9, k_b)
    yield
    v = shifted(slice(2 * w, 3 * w))
    put(1, v)
    put(4, _seg_sum(rk * k_f + rk * k_b, e_ref) * v)


def _pool_mixer(buf, pw_ref, ps_ref, o_ref, pos, n_own):
    for gi, win in enumerate(POOL_WINDOWS):
        hw = win // 2
        cols = slice(gi * POOL_GROUP, (gi + 1) * POOL_GROUP)
        s = buf[LO - hw:HI - hw, cols]
        for o in range(-hw + 1, hw):
            s = s + buf[LO + o:HI + o, cols]
        cnt = (jnp.minimum(pos + hw, n_own) - jnp.maximum(pos - hw, 0)).astype(F32)
        z = s / cnt - buf[LO:HI, cols]
        o_ref[:, cols] = (_dot(z, pw_ref[gi]) * ps_ref[:, cols]).astype(o_ref.dtype)


def _pick_rows(refs, in_ctx):
    if len(refs) == 3:
        return [r[...] for r in refs]
    return [jnp.where(in_ctx, c[...], x[...]) for c, x in zip(refs[0:3], refs[3:6])]


def _front_kernel(*refs, nparts, nb, nct, nt, n_ctx, n_seq):
    x_refs, refs = refs[:3 * nparts], refs[3 * nparts:]
    (mod_ref, g_ref, w_ref, cs_ref, sn_ref, gq_ref, gkv_ref, wq_ref, wkv_ref,
     mu_ref, w0_ref, a0_ref, w2_ref, a2_ref, g2_ref, kk_ref, ka_ref, rk_ref, e_ref,
     pw_ref, ps_ref, cw_ref,
     q_ref, k_ref, v_ref, pack_ref, pool_ref, conv_ref,
     rw_buf, pool_buf, u_buf) = refs
    b = pl.program_id(0)
    t = pl.program_id(1)
    in_ctx = t < nct
    row = jnp.where(in_ctx, nb, b)
    has_prev, has_next = _seq_edges(t, nct, nt)
    shift, scale = _mod_row(mod_ref, row, 0), 1.0 + _mod_row(mod_ref, row, 1)
    norm = lambda x: _rms(x, g_ref[...]) * scale + shift
    x_cur, x_prev, x_next = _pick_rows(x_refs, in_ctx)
    xn = jnp.concatenate([norm(x_cur), jnp.where(has_prev, norm(x_prev), 0.0), jnp.where(has_next, norm(x_next), 0.0)],
                         axis=0).astype(BF16)

    mla0 = 0
    rw0 = IN_SEGMENTS[0]
    pool0 = rw0 + RWKV_COLS
    conv0 = pool0 + POOL_WIDTH
    cw = CONV_WIDTH
    proj = lambda rows, c0, c1: jnp.dot(rows, w_ref[:, c0:c1], preferred_element_type=F32)

    c0 = 0
    while c0 < RWKV_COLS:
        c1 = min(c0 + 512, RWKV_COLS)
        _store_with_halo(rw_buf, slice(c0, c1), proj(xn, rw0 + c0, rw0 + c1))
        c0 = c1
    stages = _rwkv_features(rw_buf, mu_ref, w0_ref, a0_ref, w2_ref, a2_ref, g2_ref, kk_ref, ka_ref, rk_ref, e_ref, pack_ref)
    next(stages)
    f_mla = proj(xn[0:ROW_TILE], mla0, rw0)
    next(stages)
    _store_with_halo(pool_buf, slice(None), proj(xn, pool0, conv0))
    gb = proj(xn[0:ROW_TILE], conv0, conv0 + cw)
    next(stages)
    gc = proj(xn, conv0 + cw, conv0 + 2 * cw)
    next(stages)
    hx = proj(xn, conv0 + 2 * cw, conv0 + 3 * cw)
    for _ in stages:
        pass

    _mla_qkv(f_mla, cs_ref[...], sn_ref[...], gq_ref, gkv_ref, wq_ref, wkv_ref, q_ref, k_ref, v_ref)
    pos = lax.broadcasted_iota(jnp.int32, (ROW_TILE, 1), 0) + jnp.where(in_ctx, t, t - nct) * ROW_TILE
    _pool_mixer(pool_buf, pw_ref, ps_ref, pool_ref, pos, jnp.where(in_ctx, n_ctx, n_seq))
    _store_with_halo(u_buf, slice(None), gc * hx)
    z = cw_ref[0:1, :] * u_buf[LO - 1:HI - 1] + cw_ref[1:2, :] * u_buf[LO:HI] + cw_ref[2:3, :] * u_buf[LO + 1:HI + 1]
    conv_ref[...] = (gb * z).astype(conv_ref.dtype)


def _layer_spec(a, l):
    return _resident((None,) + a.shape[1:], lambda *_: (l,) + (0,) * (a.ndim - 1))


def _front(h_parts, l, mod, g, w_p, cs, sn, gq, gkv, wq, wkv, p, pool_w, pool_scale, conv_w, nb, nct, n_ctx):
    b_ = h_parts[0].shape[0]
    n_ = sum(h.shape[1] for h in h_parts)
    nt = n_ // ROW_TILE
    qk_w = MLA_HEADS * MLA_QK_PAD
    pack_w = RWKV_PACK_BLOCKS * RWKV_WIDTH
    full = lambda a: _resident(a.shape, lambda b, t: (0,) * a.ndim)
    row = lambda w: pl.BlockSpec((None, ROW_TILE, w), lambda b, t: (b, t, 0))
    tab = pl.BlockSpec((ROW_TILE, 128), lambda b, t: (t, 0))
    x_specs, x_args, tile0 = [], [], 0
    for h in h_parts:
        x_specs += _halo_specs(D_MODEL, h.shape[1], tile0)
        x_args += [h, h, h]
        tile0 += h.shape[1] // ROW_TILE
    consts = (mod, g.reshape(1, D_MODEL))
    mla = (gq.reshape(1, -1), gkv.reshape(1, -1), wq, wkv)
    rwk = (p["mu"], p["w0"], p["a0"], p["w2"], p["a2"], p["g2"], p["k_k"], p["k_a"], p["r_k"], p["seg"])
    mix = (pool_w, pool_scale, conv_w)
    return pl.pallas_call(
        functools.partial(_front_kernel, nparts=len(h_parts), nb=nb, nct=nct, nt=nt, n_ctx=n_ctx, n_seq=n_ - n_ctx),
        grid=(b_, nt),
        in_specs=x_specs + [full(a) for a in consts] + [_layer_spec(w_p, l), tab, tab]
                 + [full(a) for a in mla + rwk + mix],
        out_specs=(row(qk_w), row(qk_w), row(MLA_HEADS * MLA_V), row(pack_w), row(POOL_WIDTH), row(CONV_WIDTH)),
        out_shape=(
            jax.ShapeDtypeStruct((b_, n_, qk_w), BF16),
            jax.ShapeDtypeStruct((b_, n_, qk_w), BF16),
            jax.ShapeDtypeStruct((b_, n_, MLA_HEADS * MLA_V), BF16),
            jax.ShapeDtypeStruct((b_, n_, pack_w), F32),
            jax.ShapeDtypeStruct((b_, n_, POOL_WIDTH), BF16),
            jax.ShapeDtypeStruct((b_, n_, CONV_WIDTH), BF16),
        ),
        scratch_shapes=[pltpu.VMEM((HALO_ROWS, RWKV_COLS), F32), pltpu.VMEM((HALO_ROWS, POOL_WIDTH), F32),
                        pltpu.VMEM((HALO_ROWS, CONV_WIDTH), F32)],
        compiler_params=_cparams(("parallel", "parallel")),
        name="front",
    )(*x_args, *consts, w_p, cs, sn, *mla, *rwk, *mix)


ATTN_Q_TILE = 512
ATTN_SOFTMAX_ROWS = 128


def _attn_kernel(q_ref, k_ref, v_ref, o_ref, *, n_ctx, n_all, tq, with_ctx):
    s = pl.program_id(1)

    def attend(q_rows, o_rows, nk, nq):
        for h in range(MLA_HEADS):
            q = q_ref[q_rows, h * MLA_QK_PAD:(h + 1) * MLA_QK_PAD]
            k = k_ref[0:nk, h * MLA_QK_PAD:(h + 1) * MLA_QK_PAD]
            sc = lax.dot_general(q, k, (((1,), (1,)), ((), ())), preferred_element_type=F32)
            ps, ls = [], []
            for r0 in range(0, nq, ATTN_SOFTMAX_ROWS):
                blk = sc[r0:r0 + ATTN_SOFTMAX_ROWS]
                p = jnp.exp(blk - jnp.max(blk, axis=-1, keepdims=True))
                ls.append(jnp.sum(p, axis=-1, keepdims=True))
                ps.append(p.astype(BF16))
            p = jnp.concatenate(ps, axis=0)
            l = jnp.concatenate(ls, axis=0)
            o = jnp.dot(p, v_ref[0:nk, h * MLA_V:(h + 1) * MLA_V], preferred_element_type=F32)
            o_ref[o_rows, h * MLA_V:(h + 1) * MLA_V] = (o / l).astype(o_ref.dtype)

    if with_ctx:
        @pl.when(s == 0)
        def _():
            attend(slice(0, n_ctx), slice(0, n_ctx), n_ctx, n_ctx)

        @pl.when(s > 0)
        def _():
            r0 = pl.multiple_of(n_ctx + (s - 1) * tq, ROW_TILE)
            attend(pl.ds(r0, tq), pl.ds(r0, tq), n_all, tq)
    else:
        attend(pl.ds(pl.multiple_of(n_ctx + s * tq, ROW_TILE), tq), pl.ds(pl.multiple_of(s * tq, ROW_TILE), tq), n_all, tq)


def _attention(q, k, v, n_ctx, with_ctx):
    b_, n_, qk_w = q.shape
    n_seq = n_ - n_ctx
    tq = ATTN_Q_TILE if n_seq % ATTN_Q_TILE == 0 else ROW_TILE
    n_out = n_ if with_ctx else n_seq
    whole = lambda rows, w: pl.BlockSpec((None, rows, w), lambda b, s: (b, 0, 0))
    return pl.pallas_call(
        functools.partial(_attn_kernel, n_ctx=n_ctx, n_all=n_, tq=tq, with_ctx=with_ctx),
        grid=(b_, n_seq // tq + (1 if with_ctx else 0)),
        in_specs=[whole(n_, qk_w), whole(n_, qk_w), whole(n_, MLA_HEADS * MLA_V)],
        out_specs=whole(n_out, MLA_HEADS * MLA_V),
        out_shape=jax.ShapeDtypeStruct((b_, n_out, MLA_HEADS * MLA_V), BF16),
        compiler_params=_cparams(("parallel", "arbitrary")),
        name="mla_attention",
    )(q, k, v)


def _bd(x, m0):
    return jnp.concatenate([jnp.where(m0, x, 0.0), jnp.where(m0, 0.0, x)], axis=0)


def _wkv_kernel(r_ref, v_ref, kk_ref, lw_ref, kd_ref, a_ref, y_ref, s_ref, rw_s, y0_s, m_s, n_s, *, rev):
    c_ = WKV_CHUNK
    nsub = ROW_TILE // c_
    npair = RWKV_WIDTH // 128

    @pl.when(pl.program_id(1) == 0)
    def _():
        s_ref[...] = jnp.zeros_like(s_ref)

    ri = lax.broadcasted_iota(jnp.int32, (c_, c_), 0)
    ci = lax.broadcasted_iota(jnp.int32, (c_, c_), 1)
    tmat = ((ri <= ci) if rev else (ri >= ci)).astype(BF16)
    rc = lax.broadcasted_iota(jnp.int32, (c_, 128), 0)
    cc = lax.broadcasted_iota(jnp.int32, (c_, 128), 1) % c_
    strict = (rc < cc) if rev else (rc > cc)
    incl = (rc <= cc) if rev else (rc >= cc)
    eye = rc == cc
    m0_128 = lax.broadcasted_iota(jnp.int32, (1, 128), 1) < c_
    m0_256 = (lax.broadcasted_iota(jnp.int32, (1, 256), 1) % 128) < c_
    zeros_cv = jnp.zeros((c_, 128), F32)

    items = [(c, p) for c in range(nsub) for p in range(npair)]
    at, rt, bt, kt, bh, kh, vv, e_tot = {}, {}, {}, {}, {}, {}, {}, {}
    for c in range(nsub):
        rows = slice(c * c_, (c + 1) * c_)
        lw = lw_ref[rows, :]
        lc = _dot_exact_lhs(tmat, lw)
        ltot = jnp.sum(lw, axis=0, keepdims=True)
        e_neg = jnp.exp(-lc)
        e_end = jnp.exp(ltot - lc)
        kk = kk_ref[rows, :]
        kd = kd_ref[rows, :]
        bv = kk * a_ref[rows, :]
        vv[c] = v_ref[rows, :]
        at[c] = -kk * jnp.exp(lc - lw)
        rt[c] = r_ref[rows, :] * jnp.exp(lc)
        bt[c], kt[c] = bv * e_neg, kd * e_neg
        bh[c], kh[c] = bv * e_end, kd * e_end
        e_tot[c] = jnp.exp(ltot)
    sl = lambda p: slice(p * 128, (p + 1) * 128)

    g = {}
    for c, p in items:
        ar = jnp.concatenate([at[c][:, sl(p)], rt[c][:, sl(p)]], axis=0)
        bd = jnp.concatenate([_bd(bt[c][:, sl(p)], m0_128), _bd(kt[c][:, sl(p)], m0_128)], axis=0)
        g[c, p] = _dot_nt(ar, bd)
    av, x, pw, a_rb = {}, {}, {}, {}
    for c, p in items:
        a_ak = jnp.where(strict, g[c, p][0:c_, 128:256], 0.0)
        a_rk = jnp.where(incl, g[c, p][c_:2 * c_, 128:256], 0.0)
        av[c, p] = _dot(jnp.concatenate([a_ak, a_rk], axis=0), _bd(vv[c][:, sl(p)], m0_128))
        pw[c, p] = jnp.where(strict, g[c, p][0:c_, 0:128], 0.0)
        a_rb[c, p] = jnp.where(incl, g[c, p][c_:2 * c_, 0:128], 0.0)
    for c, p in items:
        x[c, p] = jnp.concatenate([at[c][:, sl(p)], av[c, p][0:c_]], axis=1)
    for it in range(6):
        dot = _dot_x3 if it < 2 else _dot
        for c, p in items:
            x[c, p] = x[c, p] + dot(pw[c, p], _bd(x[c, p], m0_256))
        if it < 5:
            for c, p in items:
                pw[c, p] = dot(pw[c, p], _bd(pw[c, p], m0_128))
    for c, p in items:
        ry = _dot(a_rb[c, p], _bd(x[c, p], m0_256))
        rw_s[c, p] = rt[c][:, sl(p)] + ry[:, 0:128]
        y0_s[c, p] = ry[:, 128:256] + av[c, p][c_:2 * c_]
    for c, p in items:
        bkt = jnp.concatenate([bh[c][:, sl(p)], kh[c][:, sl(p)]], axis=0).T
        rhs = jnp.concatenate([x[c, p], jnp.concatenate([zeros_cv, vv[c][:, sl(p)]], axis=1)], axis=0)
        z = _dot(bkt, rhs)
        m_s[c, p] = (jnp.where(m0_128, z[0:c_, 0:128], z[c_:2 * c_, 0:128])
                     + jnp.where(eye, e_tot[c][:, sl(p)], 0.0))
        n_s[c, p] = jnp.where(m0_128, z[0:c_, 128:256], z[c_:2 * c_, 128:256])

    for i in range(nsub):
        c = (nsub - 1 - i) if rev else i
        for p in range(npair):
            lhs = jnp.concatenate([rw_s[c, p], m_s[c, p]], axis=0)
            o = _dot_x3(lhs, _bd(s_ref[p], m0_128))
            y_ref[c * c_:(c + 1) * c_, sl(p)] = o[0:c_] + y0_s[c, p]
            s_ref[p] = o[c_:2 * c_] + n_s[c, p]


def _wkv(pack, nct, rev):
    b_, n_, _ = pack.shape
    nt = n_ // ROW_TILE
    w = RWKV_WIDTH
    nsub = ROW_TILE // WKV_CHUNK
    npair = w // 128
    if rev:
        order = lambda s: jnp.where(s < nct, nct - 1 - s, nt + nct - 1 - s)
    else:
        order = lambda s: s
    col = lambda j: pl.BlockSpec((None, ROW_TILE, w), lambda b, s: (b, order(s), j))
    d = 3 if rev else 0
    scr = lambda: pltpu.VMEM((nsub, npair, WKV_CHUNK, 128), F32)
    return pl.pallas_call(
        functools.partial(_wkv_kernel, rev=rev),
        grid=(b_, nt),
        in_specs=[col(0), col(1), col(2), col(5 + d), col(6 + d), col(7 + d)],
        out_specs=pl.BlockSpec((None, ROW_TILE, w), lambda b, s: (b, order(s), 0)),
        out_shape=jax.ShapeDtypeStruct((b_, n_, w), F32),
        scratch_shapes=[pltpu.VMEM((npair, WKV_CHUNK, 128), F32), scr(), scr(), scr(), scr()],
        compiler_params=_cparams(("parallel", "arbitrary")),
        name="wkv_bwd" if rev else "wkv_fwd",
    )(pack, pack, pack, pack, pack, pack)


def _mix_out_kernel(att_ref, yf_ref, yb_ref, g_ref, bonus_ref, lg_ref, lb_ref, e_ref, pool_ref, conv_ref,
                    mod_ref, w_ref, g2_ref, *refs, nb, nct, t_off):
    h_refs, (o_ref, xn_ref) = refs[:-2], refs[-2:]
    b = pl.program_id(0)
    t = pl.program_id(1) + t_off
    in_ctx = t < nct
    row = jnp.where(in_ctx, nb, b)
    h = h_refs[0][...] if len(h_refs) == 1 else jnp.where(in_ctx, h_refs[0][...], h_refs[1][...])
    y = yf_ref[...] + yb_ref[...]
    mean = _seg_sum(y, e_ref) * (1.0 / RWKV_HEAD)
    d = y - mean
    var = _seg_sum(d * d, e_ref) * (1.0 / RWKV_HEAD)
    yn = d * lax.rsqrt(var + RWKV_GN_EPS) * lg_ref[...] + lb_ref[...]
    rw = ((yn + bonus_ref[...]) * g_ref[...]).astype(BF16)
    mixed = jnp.concatenate([att_ref[...], rw, pool_ref[...], conv_ref[...]], axis=1)
    acc = jnp.dot(mixed, w_ref[...], preferred_element_type=F32)
    h = h + _mod_row(mod_ref, row, 2) * acc
    o_ref[...] = h
    xn_ref[...] = (_rms(h, g2_ref[...]) * (1.0 + _mod_row(mod_ref, row, 4)) + _mod_row(mod_ref, row, 3)).astype(BF16)


def _mix_out(att, yf, yb, pack, pool, conv, h_parts, l, mod, w_out, g2, p, nb, nct, with_ctx):
    b_, n_, gw = yf.shape
    t_off = 0 if with_ctx else nct
    n_out = n_ - t_off * ROW_TILE
    blk = lambda j: pl.BlockSpec((None, ROW_TILE, gw), lambda b, t: (b, t + t_off, j))
    full = lambda a: _resident(a.shape, lambda b, t: (0,) * a.ndim)
    h_specs, tile0 = [], 0
    for h in h_parts:
        h_specs.append(_token_tile_spec(h.shape[1], tile0, t_off))
        tile0 += h.shape[1] // ROW_TILE
    out = pl.BlockSpec((None, ROW_TILE, D_MODEL), lambda b, t: (b, t, 0))
    return pl.pallas_call(
        functools.partial(_mix_out_kernel, nb=nb, nct=nct, t_off=t_off),
        grid=(b_, n_out // ROW_TILE),
        in_specs=[pl.BlockSpec((None, ROW_TILE, gw), lambda b, t: (b, t, 0)),
                  blk(0), blk(0), blk(3), blk(4), full(p["ln_g"]), full(p["ln_b"]), full(p["seg"]),
                  blk(0), blk(0), full(mod), _layer_spec(w_out, l), full(g2)] + h_specs,
        out_specs=(out, out),
        out_shape=(jax.ShapeDtypeStruct((b_, n_out, D_MODEL), F32), jax.ShapeDtypeStruct((b_, n_out, D_MODEL), BF16)),
        compiler_params=_cparams(("parallel", "parallel")),
        name="mix_out",
    )(att, yf, yb, pack, pack, p["ln_g"], p["ln_b"], p["seg"], pool, conv, mod, w_out, g2, *h_parts)


MLP_HIDDEN_TILE = 512


def _mlp_kernel(h_ref, xn_ref, mod_ref, w1_ref, w2_ref, gf_ref, o_ref, acc_s, *, nb, ctx_rows, final):
    b = pl.program_id(0)
    t = pl.program_id(1)
    j = pl.program_id(2)

    @pl.when(j == 0)
    def _():
        acc_s[...] = jnp.zeros_like(acc_s)

    hid = jnp.dot(xn_ref[...], w1_ref[...], preferred_element_type=F32)
    hid = jnp.square(jnp.maximum(hid, 0.0))
    acc_s[...] += jnp.dot(hid.astype(BF16), w2_ref[...], preferred_element_type=F32)

    @pl.when(j == pl.num_programs(2) - 1)
    def _():
        def put(rows, mrow):
            o = h_ref[rows, :] + _mod_row(mod_ref, mrow, 5) * acc_s[rows, :]
            if final:
                o = _rms(o, gf_ref[...])
            o_ref[rows, :] = o

        first_row = jnp.where(t == 0, nb, b)
        if ctx_rows == 0:
            put(slice(None), b)
        elif ctx_rows == h_ref.shape[0]:
            put(slice(None), first_row)
        else:
            put(slice(0, ctx_rows), first_row)
            put(slice(ctx_rows, None), b)


def _mlp(h, xn, l, mod, w1, w2, g_final, nb, tm, ctx_rows, final):
    b_, n_, _ = h.shape
    th = MLP_HIDDEN_TILE
    tile = pl.BlockSpec((None, tm, D_MODEL), lambda b, t, j: (b, t, 0))
    return pl.pallas_call(
        functools.partial(_mlp_kernel, nb=nb, ctx_rows=ctx_rows, final=final),
        grid=(b_, n_ // tm, FF_HIDDEN // th),
        in_specs=[
            tile, tile,
            _resident((MOD_ROWS, 6 * D_MODEL), lambda b, t, j: (0, 0)),
            pl.BlockSpec((None, D_MODEL, th), lambda b, t, j: (l, 0, j)),
            pl.BlockSpec((None, th, D_MODEL), lambda b, t, j: (l, j, 0)),
            _resident((1, D_MODEL), lambda b, t, j: (0, 0)),
        ],
        out_specs=tile,
        out_shape=jax.ShapeDtypeStruct((b_, n_, D_MODEL), F32),
        scratch_shapes=[pltpu.VMEM((tm, D_MODEL), F32)],
        compiler_params=_cparams(("parallel", "parallel", "arbitrary")),
        name="mlp",
    )(h, xn, mod, w1, w2, g_final.reshape(1, D_MODEL))


def _rope_partner_perm():
    q = MLA_ROPE // 4
    i = np.arange(MLA_ROPE)
    return np.where((i // q) % 2 == 0, i + q, i - q)


def _rope_tables(n_ctx, n_seq):
    rows = n_seq // GRID_W
    row = jnp.repeat(jnp.arange(rows), GRID_W)
    col = jnp.tile(jnp.arange(GRID_W), rows)
    pos = jnp.stack([row, col], axis=-1).astype(F32)
    axis_dim = MLA_ROPE // 2
    inv_freq = ROPE_BASE ** (-jnp.arange(0, axis_dim, 2, dtype=F32) / axis_dim)
    ang = pos[:, :, None] * inv_freq
    cos, sin = jnp.cos(ang), jnp.sin(ang)
    cos64 = jnp.concatenate([cos[:, 0], cos[:, 0], cos[:, 1], cos[:, 1]], axis=-1)
    sin64 = jnp.concatenate([-sin[:, 0], sin[:, 0], -sin[:, 1], sin[:, 1]], axis=-1)
    pad = jnp.zeros((n_seq, 128 - MLA_ROPE), F32)
    cs_lat = jnp.concatenate([cos64, pad], axis=-1)
    sn_lat = jnp.concatenate([sin64, pad], axis=-1)
    cs_ctx = jnp.concatenate([jnp.ones((n_ctx, MLA_ROPE), F32), jnp.zeros((n_ctx, 128 - MLA_ROPE), F32)], axis=-1)
    return jnp.concatenate([cs_ctx, cs_lat], axis=0), jnp.concatenate([jnp.zeros((n_ctx, 128), F32), sn_lat], axis=0)


def _in_proj_weights(w_in):
    perm = _rope_partner_perm()
    kr0 = MLA_Q_RANK + MLA_KV_RANK
    return jnp.concatenate([w_in[:, :, :MLA_COLS], w_in[:, :, kr0:MLA_COLS][:, :, perm], w_in[:, :, MLA_COLS:]],
                           axis=2).astype(BF16)


def _layer_params(l, mla_w_uq, mla_w_ukv, rwkv_mu, rwkv_w0, rwkv_w2, rwkv_a0, rwkv_a2, rwkv_g2,
                  rwkv_k_k, rwkv_k_a, rwkv_r_k, rwkv_ln_g, rwkv_ln_b, seg):
    perm = _rope_partner_perm()
    wq = mla_w_uq[l].reshape(MLA_Q_RANK, MLA_HEADS, MLA_NOPE + MLA_ROPE)
    wq = jnp.concatenate([wq, wq[:, :, MLA_NOPE:][:, :, perm]], axis=-1).reshape(MLA_Q_RANK, MLA_HEADS * MLA_QK_PAD)

    wd = RWKV_WIDTH
    z = lambda r: jnp.zeros((r, wd), F32)
    w2 = jnp.concatenate([
        jnp.concatenate([rwkv_w2[l, 0], z(DECAY_RANK), z(256 - 2 * DECAY_RANK)], axis=0),
        jnp.concatenate([z(DECAY_RANK), rwkv_w2[l, 1], z(256 - 2 * DECAY_RANK)], axis=0)], axis=1)
    lead = 256 - 2 * ICL_RANK
    a2 = jnp.concatenate([
        jnp.concatenate([z(lead), rwkv_a2[l, 0], z(ICL_RANK)], axis=0),
        jnp.concatenate([z(lead), z(ICL_RANK), rwkv_a2[l, 1]], axis=0)], axis=1)
    rw = {
        "mu": rwkv_mu[l].reshape(1, RWKV_COLS),
        "w0": rwkv_w0[l].reshape(1, 2 * wd),
        "a0": rwkv_a0[l].reshape(1, 2 * wd),
        "w2": w2.astype(BF16),
        "a2": a2.astype(BF16),
        "g2": rwkv_g2[l].astype(BF16),
        "k_k": rwkv_k_k[l].reshape(1, wd),
        "k_a": rwkv_k_a[l].reshape(1, wd),
        "r_k": rwkv_r_k[l].reshape(1, wd),
        "ln_g": rwkv_ln_g[l].reshape(1, wd),
        "ln_b": rwkv_ln_b[l].reshape(1, wd),
        "seg": seg,
    }
    return wq.astype(BF16), mla_w_ukv[l].astype(BF16), rw


def kernel(x, c, ctx, c_ctx, ada_w, ada_b, norm1_g, norm2_g, w_in, mla_q_norm_g, mla_w_uq, mla_kv_norm_g, mla_w_ukv, rwkv_mu, rwkv_w0, rwkv_w2, rwkv_a0, rwkv_a2, rwkv_g2, rwkv_k_k, rwkv_k_a, rwkv_r_k, rwkv_ln_g, rwkv_ln_b, pool_w, pool_scale, conv_w, w_out, mlp_w1, mlp_w2, final_norm_g):
    nb, n_seq, d = x.shape
    n_ctx = ctx.shape[1]
    depth = ada_w.shape[0]
    assert d == D_MODEL and nb < MOD_ROWS
    assert n_ctx % ROW_TILE == 0 and n_seq % ROW_TILE == 0 and n_seq % GRID_W == 0
    nct = n_ctx // ROW_TILE

    cc = jnp.zeros((MOD_ROWS, D_MODEL), F32).at[:nb].set(c).at[nb].set(c_ctx)
    mod = _modulation(cc, ada_w, ada_b)
    cs, sn = _rope_tables(n_ctx, n_seq)
    head = np.arange(RWKV_WIDTH) // RWKV_HEAD
    seg = jnp.asarray(head[:, None] == head[None, :], dtype=BF16)

    w_in_p = _in_proj_weights(w_in)
    w_out_b, w1_b, w2_b = w_out.astype(BF16), mlp_w1.astype(BF16), mlp_w2.astype(BF16)
    h_parts = (ctx, x)
    for l in range(depth):
        last = l == depth - 1
        with_ctx = not last
        wq, wkv, rw_p = _layer_params(
            l, mla_w_uq, mla_w_ukv, rwkv_mu, rwkv_w0, rwkv_w2, rwkv_a0, rwkv_a2, rwkv_g2,
            rwkv_k_k, rwkv_k_a, rwkv_r_k, rwkv_ln_g, rwkv_ln_b, seg)

        q, k, v, pack, pool, conv = _front(
            h_parts, l, mod[l], norm1_g[l], w_in_p, cs, sn, mla_q_norm_g[l], mla_kv_norm_g[l], wq, wkv, rw_p,
            pool_w[l].astype(BF16), pool_scale[l].reshape(1, POOL_WIDTH), conv_w[l], nb, nct, n_ctx)
        att = _attention(q, k, v, n_ctx, with_ctx)
        y_f = _wkv(pack, nct, rev=False)
        y_b = _wkv(pack, nct, rev=True)
        h_mid, xn_mid = _mix_out(att, y_f, y_b, pack, pool, conv, h_parts, l, mod[l], w_out_b,
                                 norm2_g[l].reshape(1, D_MODEL), rw_p, nb, nct, with_ctx)

        n_rows = h_mid.shape[1]
        if with_ctx:
            tm = 3 * ROW_TILE if n_rows % (3 * ROW_TILE) == 0 else ROW_TILE
            ctx_rows = n_ctx
            assert n_ctx <= tm, "context longer than one MLP row tile"
        else:
            tm = 2 * ROW_TILE if n_rows % (2 * ROW_TILE) == 0 else ROW_TILE
            ctx_rows = 0
        h_parts = (_mlp(h_mid, xn_mid, l, mod[l], w1_b, w2_b, final_norm_g, nb, tm, ctx_rows, last),)
    return h_parts[0]
```

```python
import functools

import numpy as np
import jax
import jax.numpy as jnp
from jax import lax
from jax.experimental import pallas as pl
from jax.experimental.pallas import tpu as pltpu

F32 = jnp.float32
BF16 = jnp.bfloat16

D_MODEL = 2048
GRID_W = 64
NORM_EPS = 1e-6
GROUP_WIDTH = D_MODEL // 4

MLA_NOPE = 128
MLA_ROPE = 64
MLA_V = 128
MLA_HEADS = GROUP_WIDTH // MLA_V
MLA_Q_RANK = 384
MLA_KV_RANK = 128
ROPE_BASE = 10000.0
MLA_QK_PAD = 256
MLA_FEAT = 640

RWKV_HEAD = 64
RWKV_HEADS = GROUP_WIDTH // RWKV_HEAD
RWKV_WIDTH = RWKV_HEADS * RWKV_HEAD
DECAY_RANK = 96
ICL_RANK = 96
GATE_RANK = 256
RWKV_GN_EPS = 64e-5
RWKV_COLS = 3 * RWKV_WIDTH + 2 * DECAY_RANK + 2 * ICL_RANK + GATE_RANK
WKV_CHUNK = 64
RWKV_PACK_BLOCKS = 11

POOL_WINDOWS = (2, 4, 8, 16)
POOL_WIDTH = GROUP_WIDTH
POOL_GROUP = POOL_WIDTH // len(POOL_WINDOWS)
CONV_WIDTH = GROUP_WIDTH
CONV_COLS = 3 * CONV_WIDTH
FF_HIDDEN = 4 * D_MODEL
MLA_COLS = MLA_Q_RANK + MLA_KV_RANK + MLA_ROPE

ROW_TILE = 256
HALO = 8
MOD_ROWS = 16
VMEM_LIMIT = 56 * 1024 * 1024


def _cparams(sem, vmem_limit=VMEM_LIMIT):
    return pltpu.CompilerParams(dimension_semantics=sem, vmem_limit_bytes=vmem_limit)


def _resident(shape, index_map):
    return pl.BlockSpec(shape, index_map, pipeline_mode=pl.Buffered(1))


def _rms(x, g):
    return x * lax.rsqrt(jnp.mean(x * x, axis=-1, keepdims=True) + NORM_EPS) * g


def _dot(a, b):
    return jnp.dot(a.astype(BF16), b.astype(BF16), preferred_element_type=F32)


def _dot_nt(a, b):
    return lax.dot_general(a.astype(BF16), b.astype(BF16), (((1,), (1,)), ((), ())), preferred_element_type=F32)


def _split3(x):
    h1 = x.astype(BF16)
    r1 = x - h1.astype(F32)
    h2 = r1.astype(BF16)
    h3 = (r1 - h2.astype(F32)).astype(BF16)
    return h1, h2, h3


def _dot_exact_lhs(a01, x):
    h1, h2, h3 = _split3(x)
    d = lambda h: jnp.dot(a01, h, preferred_element_type=F32)
    return d(h1) + d(h2) + d(h3)


def _dot_x3(a, b):
    a1 = a.astype(BF16)
    a2 = (a - a1.astype(F32)).astype(BF16)
    b1 = b.astype(BF16)
    b2 = (b - b1.astype(F32)).astype(BF16)
    m = a.shape[0]
    t = jnp.dot(jnp.concatenate([a1, a2], axis=0), b1, preferred_element_type=F32)
    return t[0:m] + (t[m:2 * m] + jnp.dot(a1, b2, preferred_element_type=F32))


def _mod_kernel(c_ref, w_ref, b_ref, o_ref):
    c = c_ref[...]
    s = c * jax.nn.sigmoid(c)
    o_ref[...] = _dot(s, w_ref[...]) + b_ref[...]


def _modulation(cc, ada_w, ada_b):
    depth = ada_w.shape[0]
    tn = 1024
    return pl.pallas_call(
        _mod_kernel,
        grid=(depth, 6 * D_MODEL // tn),
        in_specs=[
            pl.BlockSpec((MOD_ROWS, D_MODEL), lambda l, j: (0, 0)),
            pl.BlockSpec((None, D_MODEL, tn), lambda l, j: (l, 0, j)),
            pl.BlockSpec((None, 1, tn), lambda l, j: (l, 0, j)),
        ],
        out_specs=pl.BlockSpec((None, MOD_ROWS, tn), lambda l, j: (l, 0, j)),
        out_shape=jax.ShapeDtypeStruct((depth, MOD_ROWS, 6 * D_MODEL), F32),
        compiler_params=_cparams(("parallel", "parallel")),
        name="modulation",
    )(cc, ada_w, ada_b.reshape(depth, 1, 6 * D_MODEL))


def _mod_row(mod_ref, row, k):
    return mod_ref[pl.ds(row, 1), k * D_MODEL:(k + 1) * D_MODEL]


def _halo_specs(width, n_rows, tile0=0):
    per = ROW_TILE // HALO
    last_tile = n_rows // ROW_TILE - 1
    last = n_rows // HALO - 1
    clamp = lambda i, hi: jnp.clip(i, 0, hi)
    cur = pl.BlockSpec((None, ROW_TILE, width), lambda b, t: (b, clamp(t - tile0, last_tile), 0))
    prev = pl.BlockSpec((None, HALO, width), lambda b, t: (b, clamp((t - tile0) * per - 1, last), 0))
    nxt = pl.BlockSpec((None, HALO, width), lambda b, t: (b, clamp((t - tile0 + 1) * per, last), 0))
    return cur, prev, nxt


def _token_tile_spec(n_rows, tile0, t_off):
    last_tile = n_rows // ROW_TILE - 1
    return pl.BlockSpec((None, ROW_TILE, D_MODEL), lambda b, t: (b, jnp.clip(t + t_off - tile0, 0, last_tile), 0))


def _seq_edges(t, nct, nt):
    has_prev = jnp.logical_and(t != 0, t != nct)
    has_next = jnp.logical_and(t != nct - 1, t != nt - 1)
    return has_prev, has_next


def _seg_sum(x, e_ref):
    h1 = x.astype(BF16)
    h2 = (x - h1.astype(F32)).astype(BF16)
    e = e_ref[...]
    return jnp.dot(h1, e, preferred_element_type=F32) + jnp.dot(h2, e, preferred_element_type=F32)


IN_SEGMENTS = (MLA_FEAT, RWKV_COLS, POOL_WIDTH, CONV_COLS)
IN_COLS_PADDED = sum(IN_SEGMENTS)
HALO_ROWS = ROW_TILE + 2 * HALO
LO, HI = HALO, HALO + ROW_TILE


def _rope(x, cs, sn):
    return x * cs + pltpu.roll(x, 64, 1) * sn


def _store_with_halo(buf, cols, res):
    buf[LO:HI, cols] = res[0:ROW_TILE]
    buf[0:LO, cols] = res[ROW_TILE:ROW_TILE + HALO]
    buf[HI:HI + HALO, cols] = res[ROW_TILE + HALO:ROW_TILE + 2 * HALO]


def _mla_qkv(f, cs, sn, gq_ref, gkv_ref, wq_ref, wkv_ref, q_ref, k_ref, v_ref):
    scale = (MLA_NOPE + MLA_ROPE) ** -0.5
    q = _dot(_rms(f[:, 0:MLA_Q_RANK], gq_ref[...]), wq_ref[...])
    kv = _dot(_rms(f[:, MLA_Q_RANK:MLA_Q_RANK + MLA_KV_RANK], gkv_ref[...]), wkv_ref[...])
    k_pe = _rope(f[:, 512:640], cs, sn).astype(BF16)
    for h in range(MLA_HEADS):
        o = h * MLA_QK_PAD
        q_ref[:, o:o + 128] = (q[:, o:o + 128] * scale).astype(BF16)
        q_ref[:, o + 128:o + 256] = (_rope(q[:, o + 128:o + 256], cs, sn) * scale).astype(BF16)
        k_ref[:, o:o + 128] = kv[:, o:o + 128].astype(BF16)
        k_ref[:, o + 128:o + 256] = k_pe
        v_ref[:, h * MLA_V:(h + 1) * MLA_V] = kv[:, o + 128:o + 256].astype(BF16)


def _rwkv_features(buf, mu_ref, w0_ref, a0_ref, w2_ref, a2_ref, g2_ref, kk_ref, ka_ref, rk_ref, e_ref, o_ref):
    w = RWKV_WIDTH

    def shifted(cols):
        f = buf[LO:HI, cols]
        return f + mu_ref[:, cols] * (0.5 * (buf[LO - 1:HI - 1, cols] + buf[LO + 1:HI + 1, cols]) - f)

    def put(i, val):
        o_ref[:, i * w:(i + 1) * w] = val

    xw = w0_ref[...] + _dot(jnp.tanh(shifted(slice(1536, 1792))), w2_ref[...])
    lw = -float(np.exp(-0.5)) * jax.nn.sigmoid(xw)
    put(5, lw[:, 0:w])
    put(8, lw[:, w:2 * w])
    yield
    a = jax.nn.sigmoid(a0_ref[...] + _dot(shifted(slice(1664, 1920)), a2_ref[...]))
    put(7, a[:, 0:w])
    put(10, a[:, w:2 * w])
    put(3, _dot(jax.nn.sigmoid(shifted(slice(1920, 2176))), g2_ref[...]))
    yield
    k = shifted(slice(w, 2 * w))
    kkv = k * kk_ref[...]
    put(2, kkv / jnp.maximum(jnp.sqrt(_seg_sum(kkv * kkv, e_ref)), 1e-12))
    yield
    r = shifted(slice(0, w))
    put(0, r)
    rk = r * rk_ref[...]
    ka = ka_ref[...]
    k_f = k * (1.0 + (a[:, 0:w] - 1.0) * ka)
    k_b = k * (1.0 + (a[:, w:2 * w] - 1.0) * ka)
    put(6, k_f)
    put(9, k_b)
    yield
    v = shifted(slice(2 * w, 3 * w))
    put(1, v)
    put(4, _seg_sum(rk * k_f + rk * k_b, e_ref) * v)


def _pool_mixer(buf, pw_ref, ps_ref, o_ref, pos, n_own):
    for gi, win in enumerate(POOL_WINDOWS):
        hw = win // 2
        cols = slice(gi * POOL_GROUP, (gi + 1) * POOL_GROUP)
        s = buf[LO - hw:HI - hw, cols]
        for o in range(-hw + 1, hw):
            s = s + buf[LO + o:HI + o, cols]
        cnt = (jnp.minimum(pos + hw, n_own) - jnp.maximum(pos - hw, 0)).astype(F32)
        z = s / cnt - buf[LO:HI, cols]
        o_ref[:, cols] = (_dot(z, pw_ref[gi]) * ps_ref[:, cols]).astype(o_ref.dtype)


def _pick_rows(refs, in_ctx):
    if len(refs) == 3:
        return [r[...] for r in refs]
    return [jnp.where(in_ctx, c[...], x[...]) for c, x in zip(refs[0:3], refs[3:6])]


def _front_kernel(*refs, nparts, nb, nct, nt, n_ctx, n_seq):
    x_refs, refs = refs[:3 * nparts], refs[3 * nparts:]
    (mod_ref, g_ref, w_ref, cs_ref, sn_ref, gq_ref, gkv_ref, wq_ref, wkv_ref,
     mu_ref, w0_ref, a0_ref, w2_ref, a2_ref, g2_ref, kk_ref, ka_ref, rk_ref, e_ref,
     pw_ref, ps_ref, cw_ref,
     q_ref, k_ref, v_ref, pack_ref, pool_ref, conv_ref,
     rw_buf, pool_buf, u_buf) = refs
    b = pl.program_id(0)
    t = pl.program_id(1)
    in_ctx = t < nct
    row = jnp.where(in_ctx, nb, b)
    has_prev, has_next = _seq_edges(t, nct, nt)
    shift, scale = _mod_row(mod_ref, row, 0), 1.0 + _mod_row(mod_ref, row, 1)
    norm = lambda x: _rms(x, g_ref[...]) * scale + shift
    x_cur, x_prev, x_next = _pick_rows(x_refs, in_ctx)
    xn = jnp.concatenate([norm(x_cur), jnp.where(has_prev, norm(x_prev), 0.0), jnp.where(has_next, norm(x_next), 0.0)],
                         axis=0).astype(BF16)

    mla0 = 0
    rw0 = IN_SEGMENTS[0]
    pool0 = rw0 + RWKV_COLS
    conv0 = pool0 + POOL_WIDTH
    cw = CONV_WIDTH
    proj = lambda rows, c0, c1: jnp.dot(rows, w_ref[:, c0:c1], preferred_element_type=F32)

    c0 = 0
    while c0 < RWKV_COLS:
        c1 = min(c0 + 512, RWKV_COLS)
        _store_with_halo(rw_buf, slice(c0, c1), proj(xn, rw0 + c0, rw0 + c1))
        c0 = c1
    stages = _rwkv_features(rw_buf, mu_ref, w0_ref, a0_ref, w2_ref, a2_ref, g2_ref, kk_ref, ka_ref, rk_ref, e_ref, pack_ref)
    next(stages)
    f_mla = proj(xn[0:ROW_TILE], mla0, rw0)
    _store_with_halo(pool_buf, slice(None), proj(xn, pool0, conv0))
    next(stages)
    gb = proj(xn[0:ROW_TILE], conv0, conv0 + cw)
    _mla_qkv(f_mla, cs_ref[...], sn_ref[...], gq_ref, gkv_ref, wq_ref, wkv_ref, q_ref, k_ref, v_ref)
    next(stages)
    gc = proj(xn, conv0 + cw, conv0 + 2 * cw)
    pos = lax.broadcasted_iota(jnp.int32, (ROW_TILE, 1), 0) + jnp.where(in_ctx, t, t - nct) * ROW_TILE
    _pool_mixer(pool_buf, pw_ref, ps_ref, pool_ref, pos, jnp.where(in_ctx, n_ctx, n_seq))
    next(stages)
    hx = proj(xn, conv0 + 2 * cw, conv0 + 3 * cw)
    for _ in stages:
        pass
    _store_with_halo(u_buf, slice(None), gc * hx)
    z = cw_ref[0:1, :] * u_buf[LO - 1:HI - 1] + cw_ref[1:2, :] * u_buf[LO:HI] + cw_ref[2:3, :] * u_buf[LO + 1:HI + 1]
    conv_ref[...] = (gb * z).astype(conv_ref.dtype)


def _layer_spec(a, l):
    return _resident((None,) + a.shape[1:], lambda *_: (l,) + (0,) * (a.ndim - 1))


def _front(h_parts, l, mod, g, w_p, cs, sn, gq, gkv, wq, wkv, p, pool_w, pool_scale, conv_w, nb, nct, n_ctx):
    b_ = h_parts[0].shape[0]
    n_ = sum(h.shape[1] for h in h_parts)
    nt = n_ // ROW_TILE
    qk_w = MLA_HEADS * MLA_QK_PAD
    pack_w = RWKV_PACK_BLOCKS * RWKV_WIDTH
    full = lambda a: _resident(a.shape, lambda b, t: (0,) * a.ndim)
    row = lambda w: pl.BlockSpec((None, ROW_TILE, w), lambda b, t: (b, t, 0))
    tab = pl.BlockSpec((ROW_TILE, 128), lambda b, t: (t, 0))
    x_specs, x_args, tile0 = [], [], 0
    for h in h_parts:
        x_specs += _halo_specs(D_MODEL, h.shape[1], tile0)
        x_args += [h, h, h]
        tile0 += h.shape[1] // ROW_TILE
    consts = (mod, g.reshape(1, D_MODEL))
    mla = (gq.reshape(1, -1), gkv.reshape(1, -1), wq, wkv)
    rwk = (p["mu"], p["w0"], p["a0"], p["w2"], p["a2"], p["g2"], p["k_k"], p["k_a"], p["r_k"], p["seg"])
    mix = (pool_w, pool_scale, conv_w)
    return pl.pallas_call(
        functools.partial(_front_kernel, nparts=len(h_parts), nb=nb, nct=nct, nt=nt, n_ctx=n_ctx, n_seq=n_ - n_ctx),
        grid=(b_, nt),
        in_specs=x_specs + [full(a) for a in consts] + [_layer_spec(w_p, l), tab, tab]
                 + [full(a) for a in mla + rwk + mix],
        out_specs=(row(qk_w), row(qk_w), row(MLA_HEADS * MLA_V), row(pack_w), row(POOL_WIDTH), row(CONV_WIDTH)),
        out_shape=(
            jax.ShapeDtypeStruct((b_, n_, qk_w), BF16),
            jax.ShapeDtypeStruct((b_, n_, qk_w), BF16),
            jax.ShapeDtypeStruct((b_, n_, MLA_HEADS * MLA_V), BF16),
            jax.ShapeDtypeStruct((b_, n_, pack_w), F32),
            jax.ShapeDtypeStruct((b_, n_, POOL_WIDTH), BF16),
            jax.ShapeDtypeStruct((b_, n_, CONV_WIDTH), BF16),
        ),
        scratch_shapes=[pltpu.VMEM((HALO_ROWS, RWKV_COLS), F32), pltpu.VMEM((HALO_ROWS, POOL_WIDTH), F32),
                        pltpu.VMEM((HALO_ROWS, CONV_WIDTH), F32)],
        compiler_params=_cparams(("parallel", "parallel")),
        name="front",
    )(*x_args, *consts, w_p, cs, sn, *mla, *rwk, *mix)


ATTN_Q_TILE = 512
ATTN_SOFTMAX_ROWS = 128


def _attn_kernel(q_ref, k_ref, v_ref, o_ref, *, n_ctx, n_all, tq, with_ctx):
    s = pl.program_id(1)

    def attend(q_rows, o_rows, nk, nq):
        for h in range(MLA_HEADS):
            q = q_ref[q_rows, h * MLA_QK_PAD:(h + 1) * MLA_QK_PAD]
            k = k_ref[0:nk, h * MLA_QK_PAD:(h + 1) * MLA_QK_PAD]
            sc = lax.dot_general(q, k, (((1,), (1,)), ((), ())), preferred_element_type=F32)
            ps, ls = [], []
            for r0 in range(0, nq, ATTN_SOFTMAX_ROWS):
                blk = sc[r0:r0 + ATTN_SOFTMAX_ROWS]
                p = jnp.exp(blk - jnp.max(blk, axis=-1, keepdims=True))
                ls.append(jnp.sum(p, axis=-1, keepdims=True))
                ps.append(p.astype(BF16))
            p = jnp.concatenate(ps, axis=0)
            l = jnp.concatenate(ls, axis=0)
            o = jnp.dot(p, v_ref[0:nk, h * MLA_V:(h + 1) * MLA_V], preferred_element_type=F32)
            o_ref[o_rows, h * MLA_V:(h + 1) * MLA_V] = (o / l).astype(o_ref.dtype)

    if with_ctx:
        @pl.when(s == 0)
        def _():
            attend(slice(0, n_ctx), slice(0, n_ctx), n_ctx, n_ctx)

        @pl.when(s > 0)
        def _():
            r0 = pl.multiple_of(n_ctx + (s - 1) * tq, ROW_TILE)
            attend(pl.ds(r0, tq), pl.ds(r0, tq), n_all, tq)
    else:
        attend(pl.ds(pl.multiple_of(n_ctx + s * tq, ROW_TILE), tq), pl.ds(pl.multiple_of(s * tq, ROW_TILE), tq), n_all, tq)


def _attention(q, k, v, n_ctx, with_ctx):
    b_, n_, qk_w = q.shape
    n_seq = n_ - n_ctx
    tq = ATTN_Q_TILE if n_seq % ATTN_Q_TILE == 0 else ROW_TILE
    n_out = n_ if with_ctx else n_seq
    whole = lambda rows, w: pl.BlockSpec((None, rows, w), lambda b, s: (b, 0, 0))
    return pl.pallas_call(
        functools.partial(_attn_kernel, n_ctx=n_ctx, n_all=n_, tq=tq, with_ctx=with_ctx),
        grid=(b_, n_seq // tq + (1 if with_ctx else 0)),
        in_specs=[whole(n_, qk_w), whole(n_, qk_w), whole(n_, MLA_HEADS * MLA_V)],
        out_specs=whole(n_out, MLA_HEADS * MLA_V),
        out_shape=jax.ShapeDtypeStruct((b_, n_out, MLA_HEADS * MLA_V), BF16),
        compiler_params=_cparams(("parallel", "arbitrary")),
        name="mla_attention",
    )(q, k, v)


def _bd(x, m0):
    return jnp.concatenate([jnp.where(m0, x, 0.0), jnp.where(m0, 0.0, x)], axis=0)


def _wkv_kernel(r_ref, v_ref, kk_ref, lw_ref, kd_ref, a_ref, y_ref, s_ref, rw_s, y0_s, m_s, n_s, *, rev):
    c_ = WKV_CHUNK
    nsub = ROW_TILE // c_
    npair = RWKV_WIDTH // 128

    @pl.when(pl.program_id(1) == 0)
    def _():
        s_ref[...] = jnp.zeros_like(s_ref)

    ri = lax.broadcasted_iota(jnp.int32, (c_, c_), 0)
    ci = lax.broadcasted_iota(jnp.int32, (c_, c_), 1)
    tmat = ((ri <= ci) if rev else (ri >= ci)).astype(BF16)
    rc = lax.broadcasted_iota(jnp.int32, (c_, 128), 0)
    cc = lax.broadcasted_iota(jnp.int32, (c_, 128), 1) % c_
    strict = (rc < cc) if rev else (rc > cc)
    incl = (rc <= cc) if rev else (rc >= cc)
    eye = rc == cc
    m0_128 = lax.broadcasted_iota(jnp.int32, (1, 128), 1) < c_
    m0_256 = (lax.broadcasted_iota(jnp.int32, (1, 256), 1) % 128) < c_
    zeros_cv = jnp.zeros((c_, 128), F32)

    items = [(c, p) for c in range(nsub) for p in range(npair)]
    at, rt, bt, kt, bh, kh, vv, e_tot = {}, {}, {}, {}, {}, {}, {}, {}
    for c in range(nsub):
        rows = slice(c * c_, (c + 1) * c_)
        lw = lw_ref[rows, :]
        lc = _dot_exact_lhs(tmat, lw)
        ltot = jnp.sum(lw, axis=0, keepdims=True)
        e_neg = jnp.exp(-lc)
        e_end = jnp.exp(ltot - lc)
        kk = kk_ref[rows, :]
        kd = kd_ref[rows, :]
        bv = kk * a_ref[rows, :]
        vv[c] = v_ref[rows, :]
        at[c] = -kk * jnp.exp(lc - lw)
        rt[c] = r_ref[rows, :] * jnp.exp(lc)
        bt[c], kt[c] = bv * e_neg, kd * e_neg
        bh[c], kh[c] = bv * e_end, kd * e_end
        e_tot[c] = jnp.exp(ltot)
    sl = lambda p: slice(p * 128, (p + 1) * 128)

    g = {}
    for c, p in items:
        ar = jnp.concatenate([at[c][:, sl(p)], rt[c][:, sl(p)]], axis=0)
        bd = jnp.concatenate([_bd(bt[c][:, sl(p)], m0_128), _bd(kt[c][:, sl(p)], m0_128)], axis=0)
        g[c, p] = _dot_nt(ar, bd)
    av, x, pw, a_rb = {}, {}, {}, {}
    for c, p in items:
        a_ak = jnp.where(strict, g[c, p][0:c_, 128:256], 0.0)
        a_rk = jnp.where(incl, g[c, p][c_:2 * c_, 128:256], 0.0)
        av[c, p] = _dot(jnp.concatenate([a_ak, a_rk], axis=0), _bd(vv[c][:, sl(p)], m0_128))
        pw[c, p] = jnp.where(strict, g[c, p][0:c_, 0:128], 0.0)
        a_rb[c, p] = jnp.where(incl, g[c, p][c_:2 * c_, 0:128], 0.0)
    for c, p in items:
        x[c, p] = jnp.concatenate([at[c][:, sl(p)], av[c, p][0:c_]], axis=1)
    for it in range(6):
        dot = _dot_x3 if it < 2 else _dot
        for c, p in items:
            x[c, p] = x[c, p] + dot(pw[c, p], _bd(x[c, p], m0_256))
        if it < 5:
            for c, p in items:
                pw[c, p] = dot(pw[c, p], _bd(pw[c, p], m0_128))
    for c, p in items:
        ry = _dot(a_rb[c, p], _bd(x[c, p], m0_256))
        rw_s[c, p] = rt[c][:, sl(p)] + ry[:, 0:128]
        y0_s[c, p] = ry[:, 128:256] + av[c, p][c_:2 * c_]
    for c, p in items:
        bkt = jnp.concatenate([bh[c][:, sl(p)], kh[c][:, sl(p)]], axis=0).T
        rhs = jnp.concatenate([x[c, p], jnp.concatenate([zeros_cv, vv[c][:, sl(p)]], axis=1)], axis=0)
        z = _dot(bkt, rhs)
        m_s[c, p] = (jnp.where(m0_128, z[0:c_, 0:128], z[c_:2 * c_, 0:128])
                     + jnp.where(eye, e_tot[c][:, sl(p)], 0.0))
        n_s[c, p] = jnp.where(m0_128, z[0:c_, 128:256], z[c_:2 * c_, 128:256])

    for i in range(nsub):
        c = (nsub - 1 - i) if rev else i
        for p in range(npair):
            lhs = jnp.concatenate([rw_s[c, p], m_s[c, p]], axis=0)
            o = _dot_x3(lhs, _bd(s_ref[p], m0_128))
            y_ref[c * c_:(c + 1) * c_, sl(p)] = o[0:c_] + y0_s[c, p]
            s_ref[p] = o[c_:2 * c_] + n_s[c, p]


def _wkv(pack, nct, rev):
    b_, n_, _ = pack.shape
    nt = n_ // ROW_TILE
    w = RWKV_WIDTH
    nsub = ROW_TILE // WKV_CHUNK
    npair = w // 128
    if rev:
        order = lambda s: jnp.where(s < nct, nct - 1 - s, nt + nct - 1 - s)
    else:
        order = lambda s: s
    col = lambda j: pl.BlockSpec((None, ROW_TILE, w), lambda b, s: (b, order(s), j))
    d = 3 if rev else 0
    scr = lambda: pltpu.VMEM((nsub, npair, WKV_CHUNK, 128), F32)
    return pl.pallas_call(
        functools.partial(_wkv_kernel, rev=rev),
        grid=(b_, nt),
        in_specs=[col(0), col(1), col(2), col(5 + d), col(6 + d), col(7 + d)],
        out_specs=pl.BlockSpec((None, ROW_TILE, w), lambda b, s: (b, order(s), 0)),
        out_shape=jax.ShapeDtypeStruct((b_, n_, w), F32),
        scratch_shapes=[pltpu.VMEM((npair, WKV_CHUNK, 128), F32), scr(), scr(), scr(), scr()],
        compiler_params=_cparams(("parallel", "arbitrary")),
        name="wkv_bwd" if rev else "wkv_fwd",
    )(pack, pack, pack, pack, pack, pack)


def _mix_out_kernel(att_ref, yf_ref, yb_ref, g_ref, bonus_ref, lg_ref, lb_ref, e_ref, pool_ref, conv_ref,
                    mod_ref, w_ref, g2_ref, *refs, nb, nct, t_off):
    h_refs, (o_ref, xn_ref) = refs[:-2], refs[-2:]
    b = pl.program_id(0)
    t = pl.program_id(1) + t_off
    in_ctx = t < nct
    row = jnp.where(in_ctx, nb, b)
    h = h_refs[0][...] if len(h_refs) == 1 else jnp.where(in_ctx, h_refs[0][...], h_refs[1][...])
    y = yf_ref[...] + yb_ref[...]
    mean = _seg_sum(y, e_ref) * (1.0 / RWKV_HEAD)
    d = y - mean
    var = _seg_sum(d * d, e_ref) * (1.0 / RWKV_HEAD)
    yn = d * lax.rsqrt(var + RWKV_GN_EPS) * lg_ref[...] + lb_ref[...]
    rw = ((yn + bonus_ref[...]) * g_ref[...]).astype(BF16)
    mixed = jnp.concatenate([att_ref[...], rw, pool_ref[...], conv_ref[...]], axis=1)
    acc = jnp.dot(mixed, w_ref[...], preferred_element_type=F32)
    h = h + _mod_row(mod_ref, row, 2) * acc
    o_ref[...] = h
    xn_ref[...] = (_rms(h, g2_ref[...]) * (1.0 + _mod_row(mod_ref, row, 4)) + _mod_row(mod_ref, row, 3)).astype(BF16)


def _mix_out(att, yf, yb, pack, pool, conv, h_parts, l, mod, w_out, g2, p, nb, nct, with_ctx):
    b_, n_, gw = yf.shape
    t_off = 0 if with_ctx else nct
    n_out = n_ - t_off * ROW_TILE
    blk = lambda j: pl.BlockSpec((None, ROW_TILE, gw), lambda b, t: (b, t + t_off, j))
    full = lambda a: _resident(a.shape, lambda b, t: (0,) * a.ndim)
    h_specs, tile0 = [], 0
    for h in h_parts:
        h_specs.append(_token_tile_spec(h.shape[1], tile0, t_off))
        tile0 += h.shape[1] // ROW_TILE
    out = pl.BlockSpec((None, ROW_TILE, D_MODEL), lambda b, t: (b, t, 0))
    return pl.pallas_call(
        functools.partial(_mix_out_kernel, nb=nb, nct=nct, t_off=t_off),
        grid=(b_, n_out // ROW_TILE),
        in_specs=[pl.BlockSpec((None, ROW_TILE, gw), lambda b, t: (b, t, 0)),
                  blk(0), blk(0), blk(3), blk(4), full(p["ln_g"]), full(p["ln_b"]), full(p["seg"]),
                  blk(0), blk(0), full(mod), _layer_spec(w_out, l), full(g2)] + h_specs,
        out_specs=(out, out),
        out_shape=(jax.ShapeDtypeStruct((b_, n_out, D_MODEL), F32), jax.ShapeDtypeStruct((b_, n_out, D_MODEL), BF16)),
        compiler_params=_cparams(("parallel", "parallel")),
        name="mix_out",
    )(att, yf, yb, pack, pack, p["ln_g"], p["ln_b"], p["seg"], pool, conv, mod, w_out, g2, *h_parts)


MLP_HIDDEN_TILE = 512
MLP_VMEM_LIMIT = 60 * 1024 * 1024


def _mlp_kernel(h_ref, xn_ref, mod_ref, w1_ref, w2_ref, gf_ref, o_ref, *, nb, ctx_rows, final):
    b = pl.program_id(0)
    t = pl.program_id(1)
    j = pl.program_id(2)

    @pl.when(j == 0)
    def _():
        o_ref[...] = jnp.zeros_like(o_ref)

    hid = jnp.dot(xn_ref[...], w1_ref[...], preferred_element_type=F32)
    hid = jnp.square(jnp.maximum(hid, 0.0))
    o_ref[...] += jnp.dot(hid.astype(BF16), w2_ref[...], preferred_element_type=F32)

    @pl.when(j == pl.num_programs(2) - 1)
    def _():
        def put(rows, mrow):
            o = h_ref[rows, :] + _mod_row(mod_ref, mrow, 5) * o_ref[rows, :]
            if final:
                o = _rms(o, gf_ref[...])
            o_ref[rows, :] = o

        first_row = jnp.where(t == 0, nb, b)
        if ctx_rows == 0:
            put(slice(None), b)
        elif ctx_rows == h_ref.shape[0]:
            put(slice(None), first_row)
        else:
            put(slice(0, ctx_rows), first_row)
            put(slice(ctx_rows, None), b)


def _mlp(h, xn, l, mod, w1, w2, g_final, nb, tm, ctx_rows, final):
    b_, n_, _ = h.shape
    th = MLP_HIDDEN_TILE
    tile = pl.BlockSpec((None, tm, D_MODEL), lambda b, t, j: (b, t, 0))
    return pl.pallas_call(
        functools.partial(_mlp_kernel, nb=nb, ctx_rows=ctx_rows, final=final),
        grid=(b_, n_ // tm, FF_HIDDEN // th),
        in_specs=[
            tile, tile,
            _resident((MOD_ROWS, 6 * D_MODEL), lambda b, t, j: (0, 0)),
            pl.BlockSpec((None, D_MODEL, th), lambda b, t, j: (l, 0, j)),
            pl.BlockSpec((None, th, D_MODEL), lambda b, t, j: (l, j, 0)),
            _resident((1, D_MODEL), lambda b, t, j: (0, 0)),
        ],
        out_specs=tile,
        out_shape=jax.ShapeDtypeStruct((b_, n_, D_MODEL), F32),
        compiler_params=_cparams(("parallel", "parallel", "arbitrary"), MLP_VMEM_LIMIT),
        name="mlp",
    )(h, xn, mod, w1, w2, g_final.reshape(1, D_MODEL))


def _rope_partner_perm():
    q = MLA_ROPE // 4
    i = np.arange(MLA_ROPE)
    return np.where((i // q) % 2 == 0, i + q, i - q)


def _rope_tables(n_ctx, n_seq):
    rows = n_seq // GRID_W
    row = jnp.repeat(jnp.arange(rows), GRID_W)
    col = jnp.tile(jnp.arange(GRID_W), rows)
    pos = jnp.stack([row, col], axis=-1).astype(F32)
    axis_dim = MLA_ROPE // 2
    inv_freq = ROPE_BASE ** (-jnp.arange(0, axis_dim, 2, dtype=F32) / axis_dim)
    ang = pos[:, :, None] * inv_freq
    cos, sin = jnp.cos(ang), jnp.sin(ang)
    cos64 = jnp.concatenate([cos[:, 0], cos[:, 0], cos[:, 1], cos[:, 1]], axis=-1)
    sin64 = jnp.concatenate([-sin[:, 0], sin[:, 0], -sin[:, 1], sin[:, 1]], axis=-1)
    pad = jnp.zeros((n_seq, 128 - MLA_ROPE), F32)
    cs_lat = jnp.concatenate([cos64, pad], axis=-1)
    sn_lat = jnp.concatenate([sin64, pad], axis=-1)
    cs_ctx = jnp.concatenate([jnp.ones((n_ctx, MLA_ROPE), F32), jnp.zeros((n_ctx, 128 - MLA_ROPE), F32)], axis=-1)
    return jnp.concatenate([cs_ctx, cs_lat], axis=0), jnp.concatenate([jnp.zeros((n_ctx, 128), F32), sn_lat], axis=0)


def _in_proj_weights(w_in):
    perm = _rope_partner_perm()
    kr0 = MLA_Q_RANK + MLA_KV_RANK
    return jnp.concatenate([w_in[:, :, :MLA_COLS], w_in[:, :, kr0:MLA_COLS][:, :, perm], w_in[:, :, MLA_COLS:]],
                           axis=2).astype(BF16)


def _layer_params(l, mla_w_uq, mla_w_ukv, rwkv_mu, rwkv_w0, rwkv_w2, rwkv_a0, rwkv_a2, rwkv_g2,
                  rwkv_k_k, rwkv_k_a, rwkv_r_k, rwkv_ln_g, rwkv_ln_b, seg):
    perm = _rope_partner_perm()
    wq = mla_w_uq[l].reshape(MLA_Q_RANK, MLA_HEADS, MLA_NOPE + MLA_ROPE)
    wq = jnp.concatenate([wq, wq[:, :, MLA_NOPE:][:, :, perm]], axis=-1).reshape(MLA_Q_RANK, MLA_HEADS * MLA_QK_PAD)

    wd = RWKV_WIDTH
    z = lambda r: jnp.zeros((r, wd), F32)
    w2 = jnp.concatenate([
        jnp.concatenate([rwkv_w2[l, 0], z(DECAY_RANK), z(256 - 2 * DECAY_RANK)], axis=0),
        jnp.concatenate([z(DECAY_RANK), rwkv_w2[l, 1], z(256 - 2 * DECAY_RANK)], axis=0)], axis=1)
    lead = 256 - 2 * ICL_RANK
    a2 = jnp.concatenate([
        jnp.concatenate([z(lead), rwkv_a2[l, 0], z(ICL_RANK)], axis=0),
        jnp.concatenate([z(lead), z(ICL_RANK), rwkv_a2[l, 1]], axis=0)], axis=1)
    rw = {
        "mu": rwkv_mu[l].reshape(1, RWKV_COLS),
        "w0": rwkv_w0[l].reshape(1, 2 * wd),
        "a0": rwkv_a0[l].reshape(1, 2 * wd),
        "w2": w2.astype(BF16),
        "a2": a2.astype(BF16),
        "g2": rwkv_g2[l].astype(BF16),
        "k_k": rwkv_k_k[l].reshape(1, wd),
        "k_a": rwkv_k_a[l].reshape(1, wd),
        "r_k": rwkv_r_k[l].reshape(1, wd),
        "ln_g": rwkv_ln_g[l].reshape(1, wd),
        "ln_b": rwkv_ln_b[l].reshape(1, wd),
        "seg": seg,
    }
    return wq.astype(BF16), mla_w_ukv[l].astype(BF16), rw


def kernel(x, c, ctx, c_ctx, ada_w, ada_b, norm1_g, norm2_g, w_in, mla_q_norm_g, mla_w_uq, mla_kv_norm_g, mla_w_ukv, rwkv_mu, rwkv_w0, rwkv_w2, rwkv_a0, rwkv_a2, rwkv_g2, rwkv_k_k, rwkv_k_a, rwkv_r_k, rwkv_ln_g, rwkv_ln_b, pool_w, pool_scale, conv_w, w_out, mlp_w1, mlp_w2, final_norm_g):
    nb, n_seq, d = x.shape
    n_ctx = ctx.shape[1]
    depth = ada_w.shape[0]
    assert d == D_MODEL and nb < MOD_ROWS
    assert n_ctx % ROW_TILE == 0 and n_seq % ROW_TILE == 0 and n_seq % GRID_W == 0
    nct = n_ctx // ROW_TILE

    cc = jnp.zeros((MOD_ROWS, D_MODEL), F32).at[:nb].set(c).at[nb].set(c_ctx)
    mod = _modulation(cc, ada_w, ada_b)
    cs, sn = _rope_tables(n_ctx, n_seq)
    head = np.arange(RWKV_WIDTH) // RWKV_HEAD
    seg = jnp.asarray(head[:, None] == head[None, :], dtype=BF16)

    w_in_p = _in_proj_weights(w_in)
    w_out_b, w1_b, w2_b = w_out.astype(BF16), mlp_w1.astype(BF16), mlp_w2.astype(BF16)
    h_parts = (ctx, x)
    for l in range(depth):
        last = l == depth - 1
        with_ctx = not last
        wq, wkv, rw_p = _layer_params(
            l, mla_w_uq, mla_w_ukv, rwkv_mu, rwkv_w0, rwkv_w2, rwkv_a0, rwkv_a2, rwkv_g2,
            rwkv_k_k, rwkv_k_a, rwkv_r_k, rwkv_ln_g, rwkv_ln_b, seg)

        q, k, v, pack, pool, conv = _front(
            h_parts, l, mod[l], norm1_g[l], w_in_p, cs, sn, mla_q_norm_g[l], mla_kv_norm_g[l], wq, wkv, rw_p,
            pool_w[l].astype(BF16), pool_scale[l].reshape(1, POOL_WIDTH), conv_w[l], nb, nct, n_ctx)
        att = _attention(q, k, v, n_ctx, with_ctx)
        y_f = _wkv(pack, nct, rev=False)
        y_b = _wkv(pack, nct, rev=True)
        h_mid, xn_mid = _mix_out(att, y_f, y_b, pack, pool, conv, h_parts, l, mod[l], w_out_b,
                                 norm2_g[l].reshape(1, D_MODEL), rw_p, nb, nct, with_ctx)

        n_rows = h_mid.shape[1]
        if with_ctx:
            tm = 3 * ROW_TILE if n_rows % (3 * ROW_TILE) == 0 else ROW_TILE
            ctx_rows = n_ctx
            assert n_ctx <= tm, "context longer than one MLP row tile"
        else:
            tm = 4 * ROW_TILE if n_rows % (4 * ROW_TILE) == 0 else ROW_TILE
            ctx_rows = 0
        h_parts = (_mlp(h_mid, xn_mid, l, mod[l], w1_b, w2_b, final_norm_g, nb, tm, ctx_rows, last),)
    return h_parts[0]
```

```python
import functools

import numpy as np
import jax
import jax.numpy as jnp
from jax import lax
from jax.experimental import pallas as pl
from jax.experimental.pallas import tpu as pltpu

F32 = jnp.float32
BF16 = jnp.bfloat16

D_MODEL = 2048
GRID_W = 64
NORM_EPS = 1e-6
GROUP_WIDTH = D_MODEL // 4

MLA_NOPE = 128
MLA_ROPE = 64
MLA_V = 128
MLA_HEADS = GROUP_WIDTH // MLA_V
MLA_Q_RANK = 384
MLA_KV_RANK = 128
ROPE_BASE = 10000.0
MLA_QK_PAD = 256
MLA_FEAT = 640

RWKV_HEAD = 64
RWKV_HEADS = GROUP_WIDTH // RWKV_HEAD
RWKV_WIDTH = RWKV_HEADS * RWKV_HEAD
DECAY_RANK = 96
ICL_RANK = 96
GATE_RANK = 256
RWKV_GN_EPS = 64e-5
RWKV_COLS = 3 * RWKV_WIDTH + 2 * DECAY_RANK + 2 * ICL_RANK + GATE_RANK
WKV_CHUNK = 64
RWKV_PACK_BLOCKS = 11

POOL_WINDOWS = (2, 4, 8, 16)
POOL_WIDTH = GROUP_WIDTH
POOL_GROUP = POOL_WIDTH // len(POOL_WINDOWS)
CONV_WIDTH = GROUP_WIDTH
CONV_COLS = 3 * CONV_WIDTH
FF_HIDDEN = 4 * D_MODEL
MLA_COLS = MLA_Q_RANK + MLA_KV_RANK + MLA_ROPE

ROW_TILE = 256
HALO = 8
MOD_ROWS = 16
VMEM_LIMIT = 56 * 1024 * 1024


def _cparams(sem, vmem_limit=VMEM_LIMIT):
    return pltpu.CompilerParams(dimension_semantics=sem, vmem_limit_bytes=vmem_limit)


def _resident(shape, index_map):
    return pl.BlockSpec(shape, index_map, pipeline_mode=pl.Buffered(1))


def _rms(x, g):
    return x * lax.rsqrt(jnp.mean(x * x, axis=-1, keepdims=True) + NORM_EPS) * g


def _dot(a, b):
    return jnp.dot(a.astype(BF16), b.astype(BF16), preferred_element_type=F32)


def _dot_nt(a, b):
    return lax.dot_general(a.astype(BF16), b.astype(BF16), (((1,), (1,)), ((), ())), preferred_element_type=F32)


def _split3(x):
    h1 = x.astype(BF16)
    r1 = x - h1.astype(F32)
    h2 = r1.astype(BF16)
    h3 = (r1 - h2.astype(F32)).astype(BF16)
    return h1, h2, h3


def _dot_exact_lhs(a01, x):
    return jnp.dot(jnp.concatenate([a01] * 3, axis=1), jnp.concatenate(_split3(x), axis=0), preferred_element_type=F32)


def _dot_x2(a, b):
    a1 = a.astype(BF16)
    b1 = b.astype(BF16)
    b2 = (b - b1.astype(F32)).astype(BF16)
    return jnp.dot(jnp.concatenate([a1, a1], axis=1), jnp.concatenate([b1, b2], axis=0), preferred_element_type=F32)


def _mod_kernel(c_ref, w_ref, b_ref, o_ref):
    c = c_ref[...]
    s = c * jax.nn.sigmoid(c)
    o_ref[...] = _dot(s, w_ref[...]) + b_ref[...]


def _modulation(cc, ada_w, ada_b):
    depth = ada_w.shape[0]
    tn = 1024
    return pl.pallas_call(
        _mod_kernel,
        grid=(depth, 6 * D_MODEL // tn),
        in_specs=[
            pl.BlockSpec((MOD_ROWS, D_MODEL), lambda l, j: (0, 0)),
            pl.BlockSpec((None, D_MODEL, tn), lambda l, j: (l, 0, j)),
            pl.BlockSpec((None, 1, tn), lambda l, j: (l, 0, j)),
        ],
        out_specs=pl.BlockSpec((None, MOD_ROWS, tn), lambda l, j: (l, 0, j)),
        out_shape=jax.ShapeDtypeStruct((depth, MOD_ROWS, 6 * D_MODEL), F32),
        compiler_params=_cparams(("parallel", "parallel")),
        name="modulation",
    )(cc, ada_w, ada_b.reshape(depth, 1, 6 * D_MODEL))


def _mod_row(mod_ref, row, k):
    return mod_ref[pl.ds(row, 1), k * D_MODEL:(k + 1) * D_MODEL]


def _halo_specs(width, n_rows, tile0=0):
    per = ROW_TILE // HALO
    last_tile = n_rows // ROW_TILE - 1
    last = n_rows // HALO - 1
    clamp = lambda i, hi: jnp.clip(i, 0, hi)
    cur = pl.BlockSpec((None, ROW_TILE, width), lambda b, t: (b, clamp(t - tile0, last_tile), 0))
    prev = pl.BlockSpec((None, HALO, width), lambda b, t: (b, clamp((t - tile0) * per - 1, last), 0))
    nxt = pl.BlockSpec((None, HALO, width), lambda b, t: (b, clamp((t - tile0 + 1) * per, last), 0))
    return cur, prev, nxt


def _token_tile_spec(n_rows, tile0, t_off):
    last_tile = n_rows // ROW_TILE - 1
    return pl.BlockSpec((None, ROW_TILE, D_MODEL), lambda b, t: (b, jnp.clip(t + t_off - tile0, 0, last_tile), 0))


def _seq_edges(t, nct, nt):
    has_prev = jnp.logical_and(t != 0, t != nct)
    has_next = jnp.logical_and(t != nct - 1, t != nt - 1)
    return has_prev, has_next


def _seg_sum(x, e_ref):
    h1 = x.astype(BF16)
    h2 = (x - h1.astype(F32)).astype(BF16)
    e = e_ref[...]
    return jnp.dot(h1, e, preferred_element_type=F32) + jnp.dot(h2, e, preferred_element_type=F32)


IN_SEGMENTS = (MLA_FEAT, RWKV_COLS, POOL_WIDTH, CONV_COLS)
IN_COLS_PADDED = sum(IN_SEGMENTS)
HALO_ROWS = ROW_TILE + 2 * HALO
LO, HI = HALO, HALO + ROW_TILE


def _rope(x, cs, sn):
    return x * cs + pltpu.roll(x, 64, 1) * sn


def _store_with_halo(buf, cols, res):
    buf[LO:HI, cols] = res[0:ROW_TILE]
    buf[0:LO, cols] = res[ROW_TILE:ROW_TILE + HALO]
    buf[HI:HI + HALO, cols] = res[ROW_TILE + HALO:ROW_TILE + 2 * HALO]


def _mla_qkv(f, cs, sn, gq_ref, gkv_ref, wq_ref, wkv_ref, q_ref, k_ref, v_ref):
    scale = (MLA_NOPE + MLA_ROPE) ** -0.5
    q = _dot(_rms(f[:, 0:MLA_Q_RANK], gq_ref[...]), wq_ref[...])
    kv = _dot(_rms(f[:, MLA_Q_RANK:MLA_Q_RANK + MLA_KV_RANK], gkv_ref[...]), wkv_ref[...])
    k_pe = _rope(f[:, 512:640], cs, sn).astype(BF16)
    for h in range(MLA_HEADS):
        o = h * MLA_QK_PAD
        q_ref[:, o:o + 128] = (q[:, o:o + 128] * scale).astype(BF16)
        q_ref[:, o + 128:o + 256] = (_rope(q[:, o + 128:o + 256], cs, sn) * scale).astype(BF16)
        k_ref[:, o:o + 128] = kv[:, o:o + 128].astype(BF16)
        k_ref[:, o + 128:o + 256] = k_pe
        v_ref[:, h * MLA_V:(h + 1) * MLA_V] = kv[:, o + 128:o + 256].astype(BF16)


def _rwkv_features(buf, mu_ref, w0_ref, a0_ref, w2_ref, a2_ref, g2_ref, kk_ref, ka_ref, rk_ref, e_ref, o_ref):
    w = RWKV_WIDTH

    def shifted(cols):
        f = buf[LO:HI, cols]
        return f + mu_ref[:, cols] * (0.5 * (buf[LO - 1:HI - 1, cols] + buf[LO + 1:HI + 1, cols]) - f)

    def put(i, val):
        o_ref[:, i * w:(i + 1) * w] = val

    xw = w0_ref[...] + _dot(jnp.tanh(shifted(slice(1536, 1792))), w2_ref[...])
    lw = -float(np.exp(-0.5)) * jax.nn.sigmoid(xw)
    put(5, lw[:, 0:w])
    put(8, lw[:, w:2 * w])
    yield
    a = jax.nn.sigmoid(a0_ref[...] + _dot(shifted(slice(1664, 1920)), a2_ref[...]))
    put(7, a[:, 0:w])
    put(10, a[:, w:2 * w])
    put(3, _dot(jax.nn.sigmoid(shifted(slice(1920, 2176))), g2_ref[...]))
    yield
    k = shifted(slice(w, 2 * w))
    kkv = k * kk_ref[...]
    put(2, kkv / jnp.maximum(jnp.sqrt(_seg_sum(kkv * kkv, e_ref)), 1e-12))
    yield
    r = shifted(slice(0, w))
    put(0, r)
    rk = r * rk_ref[...]
    ka = ka_ref[...]
    k_f = k * (1.0 + (a[:, 0:w] - 1.0) * ka)
    k_b = k * (1.0 + (a[:, w:2 * w] - 1.0) * ka)
    put(6, k_f)
    put(9, k_b)
    yield
    v = shifted(slice(2 * w, 3 * w))
    put(1, v)
    put(4, _seg_sum(rk * k_f + rk * k_b, e_ref) * v)


def _pool_mixer(buf, pw_ref, ps_ref, o_ref, pos, n_own):
    for gi, win in enumerate(POOL_WINDOWS):
        hw = win // 2
        cols = slice(gi * POOL_GROUP, (gi + 1) * POOL_GROUP)
        s = buf[LO - hw:HI - hw, cols]
        for o in range(-hw + 1, hw):
            s = s + buf[LO + o:HI + o, cols]
        cnt = (jnp.minimum(pos + hw, n_own) - jnp.maximum(pos - hw, 0)).astype(F32)
        z = s / cnt - buf[LO:HI, cols]
        o_ref[:, cols] = (_dot(z, pw_ref[gi]) * ps_ref[:, cols]).astype(o_ref.dtype)


def _pick_rows(refs, in_ctx):
    if len(refs) == 3:
        return [r[...] for r in refs]
    return [jnp.where(in_ctx, c[...], x[...]) for c, x in zip(refs[0:3], refs[3:6])]


def _front_kernel(*refs, nparts, nb, nct, nt, n_ctx, n_seq):
    x_refs, refs = refs[:3 * nparts], refs[3 * nparts:]
    (mod_ref, g_ref, w_ref, cs_ref, sn_ref, gq_ref, gkv_ref, wq_ref, wkv_ref,
     mu_ref, w0_ref, a0_ref, w2_ref, a2_ref, g2_ref, kk_ref, ka_ref, rk_ref, e_ref,
     pw_ref, ps_ref, cw_ref,
     q_ref, k_ref, v_ref, pack_ref, pool_ref, conv_ref,
     rw_buf, pool_buf, u_buf) = refs
    b = pl.program_id(0)
    t = pl.program_id(1)
    in_ctx = t < nct
    row = jnp.where(in_ctx, nb, b)
    has_prev, has_next = _seq_edges(t, nct, nt)
    shift, scale = _mod_row(mod_ref, row, 0), 1.0 + _mod_row(mod_ref, row, 1)
    norm = lambda x: _rms(x, g_ref[...]) * scale + shift
    x_cur, x_prev, x_next = _pick_rows(x_refs, in_ctx)
    xn = jnp.concatenate([norm(x_cur), jnp.where(has_prev, norm(x_prev), 0.0), jnp.where(has_next, norm(x_next), 0.0)],
                         axis=0).astype(BF16)

    mla0 = 0
    rw0 = IN_SEGMENTS[0]
    pool0 = rw0 + RWKV_COLS
    conv0 = pool0 + POOL_WIDTH
    cw = CONV_WIDTH
    proj = lambda rows, c0, c1: jnp.dot(rows, w_ref[:, c0:c1], preferred_element_type=F32)

    c0 = 0
    while c0 < RWKV_COLS:
        c1 = min(c0 + 512, RWKV_COLS)
        _store_with_halo(rw_buf, slice(c0, c1), proj(xn, rw0 + c0, rw0 + c1))
        c0 = c1
    stages = _rwkv_features(rw_buf, mu_ref, w0_ref, a0_ref, w2_ref, a2_ref, g2_ref, kk_ref, ka_ref, rk_ref, e_ref, pack_ref)
    next(stages)
    f_mla = proj(xn[0:ROW_TILE], mla0, rw0)
    _store_with_halo(pool_buf, slice(None), proj(xn, pool0, conv0))
    next(stages)
    gb = proj(xn[0:ROW_TILE], conv0, conv0 + cw)
    _mla_qkv(f_mla, cs_ref[...], sn_ref[...], gq_ref, gkv_ref, wq_ref, wkv_ref, q_ref, k_ref, v_ref)
    next(stages)
    gc = proj(xn, conv0 + cw, conv0 + 2 * cw)
    pos = lax.broadcasted_iota(jnp.int32, (ROW_TILE, 1), 0) + jnp.where(in_ctx, t, t - nct) * ROW_TILE
    _pool_mixer(pool_buf, pw_ref, ps_ref, pool_ref, pos, jnp.where(in_ctx, n_ctx, n_seq))
    next(stages)
    hx = proj(xn, conv0 + 2 * cw, conv0 + 3 * cw)
    for _ in stages:
        pass
    _store_with_halo(u_buf, slice(None), gc * hx)
    z = cw_ref[0:1, :] * u_buf[LO - 1:HI - 1] + cw_ref[1:2, :] * u_buf[LO:HI] + cw_ref[2:3, :] * u_buf[LO + 1:HI + 1]
    conv_ref[...] = (gb * z).astype(conv_ref.dtype)


def _layer_spec(a, l):
    return _resident((None,) + a.shape[1:], lambda *_: (l,) + (0,) * (a.ndim - 1))


def _front(h_parts, l, mod, g, w_p, cs, sn, gq, gkv, wq, wkv, p, pool_w, pool_scale, conv_w, nb, nct, n_ctx):
    b_ = h_parts[0].shape[0]
    n_ = sum(h.shape[1] for h in h_parts)
    nt = n_ // ROW_TILE
    qk_w = MLA_HEADS * MLA_QK_PAD
    pack_w = RWKV_PACK_BLOCKS * RWKV_WIDTH
    full = lambda a: _resident(a.shape, lambda b, t: (0,) * a.ndim)
    row = lambda w: pl.BlockSpec((None, ROW_TILE, w), lambda b, t: (b, t, 0))
    tab = pl.BlockSpec((ROW_TILE, 128), lambda b, t: (t, 0))
    x_specs, x_args, tile0 = [], [], 0
    for h in h_parts:
        x_specs += _halo_specs(D_MODEL, h.shape[1], tile0)
        x_args += [h, h, h]
        tile0 += h.shape[1] // ROW_TILE
    consts = (mod, g.reshape(1, D_MODEL))
    mla = (gq.reshape(1, -1), gkv.reshape(1, -1), wq, wkv)
    rwk = (p["mu"], p["w0"], p["a0"], p["w2"], p["a2"], p["g2"], p["k_k"], p["k_a"], p["r_k"], p["seg"])
    mix = (pool_w, pool_scale, conv_w)
    return pl.pallas_call(
        functools.partial(_front_kernel, nparts=len(h_parts), nb=nb, nct=nct, nt=nt, n_ctx=n_ctx, n_seq=n_ - n_ctx),
        grid=(b_, nt),
        in_specs=x_specs + [full(a) for a in consts] + [_layer_spec(w_p, l), tab, tab]
                 + [full(a) for a in mla + rwk + mix],
        out_specs=(row(qk_w), row(qk_w), row(MLA_HEADS * MLA_V), row(pack_w), row(POOL_WIDTH), row(CONV_WIDTH)),
        out_shape=(
            jax.ShapeDtypeStruct((b_, n_, qk_w), BF16),
            jax.ShapeDtypeStruct((b_, n_, qk_w), BF16),
            jax.ShapeDtypeStruct((b_, n_, MLA_HEADS * MLA_V), BF16),
            jax.ShapeDtypeStruct((b_, n_, pack_w), F32),
            jax.ShapeDtypeStruct((b_, n_, POOL_WIDTH), BF16),
            jax.ShapeDtypeStruct((b_, n_, CONV_WIDTH), BF16),
        ),
        scratch_shapes=[pltpu.VMEM((HALO_ROWS, RWKV_COLS), F32), pltpu.VMEM((HALO_ROWS, POOL_WIDTH), F32),
                        pltpu.VMEM((HALO_ROWS, CONV_WIDTH), F32)],
        compiler_params=_cparams(("parallel", "parallel")),
        name="front",
    )(*x_args, *consts, w_p, cs, sn, *mla, *rwk, *mix)


ATTN_Q_TILE = 512
ATTN_SOFTMAX_ROWS = 128


def _attn_kernel(q_ref, k_ref, v_ref, o_ref, *, n_ctx, n_all, tq, with_ctx):
    s = pl.program_id(1)

    def attend(q_rows, o_rows, nk, nq):
        for h in range(MLA_HEADS):
            q = q_ref[q_rows, h * MLA_QK_PAD:(h + 1) * MLA_QK_PAD]
            k = k_ref[0:nk, h * MLA_QK_PAD:(h + 1) * MLA_QK_PAD]
            sc = lax.dot_general(q, k, (((1,), (1,)), ((), ())), preferred_element_type=F32)
            ps, ls = [], []
            for r0 in range(0, nq, ATTN_SOFTMAX_ROWS):
                blk = sc[r0:r0 + ATTN_SOFTMAX_ROWS]
                p = jnp.exp(blk - jnp.max(blk, axis=-1, keepdims=True))
                ls.append(jnp.sum(p, axis=-1, keepdims=True))
                ps.append(p.astype(BF16))
            p = jnp.concatenate(ps, axis=0)
            l = jnp.concatenate(ls, axis=0)
            o = jnp.dot(p, v_ref[0:nk, h * MLA_V:(h + 1) * MLA_V], preferred_element_type=F32)
            o_ref[o_rows, h * MLA_V:(h + 1) * MLA_V] = (o / l).astype(o_ref.dtype)

    if with_ctx:
        @pl.when(s == 0)
        def _():
            attend(slice(0, n_ctx), slice(0, n_ctx), n_ctx, n_ctx)

        @pl.when(s > 0)
        def _():
            r0 = pl.multiple_of(n_ctx + (s - 1) * tq, ROW_TILE)
            attend(pl.ds(r0, tq), pl.ds(r0, tq), n_all, tq)
    else:
        attend(pl.ds(pl.multiple_of(n_ctx + s * tq, ROW_TILE), tq), pl.ds(pl.multiple_of(s * tq, ROW_TILE), tq), n_all, tq)


def _attention(q, k, v, n_ctx, with_ctx):
    b_, n_, qk_w = q.shape
    n_seq = n_ - n_ctx
    tq = ATTN_Q_TILE if n_seq % ATTN_Q_TILE == 0 else ROW_TILE
    n_out = n_ if with_ctx else n_seq
    whole = lambda rows, w: pl.BlockSpec((None, rows, w), lambda b, s: (b, 0, 0))
    return pl.pallas_call(
        functools.partial(_attn_kernel, n_ctx=n_ctx, n_all=n_, tq=tq, with_ctx=with_ctx),
        grid=(b_, n_seq // tq + (1 if with_ctx else 0)),
        in_specs=[whole(n_, qk_w), whole(n_, qk_w), whole(n_, MLA_HEADS * MLA_V)],
        out_specs=whole(n_out, MLA_HEADS * MLA_V),
        out_shape=jax.ShapeDtypeStruct((b_, n_out, MLA_HEADS * MLA_V), BF16),
        compiler_params=_cparams(("parallel", "arbitrary")),
        name="mla_attention",
    )(q, k, v)


def _bd(x, m0):
    return jnp.concatenate([jnp.where(m0, x, 0.0), jnp.where(m0, 0.0, x)], axis=0)


def _wkv_stages(r_ref, v_ref, kk_ref, lw_ref, kd_ref, a_ref, y_ref, s_ref, rw_s, y0_s, m_s, n_s, rev):
    c_ = WKV_CHUNK
    nsub = ROW_TILE // c_
    npair = RWKV_WIDTH // 128

    ri = lax.broadcasted_iota(jnp.int32, (c_, c_), 0)
    ci = lax.broadcasted_iota(jnp.int32, (c_, c_), 1)
    tmat = ((ri <= ci) if rev else (ri >= ci)).astype(BF16)
    rc = lax.broadcasted_iota(jnp.int32, (c_, 128), 0)
    cc = lax.broadcasted_iota(jnp.int32, (c_, 128), 1) % c_
    strict = (rc < cc) if rev else (rc > cc)
    incl = (rc <= cc) if rev else (rc >= cc)
    eye = rc == cc
    m0_128 = lax.broadcasted_iota(jnp.int32, (1, 128), 1) < c_
    m0_256 = (lax.broadcasted_iota(jnp.int32, (1, 256), 1) % 128) < c_
    zeros_cv = jnp.zeros((c_, 128), F32)

    items = [(c, p) for c in range(nsub) for p in range(npair)]
    at, rt, bt, kt, bh, kh, vv, e_tot = {}, {}, {}, {}, {}, {}, {}, {}
    for c in range(nsub):
        rows = slice(c * c_, (c + 1) * c_)
        lw = lw_ref[rows, :]
        lc = _dot_exact_lhs(tmat, lw)
        ltot = jnp.sum(lw, axis=0, keepdims=True)
        e_neg = jnp.exp(-lc)
        e_end = jnp.exp(ltot - lc)
        kk = kk_ref[rows, :]
        kd = kd_ref[rows, :]
        bv = kk * a_ref[rows, :]
        vv[c] = v_ref[rows, :]
        at[c] = -kk * jnp.exp(lc - lw)
        rt[c] = r_ref[rows, :] * jnp.exp(lc)
        bt[c], kt[c] = bv * e_neg, kd * e_neg
        bh[c], kh[c] = bv * e_end, kd * e_end
        e_tot[c] = jnp.exp(ltot)
        yield
    sl = lambda p: slice(p * 128, (p + 1) * 128)

    g = {}
    for c, p in items:
        ar = jnp.concatenate([at[c][:, sl(p)], rt[c][:, sl(p)]], axis=0)
        bd = jnp.concatenate([_bd(bt[c][:, sl(p)], m0_128), _bd(kt[c][:, sl(p)], m0_128)], axis=0)
        g[c, p] = _dot_nt(ar, bd)
    yield
    av, x, pw, a_rb = {}, {}, {}, {}
    for c, p in items:
        a_ak = jnp.where(strict, g[c, p][0:c_, 128:256], 0.0)
        a_rk = jnp.where(incl, g[c, p][c_:2 * c_, 128:256], 0.0)
        av[c, p] = _dot(jnp.concatenate([a_ak, a_rk], axis=0), _bd(vv[c][:, sl(p)], m0_128))
        pw[c, p] = jnp.where(strict, g[c, p][0:c_, 0:128], 0.0)
        a_rb[c, p] = jnp.where(incl, g[c, p][c_:2 * c_, 0:128], 0.0)
    for c, p in items:
        x[c, p] = jnp.concatenate([at[c][:, sl(p)], av[c, p][0:c_]], axis=1)
    yield
    for it in range(6):
        dot = _dot_x2 if it < 2 else _dot
        for c, p in items:
            x[c, p] = x[c, p] + dot(pw[c, p], _bd(x[c, p], m0_256))
        yield
        if it < 5:
            for c, p in items:
                pw[c, p] = _dot(pw[c, p], _bd(pw[c, p], m0_128))
            yield
    for c, p in items:
        ry = _dot(a_rb[c, p], _bd(x[c, p], m0_256))
        rw_s[c, p] = rt[c][:, sl(p)] + ry[:, 0:128]
        y0_s[c, p] = ry[:, 128:256] + av[c, p][c_:2 * c_]
    yield
    for c, p in items:
        bkt = jnp.concatenate([bh[c][:, sl(p)], kh[c][:, sl(p)]], axis=0).T
        rhs = jnp.concatenate([x[c, p], jnp.concatenate([zeros_cv, vv[c][:, sl(p)]], axis=1)], axis=0)
        z = _dot(bkt, rhs)
        m_s[c, p] = (jnp.where(m0_128, z[0:c_, 0:128], z[c_:2 * c_, 0:128])
                     + jnp.where(eye, e_tot[c][:, sl(p)], 0.0))
        n_s[c, p] = jnp.where(m0_128, z[0:c_, 128:256], z[c_:2 * c_, 128:256])
    yield

    for i in range(nsub):
        c = (nsub - 1 - i) if rev else i
        for p in range(npair):
            lhs = jnp.concatenate([rw_s[c, p], m_s[c, p]], axis=0)
            o = _dot_x2(lhs, _bd(s_ref[p], m0_128))
            y_ref[c * c_:(c + 1) * c_, sl(p)] = o[0:c_] + y0_s[c, p]
            s_ref[p] = o[c_:2 * c_] + n_s[c, p]
        yield


WKV_LEAD_STAGES = 4


def _wkv_kernel(*refs):
    ins_f, ins_b, (y_f, y_b), scr_f, scr_b = refs[0:6], refs[6:12], refs[12:14], refs[14:19], refs[19:24]

    @pl.when(pl.program_id(1) == 0)
    def _():
        scr_f[0][...] = jnp.zeros_like(scr_f[0])
        scr_b[0][...] = jnp.zeros_like(scr_b[0])

    fwd = _wkv_stages(*ins_f, y_f, *scr_f, rev=False)
    bwd = _wkv_stages(*ins_b, y_b, *scr_b, rev=True)
    for _ in range(WKV_LEAD_STAGES):
        next(fwd)
    live = [fwd, bwd]
    while live:
        for gen in list(live):
            if next(gen, StopIteration) is StopIteration:
                live.remove(gen)


def _wkv(pack, nct):
    b_, n_, _ = pack.shape
    nt = n_ // ROW_TILE
    w = RWKV_WIDTH
    nsub = ROW_TILE // WKV_CHUNK
    npair = w // 128
    fwd_tile = lambda s: s
    bwd_tile = lambda s: jnp.where(s < nct, nct - 1 - s, nt + nct - 1 - s)
    col = lambda order, j: pl.BlockSpec((None, ROW_TILE, w), lambda b, s: (b, order(s), j))
    out = lambda order: pl.BlockSpec((None, ROW_TILE, w), lambda b, s: (b, order(s), 0))
    scr = lambda: [pltpu.VMEM((npair, WKV_CHUNK, 128), F32)] + [pltpu.VMEM((nsub, npair, WKV_CHUNK, 128), F32)] * 4
    y = jax.ShapeDtypeStruct((b_, n_, w), F32)
    return pl.pallas_call(
        _wkv_kernel,
        grid=(b_, nt),
        in_specs=[col(fwd_tile, j) for j in (0, 1, 2, 5, 6, 7)] + [col(bwd_tile, j) for j in (0, 1, 2, 8, 9, 10)],
        out_specs=(out(fwd_tile), out(bwd_tile)),
        out_shape=(y, y),
        scratch_shapes=scr() + scr(),
        compiler_params=_cparams(("parallel", "arbitrary")),
        name="wkv",
    )(*([pack] * 12))


def _mix_out_kernel(att_ref, yf_ref, yb_ref, g_ref, bonus_ref, lg_ref, lb_ref, e_ref, pool_ref, conv_ref,
                    mod_ref, w_ref, g2_ref, *refs, nb, nct, t_off):
    h_refs, (o_ref, xn_ref) = refs[:-2], refs[-2:]
    b = pl.program_id(0)
    t = pl.program_id(1) + t_off
    in_ctx = t < nct
    row = jnp.where(in_ctx, nb, b)
    h = h_refs[0][...] if len(h_refs) == 1 else jnp.where(in_ctx, h_refs[0][...], h_refs[1][...])
    y = yf_ref[...] + yb_ref[...]
    mean = _seg_sum(y, e_ref) * (1.0 / RWKV_HEAD)
    d = y - mean
    var = _seg_sum(d * d, e_ref) * (1.0 / RWKV_HEAD)
    yn = d * lax.rsqrt(var + RWKV_GN_EPS) * lg_ref[...] + lb_ref[...]
    rw = ((yn + bonus_ref[...]) * g_ref[...]).astype(BF16)
    mixed = jnp.concatenate([att_ref[...], rw, pool_ref[...], conv_ref[...]], axis=1)
    acc = jnp.dot(mixed, w_ref[...], preferred_element_type=F32)
    h = h + _mod_row(mod_ref, row, 2) * acc
    o_ref[...] = h
    xn_ref[...] = (_rms(h, g2_ref[...]) * (1.0 + _mod_row(mod_ref, row, 4)) + _mod_row(mod_ref, row, 3)).astype(BF16)


def _mix_out(att, yf, yb, pack, pool, conv, h_parts, l, mod, w_out, g2, p, nb, nct, with_ctx):
    b_, n_, gw = yf.shape
    t_off = 0 if with_ctx else nct
    n_out = n_ - t_off * ROW_TILE
    blk = lambda j: pl.BlockSpec((None, ROW_TILE, gw), lambda b, t: (b, t + t_off, j))
    full = lambda a: _resident(a.shape, lambda b, t: (0,) * a.ndim)
    h_specs, tile0 = [], 0
    for h in h_parts:
        h_specs.append(_token_tile_spec(h.shape[1], tile0, t_off))
        tile0 += h.shape[1] // ROW_TILE
    out = pl.BlockSpec((None, ROW_TILE, D_MODEL), lambda b, t: (b, t, 0))
    return pl.pallas_call(
        functools.partial(_mix_out_kernel, nb=nb, nct=nct, t_off=t_off),
        grid=(b_, n_out // ROW_TILE),
        in_specs=[pl.BlockSpec((None, ROW_TILE, gw), lambda b, t: (b, t, 0)),
                  blk(0), blk(0), blk(3), blk(4), full(p["ln_g"]), full(p["ln_b"]), full(p["seg"]),
                  blk(0), blk(0), full(mod), _layer_spec(w_out, l), full(g2)] + h_specs,
        out_specs=(out, out),
        out_shape=(jax.ShapeDtypeStruct((b_, n_out, D_MODEL), F32), jax.ShapeDtypeStruct((b_, n_out, D_MODEL), BF16)),
        compiler_params=_cparams(("parallel", "parallel")),
        name="mix_out",
    )(att, yf, yb, pack, pack, p["ln_g"], p["ln_b"], p["seg"], pool, conv, mod, w_out, g2, *h_parts)


MLP_HIDDEN_TILE = 512
MLP_VMEM_LIMIT = 60 * 1024 * 1024


def _mlp_kernel(h_ref, xn_ref, mod_ref, w1_ref, w2_ref, gf_ref, o_ref, *, nb, ctx_rows, final):
    b = pl.program_id(0)
    t = pl.program_id(1)
    j = pl.program_id(2)

    @pl.when(j == 0)
    def _():
        o_ref[...] = jnp.zeros_like(o_ref)

    hid = jnp.dot(xn_ref[...], w1_ref[...], preferred_element_type=F32)
    hid = jnp.square(jnp.maximum(hid, 0.0))
    o_ref[...] += jnp.dot(hid.astype(BF16), w2_ref[...], preferred_element_type=F32)

    @pl.when(j == pl.num_programs(2) - 1)
    def _():
        def put(rows, mrow):
            o = h_ref[rows, :] + _mod_row(mod_ref, mrow, 5) * o_ref[rows, :]
            if final:
                o = _rms(o, gf_ref[...])
            o_ref[rows, :] = o

        first_row = jnp.where(t == 0, nb, b)
        if ctx_rows == 0:
            put(slice(None), b)
        elif ctx_rows == h_ref.shape[0]:
            put(slice(None), first_row)
        else:
            put(slice(0, ctx_rows), first_row)
            put(slice(ctx_rows, None), b)


def _mlp(h, xn, l, mod, w1, w2, g_final, nb, tm, ctx_rows, final):
    b_, n_, _ = h.shape
    th = MLP_HIDDEN_TILE
    tile = pl.BlockSpec((None, tm, D_MODEL), lambda b, t, j: (b, t, 0))
    return pl.pallas_call(
        functools.partial(_mlp_kernel, nb=nb, ctx_rows=ctx_rows, final=final),
        grid=(b_, n_ // tm, FF_HIDDEN // th),
        in_specs=[
            tile, tile,
            _resident((MOD_ROWS, 6 * D_MODEL), lambda b, t, j: (0, 0)),
            pl.BlockSpec((None, D_MODEL, th), lambda b, t, j: (l, 0, j)),
            pl.BlockSpec((None, th, D_MODEL), lambda b, t, j: (l, j, 0)),
            _resident((1, D_MODEL), lambda b, t, j: (0, 0)),
        ],
        out_specs=tile,
        out_shape=jax.ShapeDtypeStruct((b_, n_, D_MODEL), F32),
        compiler_params=_cparams(("parallel", "parallel", "arbitrary"), MLP_VMEM_LIMIT),
        name="mlp",
    )(h, xn, mod, w1, w2, g_final.reshape(1, D_MODEL))


def _rope_partner_perm():
    q = MLA_ROPE // 4
    i = np.arange(MLA_ROPE)
    return np.where((i // q) % 2 == 0, i + q, i - q)


def _rope_tables(n_ctx, n_seq):
    rows = n_seq // GRID_W
    row = jnp.repeat(jnp.arange(rows), GRID_W)
    col = jnp.tile(jnp.arange(GRID_W), rows)
    pos = jnp.stack([row, col], axis=-1).astype(F32)
    axis_dim = MLA_ROPE // 2
    inv_freq = ROPE_BASE ** (-jnp.arange(0, axis_dim, 2, dtype=F32) / axis_dim)
    ang = pos[:, :, None] * inv_freq
    cos, sin = jnp.cos(ang), jnp.sin(ang)
    cos64 = jnp.concatenate([cos[:, 0], cos[:, 0], cos[:, 1], cos[:, 1]], axis=-1)
    sin64 = jnp.concatenate([-sin[:, 0], sin[:, 0], -sin[:, 1], sin[:, 1]], axis=-1)
    pad = jnp.zeros((n_seq, 128 - MLA_ROPE), F32)
    cs_lat = jnp.concatenate([cos64, pad], axis=-1)
    sn_lat = jnp.concatenate([sin64, pad], axis=-1)
    cs_ctx = jnp.concatenate([jnp.ones((n_ctx, MLA_ROPE), F32), jnp.zeros((n_ctx, 128 - MLA_ROPE), F32)], axis=-1)
    return jnp.concatenate([cs_ctx, cs_lat], axis=0), jnp.concatenate([jnp.zeros((n_ctx, 128), F32), sn_lat], axis=0)


def _in_proj_weights(w_in):
    perm = _rope_partner_perm()
    kr0 = MLA_Q_RANK + MLA_KV_RANK
    return jnp.concatenate([w_in[:, :, :MLA_COLS], w_in[:, :, kr0:MLA_COLS][:, :, perm], w_in[:, :, MLA_COLS:]],
                           axis=2).astype(BF16)


def _layer_params(l, mla_w_uq, mla_w_ukv, rwkv_mu, rwkv_w0, rwkv_w2, rwkv_a0, rwkv_a2, rwkv_g2,
                  rwkv_k_k, rwkv_k_a, rwkv_r_k, rwkv_ln_g, rwkv_ln_b, seg):
    perm = _rope_partner_perm()
    wq = mla_w_uq[l].reshape(MLA_Q_RANK, MLA_HEADS, MLA_NOPE + MLA_ROPE)
    wq = jnp.concatenate([wq, wq[:, :, MLA_NOPE:][:, :, perm]], axis=-1).reshape(MLA_Q_RANK, MLA_HEADS * MLA_QK_PAD)

    wd = RWKV_WIDTH
    z = lambda r: jnp.zeros((r, wd), F32)
    w2 = jnp.concatenate([
        jnp.concatenate([rwkv_w2[l, 0], z(DECAY_RANK), z(256 - 2 * DECAY_RANK)], axis=0),
        jnp.concatenate([z(DECAY_RANK), rwkv_w2[l, 1], z(256 - 2 * DECAY_RANK)], axis=0)], axis=1)
    lead = 256 - 2 * ICL_RANK
    a2 = jnp.concatenate([
        jnp.concatenate([z(lead), rwkv_a2[l, 0], z(ICL_RANK)], axis=0),
        jnp.concatenate([z(lead), z(ICL_RANK), rwkv_a2[l, 1]], axis=0)], axis=1)
    rw = {
        "mu": rwkv_mu[l].reshape(1, RWKV_COLS),
        "w0": rwkv_w0[l].reshape(1, 2 * wd),
        "a0": rwkv_a0[l].reshape(1, 2 * wd),
        "w2": w2.astype(BF16),
        "a2": a2.astype(BF16),
        "g2": rwkv_g2[l].astype(BF16),
        "k_k": rwkv_k_k[l].reshape(1, wd),
        "k_a": rwkv_k_a[l].reshape(1, wd),
        "r_k": rwkv_r_k[l].reshape(1, wd),
        "ln_g": rwkv_ln_g[l].reshape(1, wd),
        "ln_b": rwkv_ln_b[l].reshape(1, wd),
        "seg": seg,
    }
    return wq.astype(BF16), mla_w_ukv[l].astype(BF16), rw


def kernel(x, c, ctx, c_ctx, ada_w, ada_b, norm1_g, norm2_g, w_in, mla_q_norm_g, mla_w_uq, mla_kv_norm_g, mla_w_ukv, rwkv_mu, rwkv_w0, rwkv_w2, rwkv_a0, rwkv_a2, rwkv_g2, rwkv_k_k, rwkv_k_a, rwkv_r_k, rwkv_ln_g, rwkv_ln_b, pool_w, pool_scale, conv_w, w_out, mlp_w1, mlp_w2, final_norm_g):
    nb, n_seq, d = x.shape
    n_ctx = ctx.shape[1]
    depth = ada_w.shape[0]
    assert d == D_MODEL and nb < MOD_ROWS
    assert n_ctx % ROW_TILE == 0 and n_seq % ROW_TILE == 0 and n_seq % GRID_W == 0
    nct = n_ctx // ROW_TILE

    cc = jnp.zeros((MOD_ROWS, D_MODEL), F32).at[:nb].set(c).at[nb].set(c_ctx)
    mod = _modulation(cc, ada_w, ada_b)
    cs, sn = _rope_tables(n_ctx, n_seq)
    head = np.arange(RWKV_WIDTH) // RWKV_HEAD
    seg = jnp.asarray(head[:, None] == head[None, :], dtype=BF16)

    w_in_p = _in_proj_weights(w_in)
    w_out_b, w1_b, w2_b = w_out.astype(BF16), mlp_w1.astype(BF16), mlp_w2.astype(BF16)
    h_parts = (ctx, x)
    for l in range(depth):
        last = l == depth - 1
        with_ctx = not last
        wq, wkv, rw_p = _layer_params(
            l, mla_w_uq, mla_w_ukv, rwkv_mu, rwkv_w0, rwkv_w2, rwkv_a0, rwkv_a2, rwkv_g2,
            rwkv_k_k, rwkv_k_a, rwkv_r_k, rwkv_ln_g, rwkv_ln_b, seg)

        q, k, v, pack, pool, conv = _front(
            h_parts, l, mod[l], norm1_g[l], w_in_p, cs, sn, mla_q_norm_g[l], mla_kv_norm_g[l], wq, wkv, rw_p,
            pool_w[l].astype(BF16), pool_scale[l].reshape(1, POOL_WIDTH), conv_w[l], nb, nct, n_ctx)
        att = _attention(q, k, v, n_ctx, with_ctx)
        y_f, y_b = _wkv(pack, nct)
        h_mid, xn_mid = _mix_out(att, y_f, y_b, pack, pool, conv, h_parts, l, mod[l], w_out_b,
                                 norm2_g[l].reshape(1, D_MODEL), rw_p, nb, nct, with_ctx)

        n_rows = h_mid.shape[1]
        if with_ctx:
            tm = 3 * ROW_TILE if n_rows % (3 * ROW_TILE) == 0 else ROW_TILE
            ctx_rows = n_ctx
            assert n_ctx <= tm, "context longer than one MLP row tile"
        else:
            tm = 4 * ROW_TILE if n_rows % (4 * ROW_TILE) == 0 else ROW_TILE
            ctx_rows = 0
        h_parts = (_mlp(h_mid, xn_mid, l, mod[l], w1_b, w2_b, final_norm_g, nb, tm, ctx_rows, last),)
    return h_parts[0]
```

```python
import functools

import numpy as np
import jax
import jax.numpy as jnp
from jax import lax
from jax.experimental import pallas as pl
from jax.experimental.pallas import tpu as pltpu

F32 = jnp.float32
BF16 = jnp.bfloat16

D_MODEL = 2048
GRID_W = 64
NORM_EPS = 1e-6
GROUP_WIDTH = D_MODEL // 4

MLA_NOPE = 128
MLA_ROPE = 64
MLA_V = 128
MLA_HEADS = GROUP_WIDTH // MLA_V
MLA_Q_RANK = 384
MLA_KV_RANK = 128
ROPE_BASE = 10000.0
MLA_QK_PAD = 256
MLA_FEAT = 640

RWKV_HEAD = 64
RWKV_HEADS = GROUP_WIDTH // RWKV_HEAD
RWKV_WIDTH = RWKV_HEADS * RWKV_HEAD
DECAY_RANK = 96
ICL_RANK = 96
GATE_RANK = 256
RWKV_GN_EPS = 64e-5
RWKV_COLS = 3 * RWKV_WIDTH + 2 * DECAY_RANK + 2 * ICL_RANK + GATE_RANK
WKV_CHUNK = 64
RWKV_PACK_BLOCKS = 11

POOL_WINDOWS = (2, 4, 8, 16)
POOL_WIDTH = GROUP_WIDTH
POOL_GROUP = POOL_WIDTH // len(POOL_WINDOWS)
CONV_WIDTH = GROUP_WIDTH
CONV_COLS = 3 * CONV_WIDTH
FF_HIDDEN = 4 * D_MODEL
MLA_COLS = MLA_Q_RANK + MLA_KV_RANK + MLA_ROPE

ROW_TILE = 256
HALO = 8
MOD_ROWS = 16
VMEM_LIMIT = 60 * 1024 * 1024


def _cparams(sem):
    return pltpu.CompilerParams(dimension_semantics=sem, vmem_limit_bytes=VMEM_LIMIT)


def _resident(shape, index_map):
    return pl.BlockSpec(shape, index_map, pipeline_mode=pl.Buffered(1))


def _rms(x, g):
    return x * lax.rsqrt(jnp.mean(x * x, axis=-1, keepdims=True) + NORM_EPS) * g


def _dot(a, b):
    return jnp.dot(a.astype(BF16), b.astype(BF16), preferred_element_type=F32)


def _dot_nt(a, b):
    return lax.dot_general(a.astype(BF16), b.astype(BF16), (((1,), (1,)), ((), ())), preferred_element_type=F32)


def _split3(x):
    h1 = x.astype(BF16)
    r1 = x - h1.astype(F32)
    h2 = r1.astype(BF16)
    h3 = (r1 - h2.astype(F32)).astype(BF16)
    return h1, h2, h3


def _dot_exact_lhs(a01, x):
    return jnp.dot(jnp.concatenate([a01] * 3, axis=1), jnp.concatenate(_split3(x), axis=0), preferred_element_type=F32)


def _dot_x2(a, b):
    a1 = a.astype(BF16)
    b1 = b.astype(BF16)
    b2 = (b - b1.astype(F32)).astype(BF16)
    return jnp.dot(jnp.concatenate([a1, a1], axis=1), jnp.concatenate([b1, b2], axis=0), preferred_element_type=F32)


def _mod_kernel(c_ref, w_ref, b_ref, o_ref):
    c = c_ref[...]
    s = c * jax.nn.sigmoid(c)
    o_ref[...] = _dot(s, w_ref[...]) + b_ref[...]


def _modulation(cc, ada_w, ada_b):
    depth = ada_w.shape[0]
    tn = 1024
    return pl.pallas_call(
        _mod_kernel,
        grid=(depth, 6 * D_MODEL // tn),
        in_specs=[
            pl.BlockSpec((MOD_ROWS, D_MODEL), lambda l, j: (0, 0)),
            pl.BlockSpec((None, D_MODEL, tn), lambda l, j: (l, 0, j)),
            pl.BlockSpec((None, 1, tn), lambda l, j: (l, 0, j)),
        ],
        out_specs=pl.BlockSpec((None, MOD_ROWS, tn), lambda l, j: (l, 0, j)),
        out_shape=jax.ShapeDtypeStruct((depth, MOD_ROWS, 6 * D_MODEL), F32),
        compiler_params=_cparams(("parallel", "parallel")),
        name="modulation",
    )(cc, ada_w, ada_b.reshape(depth, 1, 6 * D_MODEL))


def _mod_row(mod_ref, row, k):
    return mod_ref[pl.ds(row, 1), k * D_MODEL:(k + 1) * D_MODEL]


def _halo_specs(width, n_rows, tile0=0):
    per = ROW_TILE // HALO
    last_tile = n_rows // ROW_TILE - 1
    last = n_rows // HALO - 1
    clamp = lambda i, hi: jnp.clip(i, 0, hi)
    cur = pl.BlockSpec((None, ROW_TILE, width), lambda b, t: (b, clamp(t - tile0, last_tile), 0))
    prev = pl.BlockSpec((None, HALO, width), lambda b, t: (b, clamp((t - tile0) * per - 1, last), 0))
    nxt = pl.BlockSpec((None, HALO, width), lambda b, t: (b, clamp((t - tile0 + 1) * per, last), 0))
    return cur, prev, nxt


def _token_tile_spec(n_rows, tile0, t_off):
    last_tile = n_rows // ROW_TILE - 1
    return pl.BlockSpec((None, ROW_TILE, D_MODEL), lambda b, t: (b, jnp.clip(t + t_off - tile0, 0, last_tile), 0))


def _seq_edges(t, nct, nt):
    has_prev = jnp.logical_and(t != 0, t != nct)
    has_next = jnp.logical_and(t != nct - 1, t != nt - 1)
    return has_prev, has_next


def _seg_sum(x, e_ref):
    h1 = x.astype(BF16)
    h2 = (x - h1.astype(F32)).astype(BF16)
    e = e_ref[...]
    return jnp.dot(h1, e, preferred_element_type=F32) + jnp.dot(h2, e, preferred_element_type=F32)


IN_SEGMENTS = (MLA_FEAT, RWKV_COLS, POOL_WIDTH, CONV_COLS)
IN_COLS_PADDED = sum(IN_SEGMENTS)
HALO_ROWS = ROW_TILE + 2 * HALO
LO, HI = HALO, HALO + ROW_TILE


def _rope(x, cs, sn):
    return x * cs + pltpu.roll(x, 64, 1) * sn


def _store_with_halo(buf, cols, res):
    buf[LO:HI, cols] = res[0:ROW_TILE]
    buf[0:LO, cols] = res[ROW_TILE:ROW_TILE + HALO]
    buf[HI:HI + HALO, cols] = res[ROW_TILE + HALO:ROW_TILE + 2 * HALO]


def _mla_qkv(f, cs, sn, gq_ref, gkv_ref, wq_ref, wkv_ref, q_ref, k_ref, v_ref):
    scale = (MLA_NOPE + MLA_ROPE) ** -0.5
    q = _dot(_rms(f[:, 0:MLA_Q_RANK], gq_ref[...]), wq_ref[...])
    kv = _dot(_rms(f[:, MLA_Q_RANK:MLA_Q_RANK + MLA_KV_RANK], gkv_ref[...]), wkv_ref[...])
    k_pe = _rope(f[:, 512:640], cs, sn).astype(BF16)
    for h in range(MLA_HEADS):
        o = h * MLA_QK_PAD
        q_ref[:, o:o + 128] = (q[:, o:o + 128] * scale).astype(BF16)
        q_ref[:, o + 128:o + 256] = (_rope(q[:, o + 128:o + 256], cs, sn) * scale).astype(BF16)
        k_ref[:, o:o + 128] = kv[:, o:o + 128].astype(BF16)
        k_ref[:, o + 128:o + 256] = k_pe
        v_ref[:, h * MLA_V:(h + 1) * MLA_V] = kv[:, o + 128:o + 256].astype(BF16)


def _rwkv_features(buf, mu_ref, w0_ref, a0_ref, w2_ref, a2_ref, g2_ref, kk_ref, ka_ref, rk_ref, e_ref, o_ref):
    w = RWKV_WIDTH

    def shifted(cols):
        f = buf[LO:HI, cols]
        return f + mu_ref[:, cols] * (0.5 * (buf[LO - 1:HI - 1, cols] + buf[LO + 1:HI + 1, cols]) - f)

    def put(i, val):
        o_ref[:, i * w:(i + 1) * w] = val

    xw = w0_ref[...] + _dot(jnp.tanh(shifted(slice(1536, 1792))), w2_ref[...])
    lw = -float(np.exp(-0.5)) * jax.nn.sigmoid(xw)
    put(5, lw[:, 0:w])
    put(8, lw[:, w:2 * w])
    yield
    a = jax.nn.sigmoid(a0_ref[...] + _dot(shifted(slice(1664, 1920)), a2_ref[...]))
    put(7, a[:, 0:w])
    put(10, a[:, w:2 * w])
    put(3, _dot(jax.nn.sigmoid(shifted(slice(1920, 2176))), g2_ref[...]))
    yield
    k = shifted(slice(w, 2 * w))
    kkv = k * kk_ref[...]
    put(2, kkv / jnp.maximum(jnp.sqrt(_seg_sum(kkv * kkv, e_ref)), 1e-12))
    yield
    r = shifted(slice(0, w))
    put(0, r)
    rk = r * rk_ref[...]
    ka = ka_ref[...]
    k_f = k * (1.0 + (a[:, 0:w] - 1.0) * ka)
    k_b = k * (1.0 + (a[:, w:2 * w] - 1.0) * ka)
    put(6, k_f)
    put(9, k_b)
    yield
    v = shifted(slice(2 * w, 3 * w))
    put(1, v)
    put(4, _seg_sum(rk * k_f + rk * k_b, e_ref) * v)


def _pool_mixer(buf, pw_ref, ps_ref, o_ref, pos, n_own):
    for gi, win in enumerate(POOL_WINDOWS):
        hw = win // 2
        cols = slice(gi * POOL_GROUP, (gi + 1) * POOL_GROUP)
        s = buf[LO - hw:HI - hw, cols]
        for o in range(-hw + 1, hw):
            s = s + buf[LO + o:HI + o, cols]
        cnt = (jnp.minimum(pos + hw, n_own) - jnp.maximum(pos - hw, 0)).astype(F32)
        z = s / cnt - buf[LO:HI, cols]
        o_ref[:, cols] = (_dot(z, pw_ref[gi]) * ps_ref[:, cols]).astype(o_ref.dtype)


def _pick_rows(refs, in_ctx):
    if len(refs) == 3:
        return [r[...] for r in refs]
    return [jnp.where(in_ctx, c[...], x[...]) for c, x in zip(refs[0:3], refs[3:6])]


FRONT_FIXED_INPUTS = 22


def _front_kernel(*refs, nparts, ncast, nb, nct, nt, n_ctx, n_seq):
    x_refs, refs = refs[:3 * nparts], refs[3 * nparts:]
    (mod_ref, g_ref, w_ref, cs_ref, sn_ref, gq_ref, gkv_ref, wq_ref, wkv_ref,
     mu_ref, w0_ref, a0_ref, w2_ref, a2_ref, g2_ref, kk_ref, ka_ref, rk_ref, e_ref,
     pw_ref, ps_ref, cw_ref) = refs[:FRONT_FIXED_INPUTS]
    cast_in, refs = refs[FRONT_FIXED_INPUTS:FRONT_FIXED_INPUTS + ncast], refs[FRONT_FIXED_INPUTS + ncast:]
    q_ref, k_ref, v_ref, pack_ref, pool_ref, conv_ref = refs[:6]
    cast_out = refs[6:6 + ncast]
    rw_buf, pool_buf, u_buf = refs[6 + ncast:]
    for src, dst in zip(cast_in, cast_out):
        dst[...] = src[...].astype(dst.dtype)
    b = pl.program_id(0)
    t = pl.program_id(1)
    in_ctx = t < nct
    row = jnp.where(in_ctx, nb, b)
    has_prev, has_next = _seq_edges(t, nct, nt)
    shift, scale = _mod_row(mod_ref, row, 0), 1.0 + _mod_row(mod_ref, row, 1)
    norm = lambda x: _rms(x, g_ref[...]) * scale + shift
    x_cur, x_prev, x_next = _pick_rows(x_refs, in_ctx)
    xn = jnp.concatenate([norm(x_cur), jnp.where(has_prev, norm(x_prev), 0.0), jnp.where(has_next, norm(x_next), 0.0)],
                         axis=0).astype(BF16)

    mla0 = 0
    rw0 = IN_SEGMENTS[0]
    pool0 = rw0 + RWKV_COLS
    conv0 = pool0 + POOL_WIDTH
    cw = CONV_WIDTH
    proj = lambda rows, c0, c1: jnp.dot(rows, w_ref[:, c0:c1], preferred_element_type=F32)

    c0 = 0
    while c0 < RWKV_COLS:
        c1 = min(c0 + 512, RWKV_COLS)
        _store_with_halo(rw_buf, slice(c0, c1), proj(xn, rw0 + c0, rw0 + c1))
        c0 = c1
    stages = _rwkv_features(rw_buf, mu_ref, w0_ref, a0_ref, w2_ref, a2_ref, g2_ref, kk_ref, ka_ref, rk_ref, e_ref, pack_ref)
    next(stages)
    f_mla = proj(xn[0:ROW_TILE], mla0, rw0)
    _store_with_halo(pool_buf, slice(None), proj(xn, pool0, conv0))
    next(stages)
    gb = proj(xn[0:ROW_TILE], conv0, conv0 + cw)
    _mla_qkv(f_mla, cs_ref[...], sn_ref[...], gq_ref, gkv_ref, wq_ref, wkv_ref, q_ref, k_ref, v_ref)
    next(stages)
    gc = proj(xn, conv0 + cw, conv0 + 2 * cw)
    pos = lax.broadcasted_iota(jnp.int32, (ROW_TILE, 1), 0) + jnp.where(in_ctx, t, t - nct) * ROW_TILE
    _pool_mixer(pool_buf, pw_ref, ps_ref, pool_ref, pos, jnp.where(in_ctx, n_ctx, n_seq))
    next(stages)
    hx = proj(xn, conv0 + 2 * cw, conv0 + 3 * cw)
    for _ in stages:
        pass
    _store_with_halo(u_buf, slice(None), gc * hx)
    z = cw_ref[0:1, :] * u_buf[LO - 1:HI - 1] + cw_ref[1:2, :] * u_buf[LO:HI] + cw_ref[2:3, :] * u_buf[LO + 1:HI + 1]
    conv_ref[...] = (gb * z).astype(conv_ref.dtype)


def _layer_spec(a, l):
    return _resident((None,) + a.shape[1:], lambda *_: (l,) + (0,) * (a.ndim - 1))


def _cast_slab_spec(w, l, b_, nt):
    _, rows, cols = w.shape
    k = 1
    while k * 2 <= nt and cols % (k * 2 * 128) == 0:
        k *= 2
    assert rows % (b_ * 8) == 0
    shape = (rows // b_, cols // k)
    col = lambda t: jnp.minimum(t, k - 1)
    return (pl.BlockSpec((None,) + shape, lambda b, t: (l, b, col(t))),
            pl.BlockSpec(shape, lambda b, t: (b, col(t))),
            jax.ShapeDtypeStruct((rows, cols), BF16))


def _front(h_parts, l, mod, g, w_p, cs, sn, gq, gkv, wq, wkv, p, pool_w, pool_scale, conv_w, cast_ws, nb, nct, n_ctx):
    b_ = h_parts[0].shape[0]
    n_ = sum(h.shape[1] for h in h_parts)
    nt = n_ // ROW_TILE
    casts = [_cast_slab_spec(w, l, b_, nt) for w in cast_ws]
    qk_w = MLA_HEADS * MLA_QK_PAD
    pack_w = RWKV_PACK_BLOCKS * RWKV_WIDTH
    full = lambda a: _resident(a.shape, lambda b, t: (0,) * a.ndim)
    row = lambda w: pl.BlockSpec((None, ROW_TILE, w), lambda b, t: (b, t, 0))
    tab = pl.BlockSpec((ROW_TILE, 128), lambda b, t: (t, 0))
    x_specs, x_args, tile0 = [], [], 0
    for h in h_parts:
        x_specs += _halo_specs(D_MODEL, h.shape[1], tile0)
        x_args += [h, h, h]
        tile0 += h.shape[1] // ROW_TILE
    consts = (mod, g.reshape(1, D_MODEL))
    mla = (gq.reshape(1, -1), gkv.reshape(1, -1), wq, wkv)
    rwk = (p["mu"], p["w0"], p["a0"], p["w2"], p["a2"], p["g2"], p["k_k"], p["k_a"], p["r_k"], p["seg"])
    mix = (pool_w, pool_scale, conv_w)
    fixed = [full(a) for a in consts] + [_layer_spec(w_p, l), tab, tab] + [full(a) for a in mla + rwk + mix]
    assert len(fixed) == FRONT_FIXED_INPUTS
    outs = pl.pallas_call(
        functools.partial(_front_kernel, nparts=len(h_parts), ncast=len(casts), nb=nb, nct=nct, nt=nt,
                          n_ctx=n_ctx, n_seq=n_ - n_ctx),
        grid=(b_, nt),
        in_specs=x_specs + fixed + [c[0] for c in casts],
        out_specs=(row(qk_w), row(qk_w), row(MLA_HEADS * MLA_V), row(pack_w), row(POOL_WIDTH), row(CONV_WIDTH))
                  + tuple(c[1] for c in casts),
        out_shape=(
            jax.ShapeDtypeStruct((b_, n_, qk_w), BF16),
            jax.ShapeDtypeStruct((b_, n_, qk_w), BF16),
            jax.ShapeDtypeStruct((b_, n_, MLA_HEADS * MLA_V), BF16),
            jax.ShapeDtypeStruct((b_, n_, pack_w), F32),
            jax.ShapeDtypeStruct((b_, n_, POOL_WIDTH), BF16),
            jax.ShapeDtypeStruct((b_, n_, CONV_WIDTH), BF16),
        ) + tuple(c[2] for c in casts),
        scratch_shapes=[pltpu.VMEM((HALO_ROWS, RWKV_COLS), F32), pltpu.VMEM((HALO_ROWS, POOL_WIDTH), F32),
                        pltpu.VMEM((HALO_ROWS, CONV_WIDTH), F32)],
        compiler_params=_cparams(("parallel", "arbitrary")),
        name="front",
    )(*x_args, *consts, w_p, cs, sn, *mla, *rwk, *mix, *cast_ws)
    return outs[:6], outs[6:]


ATTN_Q_TILE = 512
ATTN_SOFTMAX_ROWS = 128


def _attn_kernel(q_ref, k_ref, v_ref, o_ref, *, n_ctx, n_all, tq, with_ctx):
    s = pl.program_id(1)

    def attend(q_rows, o_rows, nk, nq):
        for h in range(MLA_HEADS):
            q = q_ref[q_rows, h * MLA_QK_PAD:(h + 1) * MLA_QK_PAD]
            k = k_ref[0:nk, h * MLA_QK_PAD:(h + 1) * MLA_QK_PAD]
            sc = lax.dot_general(q, k, (((1,), (1,)), ((), ())), preferred_element_type=F32)
            ps, ls = [], []
            for r0 in range(0, nq, ATTN_SOFTMAX_ROWS):
                blk = sc[r0:r0 + ATTN_SOFTMAX_ROWS]
                p = jnp.exp(blk - jnp.max(blk, axis=-1, keepdims=True))
                ls.append(jnp.sum(p, axis=-1, keepdims=True))
                ps.append(p.astype(BF16))
            p = jnp.concatenate(ps, axis=0)
            l = jnp.concatenate(ls, axis=0)
            o = jnp.dot(p, v_ref[0:nk, h * MLA_V:(h + 1) * MLA_V], preferred_element_type=F32)
            o_ref[o_rows, h * MLA_V:(h + 1) * MLA_V] = (o / l).astype(o_ref.dtype)

    if with_ctx:
        @pl.when(s == 0)
        def _():
            attend(slice(0, n_ctx), slice(0, n_ctx), n_ctx, n_ctx)

        @pl.when(s > 0)
        def _():
            r0 = pl.multiple_of(n_ctx + (s - 1) * tq, ROW_TILE)
            attend(pl.ds(r0, tq), pl.ds(r0, tq), n_all, tq)
    else:
        attend(pl.ds(pl.multiple_of(n_ctx + s * tq, ROW_TILE), tq), pl.ds(pl.multiple_of(s * tq, ROW_TILE), tq), n_all, tq)


def _attention(q, k, v, n_ctx, with_ctx):
    b_, n_, qk_w = q.shape
    n_seq = n_ - n_ctx
    tq = ATTN_Q_TILE if n_seq % ATTN_Q_TILE == 0 else ROW_TILE
    n_out = n_ if with_ctx else n_seq
    whole = lambda rows, w: pl.BlockSpec((None, rows, w), lambda b, s: (b, 0, 0))
    return pl.pallas_call(
        functools.partial(_attn_kernel, n_ctx=n_ctx, n_all=n_, tq=tq, with_ctx=with_ctx),
        grid=(b_, n_seq // tq + (1 if with_ctx else 0)),
        in_specs=[whole(n_, qk_w), whole(n_, qk_w), whole(n_, MLA_HEADS * MLA_V)],
        out_specs=whole(n_out, MLA_HEADS * MLA_V),
        out_shape=jax.ShapeDtypeStruct((b_, n_out, MLA_HEADS * MLA_V), BF16),
        compiler_params=_cparams(("parallel", "arbitrary")),
        name="mla_attention",
    )(q, k, v)


def _bd(x, m0):
    return jnp.concatenate([jnp.where(m0, x, 0.0), jnp.where(m0, 0.0, x)], axis=0)


def _wkv_stages(r_ref, v_ref, kk_ref, lw_ref, kd_ref, a_ref, y_ref, s_ref, rw_s, y0_s, m_s, n_s, rev):
    c_ = WKV_CHUNK
    nsub = ROW_TILE // c_
    npair = RWKV_WIDTH // 128

    ri = lax.broadcasted_iota(jnp.int32, (c_, c_), 0)
    ci = lax.broadcasted_iota(jnp.int32, (c_, c_), 1)
    tmat = ((ri <= ci) if rev else (ri >= ci)).astype(BF16)
    rc = lax.broadcasted_iota(jnp.int32, (c_, 128), 0)
    cc = lax.broadcasted_iota(jnp.int32, (c_, 128), 1) % c_
    strict = (rc < cc) if rev else (rc > cc)
    incl = (rc <= cc) if rev else (rc >= cc)
    eye = rc == cc
    m0_128 = lax.broadcasted_iota(jnp.int32, (1, 128), 1) < c_
    m0_256 = (lax.broadcasted_iota(jnp.int32, (1, 256), 1) % 128) < c_
    zeros_cv = jnp.zeros((c_, 128), F32)

    items = [(c, p) for c in range(nsub) for p in range(npair)]
    at, rt, bt, kt, bh, kh, vv, e_tot = {}, {}, {}, {}, {}, {}, {}, {}
    for c in range(nsub):
        rows = slice(c * c_, (c + 1) * c_)
        lw = lw_ref[rows, :]
        lc = _dot_exact_lhs(tmat, lw)
        ltot = jnp.sum(lw, axis=0, keepdims=True)
        e_neg = jnp.exp(-lc)
        e_end = jnp.exp(ltot - lc)
        kk = kk_ref[rows, :]
        kd = kd_ref[rows, :]
        bv = kk * a_ref[rows, :]
        vv[c] = v_ref[rows, :]
        at[c] = -kk * jnp.exp(lc - lw)
        rt[c] = r_ref[rows, :] * jnp.exp(lc)
        bt[c], kt[c] = bv * e_neg, kd * e_neg
        bh[c], kh[c] = bv * e_end, kd * e_end
        e_tot[c] = jnp.exp(ltot)
        yield
    sl = lambda p: slice(p * 128, (p + 1) * 128)

    g = {}
    for c, p in items:
        ar = jnp.concatenate([at[c][:, sl(p)], rt[c][:, sl(p)]], axis=0)
        bd = jnp.concatenate([_bd(bt[c][:, sl(p)], m0_128), _bd(kt[c][:, sl(p)], m0_128)], axis=0)
        g[c, p] = _dot_nt(ar, bd)
    yield
    av, x, pw, a_rb = {}, {}, {}, {}
    for c, p in items:
        a_ak = jnp.where(strict, g[c, p][0:c_, 128:256], 0.0)
        a_rk = jnp.where(incl, g[c, p][c_:2 * c_, 128:256], 0.0)
        av[c, p] = _dot(jnp.concatenate([a_ak, a_rk], axis=0), _bd(vv[c][:, sl(p)], m0_128))
        pw[c, p] = jnp.where(strict, g[c, p][0:c_, 0:128], 0.0)
        a_rb[c, p] = jnp.where(incl, g[c, p][c_:2 * c_, 0:128], 0.0)
    for c, p in items:
        x[c, p] = jnp.concatenate([at[c][:, sl(p)], av[c, p][0:c_]], axis=1)
    yield
    for it in range(6):
        dot = _dot_x2 if it < 2 else _dot
        for c, p in items:
            x[c, p] = x[c, p] + dot(pw[c, p], _bd(x[c, p], m0_256))
        yield
        if it < 5:
            for c, p in items:
                pw[c, p] = _dot(pw[c, p], _bd(pw[c, p], m0_128))
            yield
    for c, p in items:
        ry = _dot(a_rb[c, p], _bd(x[c, p], m0_256))
        rw_s[c, p] = rt[c][:, sl(p)] + ry[:, 0:128]
        y0_s[c, p] = ry[:, 128:256] + av[c, p][c_:2 * c_]
    yield
    for c, p in items:
        bkt = jnp.concatenate([bh[c][:, sl(p)], kh[c][:, sl(p)]], axis=0).T
        rhs = jnp.concatenate([x[c, p], jnp.concatenate([zeros_cv, vv[c][:, sl(p)]], axis=1)], axis=0)
        z = _dot(bkt, rhs)
        m_s[c, p] = (jnp.where(m0_128, z[0:c_, 0:128], z[c_:2 * c_, 0:128])
                     + jnp.where(eye, e_tot[c][:, sl(p)], 0.0))
        n_s[c, p] = jnp.where(m0_128, z[0:c_, 128:256], z[c_:2 * c_, 128:256])
    yield

    for i in range(nsub):
        c = (nsub - 1 - i) if rev else i
        for p in range(npair):
            lhs = jnp.concatenate([rw_s[c, p], m_s[c, p]], axis=0)
            o = _dot_x2(lhs, _bd(s_ref[p], m0_128))
            y_ref[c * c_:(c + 1) * c_, sl(p)] = o[0:c_] + y0_s[c, p]
            s_ref[p] = o[c_:2 * c_] + n_s[c, p]
        yield


WKV_LEAD_STAGES = 4


def _wkv_kernel(*refs):
    ins_f, ins_b, (y_f, y_b), scr_f, scr_b = refs[0:6], refs[6:12], refs[12:14], refs[14:19], refs[19:24]

    @pl.when(pl.program_id(1) == 0)
    def _():
        scr_f[0][...] = jnp.zeros_like(scr_f[0])
        scr_b[0][...] = jnp.zeros_like(scr_b[0])

    fwd = _wkv_stages(*ins_f, y_f, *scr_f, rev=False)
    bwd = _wkv_stages(*ins_b, y_b, *scr_b, rev=True)
    for _ in range(WKV_LEAD_STAGES):
        next(fwd)
    live = [fwd, bwd]
    while live:
        for gen in list(live):
            if next(gen, StopIteration) is StopIteration:
                live.remove(gen)


def _wkv(pack, nct):
    b_, n_, _ = pack.shape
    nt = n_ // ROW_TILE
    w = RWKV_WIDTH
    nsub = ROW_TILE // WKV_CHUNK
    npair = w // 128
    fwd_tile = lambda s: s
    bwd_tile = lambda s: jnp.where(s < nct, nct - 1 - s, nt + nct - 1 - s)
    col = lambda order, j: pl.BlockSpec((None, ROW_TILE, w), lambda b, s: (b, order(s), j))
    out = lambda order: pl.BlockSpec((None, ROW_TILE, w), lambda b, s: (b, order(s), 0))
    scr = lambda: [pltpu.VMEM((npair, WKV_CHUNK, 128), F32)] + [pltpu.VMEM((nsub, npair, WKV_CHUNK, 128), F32)] * 4
    y = jax.ShapeDtypeStruct((b_, n_, w), F32)
    return pl.pallas_call(
        _wkv_kernel,
        grid=(b_, nt),
        in_specs=[col(fwd_tile, j) for j in (0, 1, 2, 5, 6, 7)] + [col(bwd_tile, j) for j in (0, 1, 2, 8, 9, 10)],
        out_specs=(out(fwd_tile), out(bwd_tile)),
        out_shape=(y, y),
        scratch_shapes=scr() + scr(),
        compiler_params=_cparams(("parallel", "arbitrary")),
        name="wkv",
    )(*([pack] * 12))


def _mix_out_kernel(att_ref, yf_ref, yb_ref, g_ref, bonus_ref, lg_ref, lb_ref, e_ref, pool_ref, conv_ref,
                    mod_ref, w_ref, g2_ref, *refs, nb, nct, t_off):
    h_refs, (o_ref, xn_ref) = refs[:-2], refs[-2:]
    b = pl.program_id(0)
    t = pl.program_id(1) + t_off
    in_ctx = t < nct
    row = jnp.where(in_ctx, nb, b)
    h = h_refs[0][...] if len(h_refs) == 1 else jnp.where(in_ctx, h_refs[0][...], h_refs[1][...])
    y = yf_ref[...] + yb_ref[...]
    mean = _seg_sum(y, e_ref) * (1.0 / RWKV_HEAD)
    d = y - mean
    var = _seg_sum(d * d, e_ref) * (1.0 / RWKV_HEAD)
    yn = d * lax.rsqrt(var + RWKV_GN_EPS) * lg_ref[...] + lb_ref[...]
    rw = ((yn + bonus_ref[...]) * g_ref[...]).astype(BF16)
    mixed = jnp.concatenate([att_ref[...], rw, pool_ref[...], conv_ref[...]], axis=1)
    acc = jnp.dot(mixed, w_ref[...], preferred_element_type=F32)
    h = h + _mod_row(mod_ref, row, 2) * acc
    o_ref[...] = h
    xn_ref[...] = (_rms(h, g2_ref[...]) * (1.0 + _mod_row(mod_ref, row, 4)) + _mod_row(mod_ref, row, 3)).astype(BF16)


def _mix_out(att, yf, yb, pack, pool, conv, h_parts, mod, w_out, g2, p, nb, nct, with_ctx):
    b_, n_, gw = yf.shape
    t_off = 0 if with_ctx else nct
    n_out = n_ - t_off * ROW_TILE
    blk = lambda j: pl.BlockSpec((None, ROW_TILE, gw), lambda b, t: (b, t + t_off, j))
    full = lambda a: _resident(a.shape, lambda b, t: (0,) * a.ndim)
    h_specs, tile0 = [], 0
    for h in h_parts:
        h_specs.append(_token_tile_spec(h.shape[1], tile0, t_off))
        tile0 += h.shape[1] // ROW_TILE
    out = pl.BlockSpec((None, ROW_TILE, D_MODEL), lambda b, t: (b, t, 0))
    return pl.pallas_call(
        functools.partial(_mix_out_kernel, nb=nb, nct=nct, t_off=t_off),
        grid=(b_, n_out // ROW_TILE),
        in_specs=[pl.BlockSpec((None, ROW_TILE, gw), lambda b, t: (b, t, 0)),
                  blk(0), blk(0), blk(3), blk(4), full(p["ln_g"]), full(p["ln_b"]), full(p["seg"]),
                  blk(0), blk(0), full(mod), full(w_out), full(g2)] + h_specs,
        out_specs=(out, out),
        out_shape=(jax.ShapeDtypeStruct((b_, n_out, D_MODEL), F32), jax.ShapeDtypeStruct((b_, n_out, D_MODEL), BF16)),
        compiler_params=_cparams(("parallel", "parallel")),
        name="mix_out",
    )(att, yf, yb, pack, pack, p["ln_g"], p["ln_b"], p["seg"], pool, conv, mod, w_out, g2, *h_parts)


MLP_HIDDEN_TILE = 512


def _mlp_kernel(h_ref, xn_ref, mod_ref, w1_ref, w2_ref, gf_ref, o_ref, *, nb, ctx_rows, final):
    b = pl.program_id(0)
    t = pl.program_id(1)
    j = pl.program_id(2)

    @pl.when(j == 0)
    def _():
        o_ref[...] = jnp.zeros_like(o_ref)

    hid = jnp.dot(xn_ref[...], w1_ref[...], preferred_element_type=F32)
    hid = jnp.square(jnp.maximum(hid, 0.0))
    o_ref[...] += jnp.dot(hid.astype(BF16), w2_ref[...], preferred_element_type=F32)

    @pl.when(j == pl.num_programs(2) - 1)
    def _():
        def put(rows, mrow):
            o = h_ref[rows, :] + _mod_row(mod_ref, mrow, 5) * o_ref[rows, :]
            if final:
                o = _rms(o, gf_ref[...])
            o_ref[rows, :] = o

        first_row = jnp.where(t == 0, nb, b)
        if ctx_rows == 0:
            put(slice(None), b)
        elif ctx_rows == h_ref.shape[0]:
            put(slice(None), first_row)
        else:
            put(slice(0, ctx_rows), first_row)
            put(slice(ctx_rows, None), b)


def _mlp(h, xn, mod, w1, w2, g_final, nb, tm, ctx_rows, final):
    b_, n_, _ = h.shape
    th = MLP_HIDDEN_TILE
    tile = pl.BlockSpec((None, tm, D_MODEL), lambda b, t, j: (b, t, 0))
    return pl.pallas_call(
        functools.partial(_mlp_kernel, nb=nb, ctx_rows=ctx_rows, final=final),
        grid=(b_, n_ // tm, FF_HIDDEN // th),
        in_specs=[
            tile, tile,
            _resident((MOD_ROWS, 6 * D_MODEL), lambda b, t, j: (0, 0)),
            pl.BlockSpec((D_MODEL, th), lambda b, t, j: (0, j)),
            pl.BlockSpec((th, D_MODEL), lambda b, t, j: (j, 0)),
            _resident((1, D_MODEL), lambda b, t, j: (0, 0)),
        ],
        out_specs=tile,
        out_shape=jax.ShapeDtypeStruct((b_, n_, D_MODEL), F32),
        compiler_params=_cparams(("parallel", "parallel", "arbitrary")),
        name="mlp",
    )(h, xn, mod, w1, w2, g_final.reshape(1, D_MODEL))


def _rope_partner_perm():
    q = MLA_ROPE // 4
    i = np.arange(MLA_ROPE)
    return np.where((i // q) % 2 == 0, i + q, i - q)


def _rope_tables(n_ctx, n_seq):
    f32 = np.float32
    rows = n_seq // GRID_W
    row = np.repeat(np.arange(rows), GRID_W)
    col = np.tile(np.arange(GRID_W), rows)
    pos = np.stack([row, col], axis=-1).astype(f32)
    axis_dim = MLA_ROPE // 2
    inv_freq = (f32(ROPE_BASE) ** (-np.arange(0, axis_dim, 2, dtype=f32) / f32(axis_dim))).astype(f32)
    ang = (pos[:, :, None] * inv_freq).astype(f32)
    cos, sin = np.cos(ang), np.sin(ang)
    cos64 = np.concatenate([cos[:, 0], cos[:, 0], cos[:, 1], cos[:, 1]], axis=-1)
    sin64 = np.concatenate([-sin[:, 0], sin[:, 0], -sin[:, 1], sin[:, 1]], axis=-1)
    pad = np.zeros((n_seq, 128 - MLA_ROPE), f32)
    cs_lat = np.concatenate([cos64, pad], axis=-1)
    sn_lat = np.concatenate([sin64, pad], axis=-1)
    cs_ctx = np.concatenate([np.ones((n_ctx, MLA_ROPE), f32), np.zeros((n_ctx, 128 - MLA_ROPE), f32)], axis=-1)
    cs = np.concatenate([cs_ctx, cs_lat], axis=0).astype(f32)
    sn = np.concatenate([np.zeros((n_ctx, 128), f32), sn_lat], axis=0).astype(f32)
    return jnp.asarray(cs), jnp.asarray(sn)


def _in_proj_weights(w_in):
    perm = _rope_partner_perm()
    kr0 = MLA_Q_RANK + MLA_KV_RANK
    return jnp.concatenate([w_in[:, :, :MLA_COLS], w_in[:, :, kr0:MLA_COLS][:, :, perm], w_in[:, :, MLA_COLS:]],
                           axis=2).astype(BF16)


def _layer_params(l, mla_w_uq, mla_w_ukv, rwkv_mu, rwkv_w0, rwkv_w2, rwkv_a0, rwkv_a2, rwkv_g2,
                  rwkv_k_k, rwkv_k_a, rwkv_r_k, rwkv_ln_g, rwkv_ln_b, seg):
    perm = _rope_partner_perm()
    wq = mla_w_uq[l].reshape(MLA_Q_RANK, MLA_HEADS, MLA_NOPE + MLA_ROPE)
    wq = jnp.concatenate([wq, wq[:, :, MLA_NOPE:][:, :, perm]], axis=-1).reshape(MLA_Q_RANK, MLA_HEADS * MLA_QK_PAD)

    wd = RWKV_WIDTH
    z = lambda r: jnp.zeros((r, wd), F32)
    w2 = jnp.concatenate([
        jnp.concatenate([rwkv_w2[l, 0], z(DECAY_RANK), z(256 - 2 * DECAY_RANK)], axis=0),
        jnp.concatenate([z(DECAY_RANK), rwkv_w2[l, 1], z(256 - 2 * DECAY_RANK)], axis=0)], axis=1)
    lead = 256 - 2 * ICL_RANK
    a2 = jnp.concatenate([
        jnp.concatenate([z(lead), rwkv_a2[l, 0], z(ICL_RANK)], axis=0),
        jnp.concatenate([z(lead), z(ICL_RANK), rwkv_a2[l, 1]], axis=0)], axis=1)
    rw = {
        "mu": rwkv_mu[l].reshape(1, RWKV_COLS),
        "w0": rwkv_w0[l].reshape(1, 2 * wd),
        "a0": rwkv_a0[l].reshape(1, 2 * wd),
        "w2": w2.astype(BF16),
        "a2": a2.astype(BF16),
        "g2": rwkv_g2[l].astype(BF16),
        "k_k": rwkv_k_k[l].reshape(1, wd),
        "k_a": rwkv_k_a[l].reshape(1, wd),
        "r_k": rwkv_r_k[l].reshape(1, wd),
        "ln_g": rwkv_ln_g[l].reshape(1, wd),
        "ln_b": rwkv_ln_b[l].reshape(1, wd),
        "seg": seg,
    }
    return wq.astype(BF16), mla_w_ukv[l].astype(BF16), rw


def kernel(x, c, ctx, c_ctx, ada_w, ada_b, norm1_g, norm2_g, w_in, mla_q_norm_g, mla_w_uq, mla_kv_norm_g, mla_w_ukv, rwkv_mu, rwkv_w0, rwkv_w2, rwkv_a0, rwkv_a2, rwkv_g2, rwkv_k_k, rwkv_k_a, rwkv_r_k, rwkv_ln_g, rwkv_ln_b, pool_w, pool_scale, conv_w, w_out, mlp_w1, mlp_w2, final_norm_g):
    nb, n_seq, d = x.shape
    n_ctx = ctx.shape[1]
    depth = ada_w.shape[0]
    assert d == D_MODEL and nb < MOD_ROWS
    assert n_ctx % ROW_TILE == 0 and n_seq % ROW_TILE == 0 and n_seq % GRID_W == 0
    nct = n_ctx // ROW_TILE

    cc = jnp.zeros((MOD_ROWS, D_MODEL), F32).at[:nb].set(c).at[nb].set(c_ctx)
    mod = _modulation(cc, ada_w, ada_b)
    cs, sn = _rope_tables(n_ctx, n_seq)
    head = np.arange(RWKV_WIDTH) // RWKV_HEAD
    seg = jnp.asarray(head[:, None] == head[None, :], dtype=BF16)

    w_in_p = _in_proj_weights(w_in)
    h_parts = (ctx, x)
    for l in range(depth):
        last = l == depth - 1
        with_ctx = not last
        wq, wkv, rw_p = _layer_params(
            l, mla_w_uq, mla_w_ukv, rwkv_mu, rwkv_w0, rwkv_w2, rwkv_a0, rwkv_a2, rwkv_g2,
            rwkv_k_k, rwkv_k_a, rwkv_r_k, rwkv_ln_g, rwkv_ln_b, seg)

        (q, k, v, pack, pool, conv), (w_out_b, w1_b, w2_b) = _front(
            h_parts, l, mod[l], norm1_g[l], w_in_p, cs, sn, mla_q_norm_g[l], mla_kv_norm_g[l], wq, wkv, rw_p,
            pool_w[l].astype(BF16), pool_scale[l].reshape(1, POOL_WIDTH), conv_w[l],
            (w_out, mlp_w1, mlp_w2), nb, nct, n_ctx)
        att = _attention(q, k, v, n_ctx, with_ctx)
        y_f, y_b = _wkv(pack, nct)
        h_mid, xn_mid = _mix_out(att, y_f, y_b, pack, pool, conv, h_parts, mod[l], w_out_b,
                                 norm2_g[l].reshape(1, D_MODEL), rw_p, nb, nct, with_ctx)

        n_rows = h_mid.shape[1]
        if with_ctx:
            tm = 3 * ROW_TILE if n_rows % (3 * ROW_TILE) == 0 else ROW_TILE
            ctx_rows = n_ctx
            assert n_ctx <= tm, "context longer than one MLP row tile"
        else:
            tm = 4 * ROW_TILE if n_rows % (4 * ROW_TILE) == 0 else ROW_TILE
            ctx_rows = 0
        h_parts = (_mlp(h_mid, xn_mid, mod[l], w1_b, w2_b, final_norm_g, nb, tm, ctx_rows, last),)
    return h_parts[0]
```

```python
import functools

import numpy as np
import jax
import jax.numpy as jnp
from jax import lax
from jax.experimental import pallas as pl
from jax.experimental.pallas import tpu as pltpu

F32 = jnp.float32
BF16 = jnp.bfloat16

D_MODEL = 2048
GRID_W = 64
NORM_EPS = 1e-6
GROUP_WIDTH = D_MODEL // 4

MLA_NOPE = 128
MLA_ROPE = 64
MLA_V = 128
MLA_HEADS = GROUP_WIDTH // MLA_V
MLA_Q_RANK = 384
MLA_KV_RANK = 128
ROPE_BASE = 10000.0
MLA_QK_PAD = 256
MLA_FEAT = 640

RWKV_HEAD = 64
RWKV_HEADS = GROUP_WIDTH // RWKV_HEAD
RWKV_WIDTH = RWKV_HEADS * RWKV_HEAD
DECAY_RANK = 96
ICL_RANK = 96
GATE_RANK = 256
RWKV_GN_EPS = 64e-5
RWKV_COLS = 3 * RWKV_WIDTH + 2 * DECAY_RANK + 2 * ICL_RANK + GATE_RANK
WKV_CHUNK = 64
RWKV_PACK_BLOCKS = 11

POOL_WINDOWS = (2, 4, 8, 16)
POOL_WIDTH = GROUP_WIDTH
POOL_GROUP = POOL_WIDTH // len(POOL_WINDOWS)
CONV_WIDTH = GROUP_WIDTH
CONV_COLS = 3 * CONV_WIDTH
FF_HIDDEN = 4 * D_MODEL
MLA_COLS = MLA_Q_RANK + MLA_KV_RANK + MLA_ROPE

ROW_TILE = 256
HALO = 8
MOD_ROWS = 16
VMEM_LIMIT = 60 * 1024 * 1024


def _cparams(sem):
    return pltpu.CompilerParams(dimension_semantics=sem, vmem_limit_bytes=VMEM_LIMIT)


def _resident(shape, index_map):
    return pl.BlockSpec(shape, index_map, pipeline_mode=pl.Buffered(1))


def _rms(x, g):
    return x * lax.rsqrt(jnp.mean(x * x, axis=-1, keepdims=True) + NORM_EPS) * g


def _dot(a, b):
    return jnp.dot(a.astype(BF16), b.astype(BF16), preferred_element_type=F32)


def _dot_nt(a, b):
    return lax.dot_general(a.astype(BF16), b.astype(BF16), (((1,), (1,)), ((), ())), preferred_element_type=F32)


def _split3(x):
    h1 = x.astype(BF16)
    r1 = x - h1.astype(F32)
    h2 = r1.astype(BF16)
    h3 = (r1 - h2.astype(F32)).astype(BF16)
    return h1, h2, h3


def _dot_exact_lhs(a01, x):
    return jnp.dot(jnp.concatenate([a01] * 3, axis=1), jnp.concatenate(_split3(x), axis=0), preferred_element_type=F32)


def _dot_x2(a, b):
    a1 = a.astype(BF16)
    b1 = b.astype(BF16)
    b2 = (b - b1.astype(F32)).astype(BF16)
    return jnp.dot(jnp.concatenate([a1, a1], axis=1), jnp.concatenate([b1, b2], axis=0), preferred_element_type=F32)


def _mod_kernel(c_ref, w_ref, b_ref, o_ref):
    c = c_ref[...]
    s = c * jax.nn.sigmoid(c)
    o_ref[...] = _dot(s, w_ref[...]) + b_ref[...]


def _modulation(cc, ada_w, ada_b):
    depth = ada_w.shape[0]
    tn = 1024
    return pl.pallas_call(
        _mod_kernel,
        grid=(depth, 6 * D_MODEL // tn),
        in_specs=[
            pl.BlockSpec((MOD_ROWS, D_MODEL), lambda l, j: (0, 0)),
            pl.BlockSpec((None, D_MODEL, tn), lambda l, j: (l, 0, j)),
            pl.BlockSpec((None, 1, tn), lambda l, j: (l, 0, j)),
        ],
        out_specs=pl.BlockSpec((None, MOD_ROWS, tn), lambda l, j: (l, 0, j)),
        out_shape=jax.ShapeDtypeStruct((depth, MOD_ROWS, 6 * D_MODEL), F32),
        compiler_params=_cparams(("parallel", "parallel")),
        name="modulation",
    )(cc, ada_w, ada_b.reshape(depth, 1, 6 * D_MODEL))


def _mod_row(mod_ref, row, k):
    return mod_ref[pl.ds(row, 1), k * D_MODEL:(k + 1) * D_MODEL]


def _halo_specs(width, n_rows, tile0=0):
    per = ROW_TILE // HALO
    last_tile = n_rows // ROW_TILE - 1
    last = n_rows // HALO - 1
    clamp = lambda i, hi: jnp.clip(i, 0, hi)
    cur = pl.BlockSpec((None, ROW_TILE, width), lambda b, t: (b, clamp(t - tile0, last_tile), 0))
    prev = pl.BlockSpec((None, HALO, width), lambda b, t: (b, clamp((t - tile0) * per - 1, last), 0))
    nxt = pl.BlockSpec((None, HALO, width), lambda b, t: (b, clamp((t - tile0 + 1) * per, last), 0))
    return cur, prev, nxt


def _token_tile_spec(n_rows, tile0, t_off):
    last_tile = n_rows // ROW_TILE - 1
    return pl.BlockSpec((None, ROW_TILE, D_MODEL), lambda b, t: (b, jnp.clip(t + t_off - tile0, 0, last_tile), 0))


def _seq_edges(t, nct, nt):
    has_prev = jnp.logical_and(t != 0, t != nct)
    has_next = jnp.logical_and(t != nct - 1, t != nt - 1)
    return has_prev, has_next


def _seg_sum(x, e_ref):
    h1 = x.astype(BF16)
    h2 = (x - h1.astype(F32)).astype(BF16)
    e = e_ref[...]
    return jnp.dot(h1, e, preferred_element_type=F32) + jnp.dot(h2, e, preferred_element_type=F32)


IN_SEGMENTS = (MLA_FEAT, RWKV_COLS, POOL_WIDTH, CONV_COLS)
IN_COLS_PADDED = sum(IN_SEGMENTS)
HALO_ROWS = ROW_TILE + 2 * HALO
LO, HI = HALO, HALO + ROW_TILE


def _rope(x, cs, sn):
    return x * cs + pltpu.roll(x, 64, 1) * sn


def _store_with_halo(buf, cols, res):
    buf[LO:HI, cols] = res[0:ROW_TILE]
    buf[0:LO, cols] = res[ROW_TILE:ROW_TILE + HALO]
    buf[HI:HI + HALO, cols] = res[ROW_TILE + HALO:ROW_TILE + 2 * HALO]


def _mla_qkv(f, cs, sn, gq_ref, gkv_ref, wq_ref, wkv_ref, q_ref, k_ref, v_ref):
    scale = (MLA_NOPE + MLA_ROPE) ** -0.5
    q = _dot(_rms(f[:, 0:MLA_Q_RANK], gq_ref[...]), wq_ref[...])
    kv = _dot(_rms(f[:, MLA_Q_RANK:MLA_Q_RANK + MLA_KV_RANK], gkv_ref[...]), wkv_ref[...])
    k_pe = _rope(f[:, 512:640], cs, sn).astype(BF16)
    for h in range(MLA_HEADS):
        o = h * MLA_QK_PAD
        q_ref[:, o:o + 128] = (q[:, o:o + 128] * scale).astype(BF16)
        q_ref[:, o + 128:o + 256] = (_rope(q[:, o + 128:o + 256], cs, sn) * scale).astype(BF16)
        k_ref[:, o:o + 128] = kv[:, o:o + 128].astype(BF16)
        k_ref[:, o + 128:o + 256] = k_pe
        v_ref[:, h * MLA_V:(h + 1) * MLA_V] = kv[:, o + 128:o + 256].astype(BF16)


def _rwkv_features(buf, mu_ref, w0_ref, a0_ref, w2_ref, a2_ref, g2_ref, kk_ref, ka_ref, rk_ref, e_ref, o_ref):
    w = RWKV_WIDTH

    def shifted(cols):
        f = buf[LO:HI, cols]
        return f + mu_ref[:, cols] * (0.5 * (buf[LO - 1:HI - 1, cols] + buf[LO + 1:HI + 1, cols]) - f)

    def put(i, val):
        o_ref[:, i * w:(i + 1) * w] = val

    xw = w0_ref[...] + _dot(jnp.tanh(shifted(slice(1536, 1792))), w2_ref[...])
    lw = -float(np.exp(-0.5)) * jax.nn.sigmoid(xw)
    put(5, lw[:, 0:w])
    put(8, lw[:, w:2 * w])
    yield
    a = jax.nn.sigmoid(a0_ref[...] + _dot(shifted(slice(1664, 1920)), a2_ref[...]))
    put(7, a[:, 0:w])
    put(10, a[:, w:2 * w])
    put(3, _dot(jax.nn.sigmoid(shifted(slice(1920, 2176))), g2_ref[...]))
    yield
    k = shifted(slice(w, 2 * w))
    kkv = k * kk_ref[...]
    put(2, kkv / jnp.maximum(jnp.sqrt(_seg_sum(kkv * kkv, e_ref)), 1e-12))
    yield
    r = shifted(slice(0, w))
    put(0, r)
    rk = r * rk_ref[...]
    ka = ka_ref[...]
    k_f = k * (1.0 + (a[:, 0:w] - 1.0) * ka)
    k_b = k * (1.0 + (a[:, w:2 * w] - 1.0) * ka)
    put(6, k_f)
    put(9, k_b)
    yield
    v = shifted(slice(2 * w, 3 * w))
    put(1, v)
    put(4, _seg_sum(rk * k_f + rk * k_b, e_ref) * v)


def _pool_mixer(buf, pw_ref, ps_ref, o_ref, pos, n_own):
    for gi, win in enumerate(POOL_WINDOWS):
        hw = win // 2
        cols = slice(gi * POOL_GROUP, (gi + 1) * POOL_GROUP)
        s = buf[LO - hw:HI - hw, cols]
        for o in range(-hw + 1, hw):
            s = s + buf[LO + o:HI + o, cols]
        cnt = (jnp.minimum(pos + hw, n_own) - jnp.maximum(pos - hw, 0)).astype(F32)
        z = s / cnt - buf[LO:HI, cols]
        o_ref[:, cols] = (_dot(z, pw_ref[gi]) * ps_ref[:, cols]).astype(o_ref.dtype)


def _pick_rows(refs, in_ctx):
    if len(refs) == 3:
        return [r[...] for r in refs]
    return [jnp.where(in_ctx, c[...], x[...]) for c, x in zip(refs[0:3], refs[3:6])]


FRONT_FIXED_INPUTS = 22


def _front_kernel(*refs, nparts, ncast, nb, nct, nt, n_ctx, n_seq):
    x_refs, refs = refs[:3 * nparts], refs[3 * nparts:]
    (mod_ref, g_ref, w_ref, cs_ref, sn_ref, gq_ref, gkv_ref, wq_ref, wkv_ref,
     mu_ref, w0_ref, a0_ref, w2_ref, a2_ref, g2_ref, kk_ref, ka_ref, rk_ref, e_ref,
     pw_ref, ps_ref, cw_ref) = refs[:FRONT_FIXED_INPUTS]
    cast_in, refs = refs[FRONT_FIXED_INPUTS:FRONT_FIXED_INPUTS + ncast], refs[FRONT_FIXED_INPUTS + ncast:]
    q_ref, k_ref, v_ref, pack_ref, pool_ref, conv_ref = refs[:6]
    cast_out = refs[6:6 + ncast]
    rw_buf, pool_buf, u_buf = refs[6 + ncast:]
    for src, dst in zip(cast_in, cast_out):
        dst[...] = src[...].astype(dst.dtype)
    b = pl.program_id(0)
    t = pl.program_id(1)
    in_ctx = t < nct
    row = jnp.where(in_ctx, nb, b)
    has_prev, has_next = _seq_edges(t, nct, nt)
    shift, scale = _mod_row(mod_ref, row, 0), 1.0 + _mod_row(mod_ref, row, 1)
    norm = lambda x: _rms(x, g_ref[...]) * scale + shift
    x_cur, x_prev, x_next = _pick_rows(x_refs, in_ctx)
    xn = jnp.concatenate([norm(x_cur), jnp.where(has_prev, norm(x_prev), 0.0), jnp.where(has_next, norm(x_next), 0.0)],
                         axis=0).astype(BF16)

    mla0 = 0
    rw0 = IN_SEGMENTS[0]
    pool0 = rw0 + RWKV_COLS
    conv0 = pool0 + POOL_WIDTH
    cw = CONV_WIDTH
    proj = lambda rows, c0, c1: jnp.dot(rows, w_ref[:, c0:c1], preferred_element_type=F32)

    c0 = 0
    while c0 < RWKV_COLS:
        c1 = min(c0 + 512, RWKV_COLS)
        _store_with_halo(rw_buf, slice(c0, c1), proj(xn, rw0 + c0, rw0 + c1))
        c0 = c1
    stages = _rwkv_features(rw_buf, mu_ref, w0_ref, a0_ref, w2_ref, a2_ref, g2_ref, kk_ref, ka_ref, rk_ref, e_ref, pack_ref)
    next(stages)
    f_mla = proj(xn[0:ROW_TILE], mla0, rw0)
    _store_with_halo(pool_buf, slice(None), proj(xn, pool0, conv0))
    next(stages)
    gb = proj(xn[0:ROW_TILE], conv0, conv0 + cw)
    _mla_qkv(f_mla, cs_ref[...], sn_ref[...], gq_ref, gkv_ref, wq_ref, wkv_ref, q_ref, k_ref, v_ref)
    next(stages)
    gc = proj(xn, conv0 + cw, conv0 + 2 * cw)
    pos = lax.broadcasted_iota(jnp.int32, (ROW_TILE, 1), 0) + jnp.where(in_ctx, t, t - nct) * ROW_TILE
    _pool_mixer(pool_buf, pw_ref, ps_ref, pool_ref, pos, jnp.where(in_ctx, n_ctx, n_seq))
    next(stages)
    hx = proj(xn, conv0 + 2 * cw, conv0 + 3 * cw)
    for _ in stages:
        pass
    _store_with_halo(u_buf, slice(None), gc * hx)
    z = cw_ref[0:1, :] * u_buf[LO - 1:HI - 1] + cw_ref[1:2, :] * u_buf[LO:HI] + cw_ref[2:3, :] * u_buf[LO + 1:HI + 1]
    conv_ref[...] = (gb * z).astype(conv_ref.dtype)


def _layer_spec(a, l):
    return _resident((None,) + a.shape[1:], lambda *_: (l,) + (0,) * (a.ndim - 1))


def _cast_slab_spec(w, l, b_, nt):
    _, rows, cols = w.shape
    k = 1
    while k * 2 <= nt and cols % (k * 2 * 128) == 0:
        k *= 2
    assert rows % (b_ * 8) == 0
    shape = (rows // b_, cols // k)
    col = lambda t: jnp.minimum(t, k - 1)
    return (pl.BlockSpec((None,) + shape, lambda b, t: (l, b, col(t))),
            pl.BlockSpec(shape, lambda b, t: (b, col(t))),
            jax.ShapeDtypeStruct((rows, cols), BF16))


def _front(h_parts, l, mod, g, w_p, cs, sn, gq, gkv, wq, wkv, p, pool_w, pool_scale, conv_w, cast_ws, nb, nct, n_ctx):
    b_ = h_parts[0].shape[0]
    n_ = sum(h.shape[1] for h in h_parts)
    nt = n_ // ROW_TILE
    casts = [_cast_slab_spec(w, l, b_, nt) for w in cast_ws]
    qk_w = MLA_HEADS * MLA_QK_PAD
    pack_w = RWKV_PACK_BLOCKS * RWKV_WIDTH
    full = lambda a: _resident(a.shape, lambda b, t: (0,) * a.ndim)
    row = lambda w: pl.BlockSpec((None, ROW_TILE, w), lambda b, t: (b, t, 0))
    tab = pl.BlockSpec((ROW_TILE, 128), lambda b, t: (t, 0))
    x_specs, x_args, tile0 = [], [], 0
    for h in h_parts:
        x_specs += _halo_specs(D_MODEL, h.shape[1], tile0)
        x_args += [h, h, h]
        tile0 += h.shape[1] // ROW_TILE
    consts = (mod, g.reshape(1, D_MODEL))
    mla = (gq.reshape(1, -1), gkv.reshape(1, -1), wq, wkv)
    rwk = (p["mu"], p["w0"], p["a0"], p["w2"], p["a2"], p["g2"], p["k_k"], p["k_a"], p["r_k"], p["seg"])
    mix = (pool_w, pool_scale, conv_w)
    fixed = [full(a) for a in consts] + [_layer_spec(w_p, l), tab, tab] + [full(a) for a in mla + rwk + mix]
    assert len(fixed) == FRONT_FIXED_INPUTS
    outs = pl.pallas_call(
        functools.partial(_front_kernel, nparts=len(h_parts), ncast=len(casts), nb=nb, nct=nct, nt=nt,
                          n_ctx=n_ctx, n_seq=n_ - n_ctx),
        grid=(b_, nt),
        in_specs=x_specs + fixed + [c[0] for c in casts],
        out_specs=(row(qk_w), row(qk_w), row(MLA_HEADS * MLA_V), row(pack_w), row(POOL_WIDTH), row(CONV_WIDTH))
                  + tuple(c[1] for c in casts),
        out_shape=(
            jax.ShapeDtypeStruct((b_, n_, qk_w), BF16),
            jax.ShapeDtypeStruct((b_, n_, qk_w), BF16),
            jax.ShapeDtypeStruct((b_, n_, MLA_HEADS * MLA_V), BF16),
            jax.ShapeDtypeStruct((b_, n_, pack_w), F32),
            jax.ShapeDtypeStruct((b_, n_, POOL_WIDTH), BF16),
            jax.ShapeDtypeStruct((b_, n_, CONV_WIDTH), BF16),
        ) + tuple(c[2] for c in casts),
        scratch_shapes=[pltpu.VMEM((HALO_ROWS, RWKV_COLS), F32), pltpu.VMEM((HALO_ROWS, POOL_WIDTH), F32),
                        pltpu.VMEM((HALO_ROWS, CONV_WIDTH), F32)],
        compiler_params=_cparams(("parallel", "arbitrary")),
        name="front",
    )(*x_args, *consts, w_p, cs, sn, *mla, *rwk, *mix, *cast_ws)
    return outs[:6], outs[6:]


ATTN_Q_TILE = 512
ATTN_SOFTMAX_ROWS = 128


def _attn_kernel(q_ref, k_ref, v_ref, o_ref, *, n_ctx, n_all, tq, with_ctx):
    s = pl.program_id(1)

    def attend(q_rows, o_rows, nk, nq):
        for h in range(MLA_HEADS):
            q = q_ref[q_rows, h * MLA_QK_PAD:(h + 1) * MLA_QK_PAD]
            k = k_ref[0:nk, h * MLA_QK_PAD:(h + 1) * MLA_QK_PAD]
            sc = lax.dot_general(q, k, (((1,), (1,)), ((), ())), preferred_element_type=F32)
            ps, ls = [], []
            for r0 in range(0, nq, ATTN_SOFTMAX_ROWS):
                blk = sc[r0:r0 + ATTN_SOFTMAX_ROWS]
                p = jnp.exp(blk - jnp.max(blk, axis=-1, keepdims=True))
                ls.append(jnp.sum(p, axis=-1, keepdims=True))
                ps.append(p.astype(BF16))
            p = jnp.concatenate(ps, axis=0)
            l = jnp.concatenate(ls, axis=0)
            o = jnp.dot(p, v_ref[0:nk, h * MLA_V:(h + 1) * MLA_V], preferred_element_type=F32)
            o_ref[o_rows, h * MLA_V:(h + 1) * MLA_V] = (o / l).astype(o_ref.dtype)

    if with_ctx:
        @pl.when(s == 0)
        def _():
            attend(slice(0, n_ctx), slice(0, n_ctx), n_ctx, n_ctx)

        @pl.when(s > 0)
        def _():
            r0 = pl.multiple_of(n_ctx + (s - 1) * tq, ROW_TILE)
            attend(pl.ds(r0, tq), pl.ds(r0, tq), n_all, tq)
    else:
        attend(pl.ds(pl.multiple_of(n_ctx + s * tq, ROW_TILE), tq), pl.ds(pl.multiple_of(s * tq, ROW_TILE), tq), n_all, tq)


def _attention(q, k, v, n_ctx, with_ctx):
    b_, n_, qk_w = q.shape
    n_seq = n_ - n_ctx
    tq = ATTN_Q_TILE if n_seq % ATTN_Q_TILE == 0 else ROW_TILE
    n_out = n_ if with_ctx else n_seq
    whole = lambda rows, w: pl.BlockSpec((None, rows, w), lambda b, s: (b, 0, 0))
    return pl.pallas_call(
        functools.partial(_attn_kernel, n_ctx=n_ctx, n_all=n_, tq=tq, with_ctx=with_ctx),
        grid=(b_, n_seq // tq + (1 if with_ctx else 0)),
        in_specs=[whole(n_, qk_w), whole(n_, qk_w), whole(n_, MLA_HEADS * MLA_V)],
        out_specs=whole(n_out, MLA_HEADS * MLA_V),
        out_shape=jax.ShapeDtypeStruct((b_, n_out, MLA_HEADS * MLA_V), BF16),
        compiler_params=_cparams(("parallel", "arbitrary")),
        name="mla_attention",
    )(q, k, v)


def _bd(x, m0):
    return jnp.concatenate([jnp.where(m0, x, 0.0), jnp.where(m0, 0.0, x)], axis=0)


def _wkv_stages(r_ref, v_ref, kk_ref, lw_ref, kd_ref, a_ref, y_ref, s_ref, rw_s, y0_s, m_s, n_s, rev):
    c_ = WKV_CHUNK
    nsub = ROW_TILE // c_
    npair = RWKV_WIDTH // 128

    ri = lax.broadcasted_iota(jnp.int32, (c_, c_), 0)
    ci = lax.broadcasted_iota(jnp.int32, (c_, c_), 1)
    tmat = ((ri <= ci) if rev else (ri >= ci)).astype(BF16)
    rc = lax.broadcasted_iota(jnp.int32, (c_, 128), 0)
    cc = lax.broadcasted_iota(jnp.int32, (c_, 128), 1) % c_
    strict = (rc < cc) if rev else (rc > cc)
    incl = (rc <= cc) if rev else (rc >= cc)
    eye = rc == cc
    m0_128 = lax.broadcasted_iota(jnp.int32, (1, 128), 1) < c_
    m0_256 = (lax.broadcasted_iota(jnp.int32, (1, 256), 1) % 128) < c_
    zeros_cv = jnp.zeros((c_, 128), F32)

    items = [(c, p) for c in range(nsub) for p in range(npair)]
    at, rt, bt, kt, bh, kh, vv, e_tot = {}, {}, {}, {}, {}, {}, {}, {}
    for c in range(nsub):
        rows = slice(c * c_, (c + 1) * c_)
        lw = lw_ref[rows, :]
        lc = _dot_exact_lhs(tmat, lw)
        ltot = jnp.sum(lw, axis=0, keepdims=True)
        e_neg = jnp.exp(-lc)
        e_end = jnp.exp(ltot - lc)
        kk = kk_ref[rows, :]
        kd = kd_ref[rows, :]
        bv = kk * a_ref[rows, :]
        vv[c] = v_ref[rows, :]
        at[c] = -kk * jnp.exp(lc - lw)
        rt[c] = r_ref[rows, :] * jnp.exp(lc)
        bt[c], kt[c] = bv * e_neg, kd * e_neg
        bh[c], kh[c] = bv * e_end, kd * e_end
        e_tot[c] = jnp.exp(ltot)
        yield
    sl = lambda p: slice(p * 128, (p + 1) * 128)

    g = {}
    for c, p in items:
        ar = jnp.concatenate([at[c][:, sl(p)], rt[c][:, sl(p)]], axis=0)
        bd = jnp.concatenate([_bd(bt[c][:, sl(p)], m0_128), _bd(kt[c][:, sl(p)], m0_128)], axis=0)
        g[c, p] = _dot_nt(ar, bd)
    yield
    av, x, pw, a_rb = {}, {}, {}, {}
    for c, p in items:
        a_ak = jnp.where(strict, g[c, p][0:c_, 128:256], 0.0)
        a_rk = jnp.where(incl, g[c, p][c_:2 * c_, 128:256], 0.0)
        av[c, p] = _dot(jnp.concatenate([a_ak, a_rk], axis=0), _bd(vv[c][:, sl(p)], m0_128))
        pw[c, p] = jnp.where(strict, g[c, p][0:c_, 0:128], 0.0)
        a_rb[c, p] = jnp.where(incl, g[c, p][c_:2 * c_, 0:128], 0.0)
    for c, p in items:
        x[c, p] = jnp.concatenate([at[c][:, sl(p)], av[c, p][0:c_]], axis=1)
    yield
    for it in range(6):
        dot = _dot_x2 if it < 2 else _dot
        for c, p in items:
            x[c, p] = x[c, p] + dot(pw[c, p], _bd(x[c, p], m0_256))
        yield
        if it < 5:
            for c, p in items:
                pw[c, p] = _dot(pw[c, p], _bd(pw[c, p], m0_128))
            yield
    for c, p in items:
        ry = _dot(a_rb[c, p], _bd(x[c, p], m0_256))
        rw_s[c, p] = rt[c][:, sl(p)] + ry[:, 0:128]
        y0_s[c, p] = ry[:, 128:256] + av[c, p][c_:2 * c_]
    yield
    for c, p in items:
        bkt = jnp.concatenate([bh[c][:, sl(p)], kh[c][:, sl(p)]], axis=0).T
        rhs = jnp.concatenate([x[c, p], jnp.concatenate([zeros_cv, vv[c][:, sl(p)]], axis=1)], axis=0)
        z = _dot(bkt, rhs)
        m_s[c, p] = (jnp.where(m0_128, z[0:c_, 0:128], z[c_:2 * c_, 0:128])
                     + jnp.where(eye, e_tot[c][:, sl(p)], 0.0))
        n_s[c, p] = jnp.where(m0_128, z[0:c_, 128:256], z[c_:2 * c_, 128:256])
    yield

    for i in range(nsub):
        c = (nsub - 1 - i) if rev else i
        for p in range(npair):
            lhs = jnp.concatenate([rw_s[c, p], m_s[c, p]], axis=0)
            o = _dot_x2(lhs, _bd(s_ref[p], m0_128))
            y_ref[c * c_:(c + 1) * c_, sl(p)] = o[0:c_] + y0_s[c, p]
            s_ref[p] = o[c_:2 * c_] + n_s[c, p]
        yield


WKV_LEAD_STAGES = 4


def _wkv_kernel(*refs):
    ins_f, ins_b, (y_f, y_b), scr_f, scr_b = refs[0:6], refs[6:12], refs[12:14], refs[14:19], refs[19:24]

    @pl.when(pl.program_id(1) == 0)
    def _():
        scr_f[0][...] = jnp.zeros_like(scr_f[0])
        scr_b[0][...] = jnp.zeros_like(scr_b[0])

    fwd = _wkv_stages(*ins_f, y_f, *scr_f, rev=False)
    bwd = _wkv_stages(*ins_b, y_b, *scr_b, rev=True)
    for _ in range(WKV_LEAD_STAGES):
        next(fwd)
    live = [fwd, bwd]
    while live:
        for gen in list(live):
            if next(gen, StopIteration) is StopIteration:
                live.remove(gen)


def _wkv(pack, nct):
    b_, n_, _ = pack.shape
    nt = n_ // ROW_TILE
    w = RWKV_WIDTH
    nsub = ROW_TILE // WKV_CHUNK
    npair = w // 128
    fwd_tile = lambda s: s
    bwd_tile = lambda s: jnp.where(s < nct, nct - 1 - s, nt + nct - 1 - s)
    col = lambda order, j: pl.BlockSpec((None, ROW_TILE, w), lambda b, s: (b, order(s), j))
    out = lambda order: pl.BlockSpec((None, ROW_TILE, w), lambda b, s: (b, order(s), 0))
    scr = lambda: [pltpu.VMEM((npair, WKV_CHUNK, 128), F32)] + [pltpu.VMEM((nsub, npair, WKV_CHUNK, 128), F32)] * 4
    y = jax.ShapeDtypeStruct((b_, n_, w), F32)
    return pl.pallas_call(
        _wkv_kernel,
        grid=(b_, nt),
        in_specs=[col(fwd_tile, j) for j in (0, 1, 2, 5, 6, 7)] + [col(bwd_tile, j) for j in (0, 1, 2, 8, 9, 10)],
        out_specs=(out(fwd_tile), out(bwd_tile)),
        out_shape=(y, y),
        scratch_shapes=scr() + scr(),
        compiler_params=_cparams(("parallel", "arbitrary")),
        name="wkv",
    )(*([pack] * 12))


def _mix_out_kernel(att_ref, yf_ref, yb_ref, g_ref, bonus_ref, lg_ref, lb_ref, e_ref, pool_ref, conv_ref,
                    mod_ref, w_ref, g2_ref, *refs, nb, nct, t_off):
    h_refs, (o_ref, xn_ref) = refs[:-2], refs[-2:]
    b = pl.program_id(0)
    t = pl.program_id(1) + t_off
    in_ctx = t < nct
    row = jnp.where(in_ctx, nb, b)
    h = h_refs[0][...] if len(h_refs) == 1 else jnp.where(in_ctx, h_refs[0][...], h_refs[1][...])
    y = yf_ref[...] + yb_ref[...]
    mean = _seg_sum(y, e_ref) * (1.0 / RWKV_HEAD)
    d = y - mean
    var = _seg_sum(d * d, e_ref) * (1.0 / RWKV_HEAD)
    yn = d * lax.rsqrt(var + RWKV_GN_EPS) * lg_ref[...] + lb_ref[...]
    rw = ((yn + bonus_ref[...]) * g_ref[...]).astype(BF16)
    mixed = jnp.concatenate([att_ref[...], rw, pool_ref[...], conv_ref[...]], axis=1)
    acc = jnp.dot(mixed, w_ref[...], preferred_element_type=F32)
    h = h + _mod_row(mod_ref, row, 2) * acc
    o_ref[...] = h
    xn_ref[...] = (_rms(h, g2_ref[...]) * (1.0 + _mod_row(mod_ref, row, 4)) + _mod_row(mod_ref, row, 3)).astype(BF16)


def _mix_out(att, yf, yb, pack, pool, conv, h_parts, mod, w_out, g2, p, nb, nct, with_ctx):
    b_, n_, gw = yf.shape
    t_off = 0 if with_ctx else nct
    n_out = n_ - t_off * ROW_TILE
    blk = lambda j: pl.BlockSpec((None, ROW_TILE, gw), lambda b, t: (b, t + t_off, j))
    full = lambda a: _resident(a.shape, lambda b, t: (0,) * a.ndim)
    h_specs, tile0 = [], 0
    for h in h_parts:
        h_specs.append(_token_tile_spec(h.shape[1], tile0, t_off))
        tile0 += h.shape[1] // ROW_TILE
    out = pl.BlockSpec((None, ROW_TILE, D_MODEL), lambda b, t: (b, t, 0))
    return pl.pallas_call(
        functools.partial(_mix_out_kernel, nb=nb, nct=nct, t_off=t_off),
        grid=(b_, n_out // ROW_TILE),
        in_specs=[pl.BlockSpec((None, ROW_TILE, gw), lambda b, t: (b, t, 0)),
                  blk(0), blk(0), blk(3), blk(4), full(p["ln_g"]), full(p["ln_b"]), full(p["seg"]),
                  blk(0), blk(0), full(mod), full(w_out), full(g2)] + h_specs,
        out_specs=(out, out),
        out_shape=(jax.ShapeDtypeStruct((b_, n_out, D_MODEL), F32), jax.ShapeDtypeStruct((b_, n_out, D_MODEL), BF16)),
        compiler_params=_cparams(("parallel", "parallel")),
        name="mix_out",
    )(att, yf, yb, pack, pack, p["ln_g"], p["ln_b"], p["seg"], pool, conv, mod, w_out, g2, *h_parts)


MLP_HIDDEN_TILE = 512


def _mlp_kernel(h_ref, xn_ref, mod_ref, w1_ref, w2_ref, gf_ref, o_ref, *, nb, ctx_rows, final):
    b = pl.program_id(0)
    t = pl.program_id(1)
    j = pl.program_id(2)

    def hidden_slice(first):
        hid = jnp.dot(xn_ref[...], w1_ref[...], preferred_element_type=F32)
        hid = jnp.square(jnp.maximum(hid, 0.0))
        part = jnp.dot(hid.astype(BF16), w2_ref[...], preferred_element_type=F32)
        o_ref[...] = part if first else o_ref[...] + part

    @pl.when(j == 0)
    def _():
        hidden_slice(True)

    @pl.when(j > 0)
    def _():
        hidden_slice(False)

    @pl.when(j == pl.num_programs(2) - 1)
    def _():
        def put(rows, mrow):
            o = h_ref[rows, :] + _mod_row(mod_ref, mrow, 5) * o_ref[rows, :]
            if final:
                o = _rms(o, gf_ref[...])
            o_ref[rows, :] = o

        first_row = jnp.where(t == 0, nb, b)
        if ctx_rows == 0:
            put(slice(None), b)
        elif ctx_rows == h_ref.shape[0]:
            put(slice(None), first_row)
        else:
            put(slice(0, ctx_rows), first_row)
            put(slice(ctx_rows, None), b)


def _mlp(h, xn, mod, w1, w2, g_final, nb, tm, ctx_rows, final):
    b_, n_, _ = h.shape
    th = MLP_HIDDEN_TILE * (2 if tm < 1024 else 1)
    tile = pl.BlockSpec((None, tm, D_MODEL), lambda b, t, j: (b, t, 0))
    return pl.pallas_call(
        functools.partial(_mlp_kernel, nb=nb, ctx_rows=ctx_rows, final=final),
        grid=(b_, n_ // tm, FF_HIDDEN // th),
        in_specs=[
            tile, tile,
            _resident((MOD_ROWS, 6 * D_MODEL), lambda b, t, j: (0, 0)),
            pl.BlockSpec((D_MODEL, th), lambda b, t, j: (0, j)),
            pl.BlockSpec((th, D_MODEL), lambda b, t, j: (j, 0)),
            _resident((1, D_MODEL), lambda b, t, j: (0, 0)),
        ],
        out_specs=tile,
        out_shape=jax.ShapeDtypeStruct((b_, n_, D_MODEL), F32),
        compiler_params=_cparams(("parallel", "parallel", "arbitrary")),
        name="mlp",
    )(h, xn, mod, w1, w2, g_final.reshape(1, D_MODEL))


def _rope_partner_perm():
    q = MLA_ROPE // 4
    i = np.arange(MLA_ROPE)
    return np.where((i // q) % 2 == 0, i + q, i - q)


def _rope_tables(n_ctx, n_seq):
    f32 = np.float32
    rows = n_seq // GRID_W
    row = np.repeat(np.arange(rows), GRID_W)
    col = np.tile(np.arange(GRID_W), rows)
    pos = np.stack([row, col], axis=-1).astype(f32)
    axis_dim = MLA_ROPE // 2
    inv_freq = (f32(ROPE_BASE) ** (-np.arange(0, axis_dim, 2, dtype=f32) / f32(axis_dim))).astype(f32)
    ang = (pos[:, :, None] * inv_freq).astype(f32)
    cos, sin = np.cos(ang), np.sin(ang)
    cos64 = np.concatenate([cos[:, 0], cos[:, 0], cos[:, 1], cos[:, 1]], axis=-1)
    sin64 = np.concatenate([-sin[:, 0], sin[:, 0], -sin[:, 1], sin[:, 1]], axis=-1)
    pad = np.zeros((n_seq, 128 - MLA_ROPE), f32)
    cs_lat = np.concatenate([cos64, pad], axis=-1)
    sn_lat = np.concatenate([sin64, pad], axis=-1)
    cs_ctx = np.concatenate([np.ones((n_ctx, MLA_ROPE), f32), np.zeros((n_ctx, 128 - MLA_ROPE), f32)], axis=-1)
    cs = np.concatenate([cs_ctx, cs_lat], axis=0).astype(f32)
    sn = np.concatenate([np.zeros((n_ctx, 128), f32), sn_lat], axis=0).astype(f32)
    return jnp.asarray(cs), jnp.asarray(sn)


def _in_proj_weights(w_in):
    perm = _rope_partner_perm()
    kr0 = MLA_Q_RANK + MLA_KV_RANK
    return jnp.concatenate([w_in[:, :, :MLA_COLS], w_in[:, :, kr0:MLA_COLS][:, :, perm], w_in[:, :, MLA_COLS:]],
                           axis=2).astype(BF16)


def _layer_params(l, mla_w_uq, mla_w_ukv, rwkv_mu, rwkv_w0, rwkv_w2, rwkv_a0, rwkv_a2, rwkv_g2,
                  rwkv_k_k, rwkv_k_a, rwkv_r_k, rwkv_ln_g, rwkv_ln_b, seg):
    perm = _rope_partner_perm()
    wq = mla_w_uq[l].reshape(MLA_Q_RANK, MLA_HEADS, MLA_NOPE + MLA_ROPE)
    wq = jnp.concatenate([wq, wq[:, :, MLA_NOPE:][:, :, perm]], axis=-1).reshape(MLA_Q_RANK, MLA_HEADS * MLA_QK_PAD)

    wd = RWKV_WIDTH
    z = lambda r: jnp.zeros((r, wd), F32)
    w2 = jnp.concatenate([
        jnp.concatenate([rwkv_w2[l, 0], z(DECAY_RANK), z(256 - 2 * DECAY_RANK)], axis=0),
        jnp.concatenate([z(DECAY_RANK), rwkv_w2[l, 1], z(256 - 2 * DECAY_RANK)], axis=0)], axis=1)
    lead = 256 - 2 * ICL_RANK
    a2 = jnp.concatenate([
        jnp.concatenate([z(lead), rwkv_a2[l, 0], z(ICL_RANK)], axis=0),
        jnp.concatenate([z(lead), z(ICL_RANK), rwkv_a2[l, 1]], axis=0)], axis=1)
    rw = {
        "mu": rwkv_mu[l].reshape(1, RWKV_COLS),
        "w0": rwkv_w0[l].reshape(1, 2 * wd),
        "a0": rwkv_a0[l].reshape(1, 2 * wd),
        "w2": w2.astype(BF16),
        "a2": a2.astype(BF16),
        "g2": rwkv_g2[l].astype(BF16),
        "k_k": rwkv_k_k[l].reshape(1, wd),
        "k_a": rwkv_k_a[l].reshape(1, wd),
        "r_k": rwkv_r_k[l].reshape(1, wd),
        "ln_g": rwkv_ln_g[l].reshape(1, wd),
        "ln_b": rwkv_ln_b[l].reshape(1, wd),
        "seg": seg,
    }
    return wq.astype(BF16), mla_w_ukv[l].astype(BF16), rw


def kernel(x, c, ctx, c_ctx, ada_w, ada_b, norm1_g, norm2_g, w_in, mla_q_norm_g, mla_w_uq, mla_kv_norm_g, mla_w_ukv, rwkv_mu, rwkv_w0, rwkv_w2, rwkv_a0, rwkv_a2, rwkv_g2, rwkv_k_k, rwkv_k_a, rwkv_r_k, rwkv_ln_g, rwkv_ln_b, pool_w, pool_scale, conv_w, w_out, mlp_w1, mlp_w2, final_norm_g):
    nb, n_seq, d = x.shape
    n_ctx = ctx.shape[1]
    depth = ada_w.shape[0]
    assert d == D_MODEL and nb < MOD_ROWS
    assert n_ctx % ROW_TILE == 0 and n_seq % ROW_TILE == 0 and n_seq % GRID_W == 0
    nct = n_ctx // ROW_TILE

    cc = jnp.zeros((MOD_ROWS, D_MODEL), F32).at[:nb].set(c).at[nb].set(c_ctx)
    mod = _modulation(cc, ada_w, ada_b)
    cs, sn = _rope_tables(n_ctx, n_seq)
    head = np.arange(RWKV_WIDTH) // RWKV_HEAD
    seg = jnp.asarray(head[:, None] == head[None, :], dtype=BF16)

    w_in_p = _in_proj_weights(w_in)
    h_parts = (ctx, x)
    for l in range(depth):
        last = l == depth - 1
        with_ctx = not last
        wq, wkv, rw_p = _layer_params(
            l, mla_w_uq, mla_w_ukv, rwkv_mu, rwkv_w0, rwkv_w2, rwkv_a0, rwkv_a2, rwkv_g2,
            rwkv_k_k, rwkv_k_a, rwkv_r_k, rwkv_ln_g, rwkv_ln_b, seg)

        (q, k, v, pack, pool, conv), (w_out_b, w1_b, w2_b) = _front(
            h_parts, l, mod[l], norm1_g[l], w_in_p, cs, sn, mla_q_norm_g[l], mla_kv_norm_g[l], wq, wkv, rw_p,
            pool_w[l].astype(BF16), pool_scale[l].reshape(1, POOL_WIDTH), conv_w[l],
            (w_out, mlp_w1, mlp_w2), nb, nct, n_ctx)
        att = _attention(q, k, v, n_ctx, with_ctx)
        y_f, y_b = _wkv(pack, nct)
        h_mid, xn_mid = _mix_out(att, y_f, y_b, pack, pool, conv, h_parts, mod[l], w_out_b,
                                 norm2_g[l].reshape(1, D_MODEL), rw_p, nb, nct, with_ctx)

        n_rows = h_mid.shape[1]
        if with_ctx:
            tm = 3 * ROW_TILE if n_rows % (3 * ROW_TILE) == 0 else ROW_TILE
            ctx_rows = n_ctx
            assert n_ctx <= tm, "context longer than one MLP row tile"
        else:
            tm = 4 * ROW_TILE if n_rows % (4 * ROW_TILE) == 0 else ROW_TILE
            ctx_rows = 0
        h_parts = (_mlp(h_mid, xn_mid, mod[l], w1_b, w2_b, final_norm_g, nb, tm, ctx_rows, last),)
    return h_parts[0]
```

```python
import functools

import numpy as np
import jax
import jax.numpy as jnp
from jax import lax
from jax.experimental import pallas as pl
from jax.experimental.pallas import tpu as pltpu

F32 = jnp.float32
BF16 = jnp.bfloat16

D_MODEL = 2048
GRID_W = 64
NORM_EPS = 1e-6
GROUP_WIDTH = D_MODEL // 4

MLA_NOPE = 128
MLA_ROPE = 64
MLA_V = 128
MLA_HEADS = GROUP_WIDTH // MLA_V
MLA_Q_RANK = 384
MLA_KV_RANK = 128
ROPE_BASE = 10000.0
MLA_QK_PAD = 256
MLA_FEAT = 640

RWKV_HEAD = 64
RWKV_HEADS = GROUP_WIDTH // RWKV_HEAD
RWKV_WIDTH = RWKV_HEADS * RWKV_HEAD
DECAY_RANK = 96
ICL_RANK = 96
GATE_RANK = 256
RWKV_GN_EPS = 64e-5
RWKV_COLS = 3 * RWKV_WIDTH + 2 * DECAY_RANK + 2 * ICL_RANK + GATE_RANK
WKV_CHUNK = 64
RWKV_PACK_BLOCKS = 11

POOL_WINDOWS = (2, 4, 8, 16)
POOL_WIDTH = GROUP_WIDTH
POOL_GROUP = POOL_WIDTH // len(POOL_WINDOWS)
CONV_WIDTH = GROUP_WIDTH
CONV_COLS = 3 * CONV_WIDTH
FF_HIDDEN = 4 * D_MODEL
MLA_COLS = MLA_Q_RANK + MLA_KV_RANK + MLA_ROPE

ROW_TILE = 256
HALO = 8
MOD_ROWS = 16
VMEM_LIMIT = 60 * 1024 * 1024


def _cparams(sem):
    return pltpu.CompilerParams(dimension_semantics=sem, vmem_limit_bytes=VMEM_LIMIT)


def _resident(shape, index_map):
    return pl.BlockSpec(shape, index_map, pipeline_mode=pl.Buffered(1))


def _rms(x, g):
    return x * lax.rsqrt(jnp.mean(x * x, axis=-1, keepdims=True) + NORM_EPS) * g


def _dot(a, b):
    return jnp.dot(a.astype(BF16), b.astype(BF16), preferred_element_type=F32)


def _dot_nt(a, b):
    return lax.dot_general(a.astype(BF16), b.astype(BF16), (((1,), (1,)), ((), ())), preferred_element_type=F32)


def _split3(x):
    h1 = x.astype(BF16)
    r1 = x - h1.astype(F32)
    h2 = r1.astype(BF16)
    h3 = (r1 - h2.astype(F32)).astype(BF16)
    return h1, h2, h3


def _dot_exact_lhs(a01, x):
    return jnp.dot(jnp.concatenate([a01] * 3, axis=1), jnp.concatenate(_split3(x), axis=0), preferred_element_type=F32)


def _dot_x2(a, b):
    a1 = a.astype(BF16)
    b1 = b.astype(BF16)
    b2 = (b - b1.astype(F32)).astype(BF16)
    return jnp.dot(jnp.concatenate([a1, a1], axis=1), jnp.concatenate([b1, b2], axis=0), preferred_element_type=F32)


def _mod_kernel(c_ref, w_ref, b_ref, o_ref):
    c = c_ref[...]
    s = c * jax.nn.sigmoid(c)
    o_ref[...] = _dot(s, w_ref[...]) + b_ref[...]


def _modulation(cc, ada_w, ada_b):
    depth = ada_w.shape[0]
    tn = 1024
    return pl.pallas_call(
        _mod_kernel,
        grid=(depth, 6 * D_MODEL // tn),
        in_specs=[
            pl.BlockSpec((MOD_ROWS, D_MODEL), lambda l, j: (0, 0)),
            pl.BlockSpec((None, D_MODEL, tn), lambda l, j: (l, 0, j)),
            pl.BlockSpec((None, 1, tn), lambda l, j: (l, 0, j)),
        ],
        out_specs=pl.BlockSpec((None, MOD_ROWS, tn), lambda l, j: (l, 0, j)),
        out_shape=jax.ShapeDtypeStruct((depth, MOD_ROWS, 6 * D_MODEL), F32),
        compiler_params=_cparams(("parallel", "parallel")),
        name="modulation",
    )(cc, ada_w, ada_b.reshape(depth, 1, 6 * D_MODEL))


def _mod_row(mod_ref, row, k):
    return mod_ref[pl.ds(row, 1), k * D_MODEL:(k + 1) * D_MODEL]


def _halo_specs(width, n_rows, tile0=0):
    per = ROW_TILE // HALO
    last_tile = n_rows // ROW_TILE - 1
    last = n_rows // HALO - 1
    clamp = lambda i, hi: jnp.clip(i, 0, hi)
    cur = pl.BlockSpec((None, ROW_TILE, width), lambda b, t: (b, clamp(t - tile0, last_tile), 0))
    prev = pl.BlockSpec((None, HALO, width), lambda b, t: (b, clamp((t - tile0) * per - 1, last), 0))
    nxt = pl.BlockSpec((None, HALO, width), lambda b, t: (b, clamp((t - tile0 + 1) * per, last), 0))
    return cur, prev, nxt


def _token_tile_spec(n_rows, tile0, t_off):
    last_tile = n_rows // ROW_TILE - 1
    return pl.BlockSpec((None, ROW_TILE, D_MODEL), lambda b, t: (b, jnp.clip(t + t_off - tile0, 0, last_tile), 0))


def _seq_edges(t, nct, nt):
    has_prev = jnp.logical_and(t != 0, t != nct)
    has_next = jnp.logical_and(t != nct - 1, t != nt - 1)
    return has_prev, has_next


def _seg_sum(x, e_ref):
    h1 = x.astype(BF16)
    h2 = (x - h1.astype(F32)).astype(BF16)
    e = e_ref[...]
    return jnp.dot(h1, e, preferred_element_type=F32) + jnp.dot(h2, e, preferred_element_type=F32)


IN_SEGMENTS = (MLA_FEAT, RWKV_COLS, POOL_WIDTH, CONV_COLS)
HALO_ROWS = ROW_TILE + 2 * HALO
LO, HI = HALO, HALO + ROW_TILE


def _rope(x, cs, sn):
    return x * cs + pltpu.roll(x, 64, 1) * sn


def _store_with_halo(buf, cols, res):
    buf[LO:HI, cols] = res[0:ROW_TILE]
    buf[0:LO, cols] = res[ROW_TILE:ROW_TILE + HALO]
    buf[HI:HI + HALO, cols] = res[ROW_TILE + HALO:ROW_TILE + 2 * HALO]


def _mla_qkv(f, cs, sn, gq_ref, gkv_ref, wq_ref, wkv_ref, q_ref, k_ref, v_ref):
    scale = (MLA_NOPE + MLA_ROPE) ** -0.5
    q = _dot(_rms(f[:, 0:MLA_Q_RANK], gq_ref[...]), wq_ref[...])
    kv = _dot(_rms(f[:, MLA_Q_RANK:MLA_Q_RANK + MLA_KV_RANK], gkv_ref[...]), wkv_ref[...])
    k_pe = _rope(f[:, 512:640], cs, sn).astype(BF16)
    for h in range(MLA_HEADS):
        o = h * MLA_QK_PAD
        q_ref[:, o:o + 128] = (q[:, o:o + 128] * scale).astype(BF16)
        q_ref[:, o + 128:o + 256] = (_rope(q[:, o + 128:o + 256], cs, sn) * scale).astype(BF16)
        k_ref[:, o:o + 128] = kv[:, o:o + 128].astype(BF16)
        k_ref[:, o + 128:o + 256] = k_pe
        v_ref[:, h * MLA_V:(h + 1) * MLA_V] = kv[:, o + 128:o + 256].astype(BF16)


def _rwkv_features(buf, mu_ref, w0_ref, a0_ref, w2_ref, a2_ref, g2_ref, kk_ref, ka_ref, rk_ref, e_ref, o_ref):
    w = RWKV_WIDTH

    def shifted(cols):
        f = buf[LO:HI, cols]
        return f + mu_ref[:, cols] * (0.5 * (buf[LO - 1:HI - 1, cols] + buf[LO + 1:HI + 1, cols]) - f)

    def put(i, val):
        o_ref[:, i * w:(i + 1) * w] = val

    xw = w0_ref[...] + _dot(jnp.tanh(shifted(slice(1536, 1792))), w2_ref[...])
    lw = -float(np.exp(-0.5)) * jax.nn.sigmoid(xw)
    put(5, lw[:, 0:w])
    put(8, lw[:, w:2 * w])
    yield
    a = jax.nn.sigmoid(a0_ref[...] + _dot(shifted(slice(1664, 1920)), a2_ref[...]))
    put(7, a[:, 0:w])
    put(10, a[:, w:2 * w])
    put(3, _dot(jax.nn.sigmoid(shifted(slice(1920, 2176))), g2_ref[...]))
    yield
    k = shifted(slice(w, 2 * w))
    kkv = k * kk_ref[...]
    put(2, kkv / jnp.maximum(jnp.sqrt(_seg_sum(kkv * kkv, e_ref)), 1e-12))
    yield
    r = shifted(slice(0, w))
    put(0, r)
    rk = r * rk_ref[...]
    ka = ka_ref[...]
    k_f = k * (1.0 + (a[:, 0:w] - 1.0) * ka)
    k_b = k * (1.0 + (a[:, w:2 * w] - 1.0) * ka)
    put(6, k_f)
    put(9, k_b)
    yield
    v = shifted(slice(2 * w, 3 * w))
    put(1, v)
    put(4, _seg_sum(rk * k_f + rk * k_b, e_ref) * v)


def _pool_mixer(buf, pw_ref, ps_ref, o_ref, pos, n_own):
    for gi, win in enumerate(POOL_WINDOWS):
        hw = win // 2
        cols = slice(gi * POOL_GROUP, (gi + 1) * POOL_GROUP)
        s = buf[LO - hw:HI - hw, cols]
        for o in range(-hw + 1, hw):
            s = s + buf[LO + o:HI + o, cols]
        cnt = (jnp.minimum(pos + hw, n_own) - jnp.maximum(pos - hw, 0)).astype(F32)
        z = s / cnt - buf[LO:HI, cols]
        o_ref[:, cols] = (_dot(z, pw_ref[gi]) * ps_ref[:, cols]).astype(o_ref.dtype)


def _pick_rows(refs, in_ctx):
    if len(refs) == 3:
        return [r[...] for r in refs]
    return [jnp.where(in_ctx, c[...], x[...]) for c, x in zip(refs[0:3], refs[3:6])]


FRONT_FIXED_INPUTS = 22


def _front_kernel(*refs, nparts, ncast, nb, nct, nt, n_ctx, n_seq):
    x_refs, refs = refs[:3 * nparts], refs[3 * nparts:]
    (mod_ref, g_ref, w_ref, cs_ref, sn_ref, gq_ref, gkv_ref, wq_ref, wkv_ref,
     mu_ref, w0_ref, a0_ref, w2_ref, a2_ref, g2_ref, kk_ref, ka_ref, rk_ref, e_ref,
     pw_ref, ps_ref, cw_ref) = refs[:FRONT_FIXED_INPUTS]
    cast_in, refs = refs[FRONT_FIXED_INPUTS:FRONT_FIXED_INPUTS + ncast], refs[FRONT_FIXED_INPUTS + ncast:]
    q_ref, k_ref, v_ref, pack_ref, pool_ref, conv_ref = refs[:6]
    cast_out = refs[6:6 + ncast]
    rw_buf, pool_buf, u_buf = refs[6 + ncast:]
    for src, dst in zip(cast_in, cast_out):
        dst[...] = src[...].astype(dst.dtype)
    b = pl.program_id(0)
    t = pl.program_id(1)
    in_ctx = t < nct
    row = jnp.where(in_ctx, nb, b)
    has_prev, has_next = _seq_edges(t, nct, nt)
    shift, scale = _mod_row(mod_ref, row, 0), 1.0 + _mod_row(mod_ref, row, 1)
    norm = lambda x: _rms(x, g_ref[...]) * scale + shift
    x_cur, x_prev, x_next = _pick_rows(x_refs, in_ctx)
    xn = jnp.concatenate([norm(x_cur), jnp.where(has_prev, norm(x_prev), 0.0), jnp.where(has_next, norm(x_next), 0.0)],
                         axis=0).astype(BF16)

    mla0 = 0
    rw0 = IN_SEGMENTS[0]
    pool0 = rw0 + RWKV_COLS
    conv0 = pool0 + POOL_WIDTH
    cw = CONV_WIDTH
    proj = lambda rows, c0, c1: jnp.dot(rows, w_ref[:, c0:c1], preferred_element_type=F32)

    c0 = 0
    while c0 < RWKV_COLS:
        c1 = min(c0 + 512, RWKV_COLS)
        _store_with_halo(rw_buf, slice(c0, c1), proj(xn, rw0 + c0, rw0 + c1))
        c0 = c1
    stages = _rwkv_features(rw_buf, mu_ref, w0_ref, a0_ref, w2_ref, a2_ref, g2_ref, kk_ref, ka_ref, rk_ref, e_ref, pack_ref)
    next(stages)
    f_mla = proj(xn[0:ROW_TILE], mla0, rw0)
    _store_with_halo(pool_buf, slice(None), proj(xn, pool0, conv0))
    next(stages)
    gb = proj(xn[0:ROW_TILE], conv0, conv0 + cw)
    _mla_qkv(f_mla, cs_ref[...], sn_ref[...], gq_ref, gkv_ref, wq_ref, wkv_ref, q_ref, k_ref, v_ref)
    next(stages)
    gc = proj(xn, conv0 + cw, conv0 + 2 * cw)
    pos = lax.broadcasted_iota(jnp.int32, (ROW_TILE, 1), 0) + jnp.where(in_ctx, t, t - nct) * ROW_TILE
    _pool_mixer(pool_buf, pw_ref, ps_ref, pool_ref, pos, jnp.where(in_ctx, n_ctx, n_seq))
    next(stages)
    hx = proj(xn, conv0 + 2 * cw, conv0 + 3 * cw)
    for _ in stages:
        pass
    _store_with_halo(u_buf, slice(None), gc * hx)
    z = cw_ref[0:1, :] * u_buf[LO - 1:HI - 1] + cw_ref[1:2, :] * u_buf[LO:HI] + cw_ref[2:3, :] * u_buf[LO + 1:HI + 1]
    conv_ref[...] = (gb * z).astype(conv_ref.dtype)


def _layer_spec(a, l):
    return _resident((None,) + a.shape[1:], lambda *_: (l,) + (0,) * (a.ndim - 1))


def _cast_slab_spec(w, l, b_, nt):
    _, rows, cols = w.shape
    k = 1
    while k * 2 <= nt and cols % (k * 2 * 128) == 0:
        k *= 2
    assert rows % (b_ * 8) == 0
    shape = (rows // b_, cols // k)
    col = lambda t: jnp.minimum(t, k - 1)
    return (pl.BlockSpec((None,) + shape, lambda b, t: (l, b, col(t))),
            pl.BlockSpec(shape, lambda b, t: (b, col(t))),
            jax.ShapeDtypeStruct((rows, cols), BF16))


def _front(h_parts, l, mod, g, w_p, cs, sn, gq, gkv, wq, wkv, p, pool_w, pool_scale, conv_w, cast_ws, nb, nct, n_ctx):
    b_ = h_parts[0].shape[0]
    n_ = sum(h.shape[1] for h in h_parts)
    nt = n_ // ROW_TILE
    casts = [_cast_slab_spec(w, l, b_, nt) for w in cast_ws]
    qk_w = MLA_HEADS * MLA_QK_PAD
    pack_w = RWKV_PACK_BLOCKS * RWKV_WIDTH
    full = lambda a: _resident(a.shape, lambda b, t: (0,) * a.ndim)
    row = lambda w: pl.BlockSpec((None, ROW_TILE, w), lambda b, t: (b, t, 0))
    tab = pl.BlockSpec((ROW_TILE, 128), lambda b, t: (t, 0))
    x_specs, x_args, tile0 = [], [], 0
    for h in h_parts:
        x_specs += _halo_specs(D_MODEL, h.shape[1], tile0)
        x_args += [h, h, h]
        tile0 += h.shape[1] // ROW_TILE
    consts = (mod, g.reshape(1, D_MODEL))
    mla = (gq.reshape(1, -1), gkv.reshape(1, -1), wq, wkv)
    rwk = (p["mu"], p["w0"], p["a0"], p["w2"], p["a2"], p["g2"], p["k_k"], p["k_a"], p["r_k"], p["seg"])
    mix = (pool_w, pool_scale, conv_w)
    fixed = [full(a) for a in consts] + [_layer_spec(w_p, l), tab, tab] + [full(a) for a in mla + rwk + mix]
    assert len(fixed) == FRONT_FIXED_INPUTS
    outs = pl.pallas_call(
        functools.partial(_front_kernel, nparts=len(h_parts), ncast=len(casts), nb=nb, nct=nct, nt=nt,
                          n_ctx=n_ctx, n_seq=n_ - n_ctx),
        grid=(b_, nt),
        in_specs=x_specs + fixed + [c[0] for c in casts],
        out_specs=(row(qk_w), row(qk_w), row(MLA_HEADS * MLA_V), row(pack_w), row(POOL_WIDTH), row(CONV_WIDTH))
                  + tuple(c[1] for c in casts),
        out_shape=(
            jax.ShapeDtypeStruct((b_, n_, qk_w), BF16),
            jax.ShapeDtypeStruct((b_, n_, qk_w), BF16),
            jax.ShapeDtypeStruct((b_, n_, MLA_HEADS * MLA_V), BF16),
            jax.ShapeDtypeStruct((b_, n_, pack_w), F32),
            jax.ShapeDtypeStruct((b_, n_, POOL_WIDTH), BF16),
            jax.ShapeDtypeStruct((b_, n_, CONV_WIDTH), BF16),
        ) + tuple(c[2] for c in casts),
        scratch_shapes=[pltpu.VMEM((HALO_ROWS, RWKV_COLS), F32), pltpu.VMEM((HALO_ROWS, POOL_WIDTH), F32),
                        pltpu.VMEM((HALO_ROWS, CONV_WIDTH), F32)],
        compiler_params=_cparams(("parallel", "arbitrary")),
        name="front",
    )(*x_args, *consts, w_p, cs, sn, *mla, *rwk, *mix, *cast_ws)
    return outs[:6], outs[6:]


ATTN_Q_TILE = 512
ATTN_SOFTMAX_ROWS = 128


def _attn_kernel(q_ref, k_ref, v_ref, o_ref, *, n_ctx, n_all, tq, with_ctx):
    s = pl.program_id(1)

    def attend(q_rows, o_rows, nk, nq):
        for h in range(MLA_HEADS):
            q = q_ref[q_rows, h * MLA_QK_PAD:(h + 1) * MLA_QK_PAD]
            k = k_ref[0:nk, h * MLA_QK_PAD:(h + 1) * MLA_QK_PAD]
            sc = lax.dot_general(q, k, (((1,), (1,)), ((), ())), preferred_element_type=F32)
            ps, ls = [], []
            for r0 in range(0, nq, ATTN_SOFTMAX_ROWS):
                blk = sc[r0:r0 + ATTN_SOFTMAX_ROWS]
                p = jnp.exp(blk - jnp.max(blk, axis=-1, keepdims=True))
                ls.append(jnp.sum(p, axis=-1, keepdims=True))
                ps.append(p.astype(BF16))
            p = jnp.concatenate(ps, axis=0)
            l = jnp.concatenate(ls, axis=0)
            o = jnp.dot(p, v_ref[0:nk, h * MLA_V:(h + 1) * MLA_V], preferred_element_type=F32)
            o_ref[o_rows, h * MLA_V:(h + 1) * MLA_V] = (o / l).astype(o_ref.dtype)

    if with_ctx:
        @pl.when(s == 0)
        def _():
            attend(slice(0, n_ctx), slice(0, n_ctx), n_ctx, n_ctx)

        @pl.when(s > 0)
        def _():
            r0 = pl.multiple_of(n_ctx + (s - 1) * tq, ROW_TILE)
            attend(pl.ds(r0, tq), pl.ds(r0, tq), n_all, tq)
    else:
        attend(pl.ds(pl.multiple_of(n_ctx + s * tq, ROW_TILE), tq), pl.ds(pl.multiple_of(s * tq, ROW_TILE), tq), n_all, tq)


def _attention(q, k, v, n_ctx, with_ctx):
    b_, n_, qk_w = q.shape
    n_seq = n_ - n_ctx
    tq = ATTN_Q_TILE if n_seq % ATTN_Q_TILE == 0 else ROW_TILE
    n_out = n_ if with_ctx else n_seq
    whole = lambda rows, w: pl.BlockSpec((None, rows, w), lambda b, s: (b, 0, 0))
    return pl.pallas_call(
        functools.partial(_attn_kernel, n_ctx=n_ctx, n_all=n_, tq=tq, with_ctx=with_ctx),
        grid=(b_, n_seq // tq + (1 if with_ctx else 0)),
        in_specs=[whole(n_, qk_w), whole(n_, qk_w), whole(n_, MLA_HEADS * MLA_V)],
        out_specs=whole(n_out, MLA_HEADS * MLA_V),
        out_shape=jax.ShapeDtypeStruct((b_, n_out, MLA_HEADS * MLA_V), BF16),
        compiler_params=_cparams(("parallel", "arbitrary")),
        name="mla_attention",
    )(q, k, v)


def _bd(x, m0):
    return jnp.concatenate([jnp.where(m0, x, 0.0), jnp.where(m0, 0.0, x)], axis=0)


def _wkv_stages(r_ref, v_ref, kk_ref, lw_ref, kd_ref, a_ref, y_ref, s_ref, rw_s, y0_s, m_s, n_s, rev):
    c_ = WKV_CHUNK
    nsub = ROW_TILE // c_
    npair = RWKV_WIDTH // 128

    ri = lax.broadcasted_iota(jnp.int32, (c_, c_), 0)
    ci = lax.broadcasted_iota(jnp.int32, (c_, c_), 1)
    tmat = ((ri <= ci) if rev else (ri >= ci)).astype(BF16)
    rc = lax.broadcasted_iota(jnp.int32, (c_, 128), 0)
    cc = lax.broadcasted_iota(jnp.int32, (c_, 128), 1) % c_
    strict = (rc < cc) if rev else (rc > cc)
    incl = (rc <= cc) if rev else (rc >= cc)
    eye = rc == cc
    m0_128 = lax.broadcasted_iota(jnp.int32, (1, 128), 1) < c_
    m0_256 = (lax.broadcasted_iota(jnp.int32, (1, 256), 1) % 128) < c_
    zeros_cv = jnp.zeros((c_, 128), F32)

    items = [(c, p) for c in range(nsub) for p in range(npair)]
    at, rt, bt, kt, bh, kh, vv, e_tot = {}, {}, {}, {}, {}, {}, {}, {}
    for c in range(nsub):
        rows = slice(c * c_, (c + 1) * c_)
        lw = lw_ref[rows, :]
        lc = _dot_exact_lhs(tmat, lw)
        ltot = jnp.sum(lw, axis=0, keepdims=True)
        e_neg = jnp.exp(-lc)
        e_end = jnp.exp(ltot - lc)
        kk = kk_ref[rows, :]
        kd = kd_ref[rows, :]
        bv = kk * a_ref[rows, :]
        vv[c] = v_ref[rows, :]
        at[c] = -kk * jnp.exp(lc - lw)
        rt[c] = r_ref[rows, :] * jnp.exp(lc)
        bt[c], kt[c] = bv * e_neg, kd * e_neg
        bh[c], kh[c] = bv * e_end, kd * e_end
        e_tot[c] = jnp.exp(ltot)
        yield
    sl = lambda p: slice(p * 128, (p + 1) * 128)

    g = {}
    for c, p in items:
        ar = jnp.concatenate([at[c][:, sl(p)], rt[c][:, sl(p)]], axis=0)
        bd = jnp.concatenate([_bd(bt[c][:, sl(p)], m0_128), _bd(kt[c][:, sl(p)], m0_128)], axis=0)
        g[c, p] = _dot_nt(ar, bd)
    yield
    av, x, pw, a_rb = {}, {}, {}, {}
    for c, p in items:
        a_ak = jnp.where(strict, g[c, p][0:c_, 128:256], 0.0)
        a_rk = jnp.where(incl, g[c, p][c_:2 * c_, 128:256], 0.0)
        av[c, p] = _dot(jnp.concatenate([a_ak, a_rk], axis=0), _bd(vv[c][:, sl(p)], m0_128))
        pw[c, p] = jnp.where(strict, g[c, p][0:c_, 0:128], 0.0)
        a_rb[c, p] = jnp.where(incl, g[c, p][c_:2 * c_, 0:128], 0.0)
    for c, p in items:
        x[c, p] = jnp.concatenate([at[c][:, sl(p)], av[c, p][0:c_]], axis=1)
    yield
    for it in range(6):
        dot = _dot_x2 if it < 2 else _dot
        for c, p in items:
            x[c, p] = x[c, p] + dot(pw[c, p], _bd(x[c, p], m0_256))
        yield
        if it < 5:
            for c, p in items:
                pw[c, p] = _dot(pw[c, p], _bd(pw[c, p], m0_128))
            yield
    for c, p in items:
        ry = _dot(a_rb[c, p], _bd(x[c, p], m0_256))
        rw_s[c, p] = rt[c][:, sl(p)] + ry[:, 0:128]
        y0_s[c, p] = ry[:, 128:256] + av[c, p][c_:2 * c_]
    yield
    for c, p in items:
        bkt = jnp.concatenate([bh[c][:, sl(p)], kh[c][:, sl(p)]], axis=0).T
        rhs = jnp.concatenate([x[c, p], jnp.concatenate([zeros_cv, vv[c][:, sl(p)]], axis=1)], axis=0)
        z = _dot(bkt, rhs)
        m_s[c, p] = (jnp.where(m0_128, z[0:c_, 0:128], z[c_:2 * c_, 0:128])
                     + jnp.where(eye, e_tot[c][:, sl(p)], 0.0))
        n_s[c, p] = jnp.where(m0_128, z[0:c_, 128:256], z[c_:2 * c_, 128:256])
    yield

    for i in range(nsub):
        c = (nsub - 1 - i) if rev else i
        for p in range(npair):
            lhs = jnp.concatenate([rw_s[c, p], m_s[c, p]], axis=0)
            o = _dot_x2(lhs, _bd(s_ref[p], m0_128))
            y_ref[c * c_:(c + 1) * c_, sl(p)] = o[0:c_] + y0_s[c, p]
            s_ref[p] = o[c_:2 * c_] + n_s[c, p]
        yield


WKV_LEAD_STAGES = 9


def _wkv_kernel(*refs):
    ins_f, ins_b, (y_f, y_b), scr_f, scr_b = refs[0:6], refs[6:12], refs[12:14], refs[14:19], refs[19:24]

    @pl.when(pl.program_id(1) == 0)
    def _():
        scr_f[0][...] = jnp.zeros_like(scr_f[0])
        scr_b[0][...] = jnp.zeros_like(scr_b[0])

    fwd = _wkv_stages(*ins_f, y_f, *scr_f, rev=False)
    bwd = _wkv_stages(*ins_b, y_b, *scr_b, rev=True)
    for _ in range(WKV_LEAD_STAGES):
        next(fwd)
    live = [fwd, bwd]
    while live:
        for gen in list(live):
            if next(gen, StopIteration) is StopIteration:
                live.remove(gen)


def _wkv(pack, nct):
    b_, n_, _ = pack.shape
    nt = n_ // ROW_TILE
    w = RWKV_WIDTH
    nsub = ROW_TILE // WKV_CHUNK
    npair = w // 128
    fwd_tile = lambda s: s
    bwd_tile = lambda s: jnp.where(s < nct, nct - 1 - s, nt + nct - 1 - s)
    col = lambda order, j: pl.BlockSpec((None, ROW_TILE, w), lambda b, s: (b, order(s), j))
    out = lambda order: pl.BlockSpec((None, ROW_TILE, w), lambda b, s: (b, order(s), 0))
    scr = lambda: [pltpu.VMEM((npair, WKV_CHUNK, 128), F32)] + [pltpu.VMEM((nsub, npair, WKV_CHUNK, 128), F32)] * 4
    y = jax.ShapeDtypeStruct((b_, n_, w), F32)
    return pl.pallas_call(
        _wkv_kernel,
        grid=(b_, nt),
        in_specs=[col(fwd_tile, j) for j in (0, 1, 2, 5, 6, 7)] + [col(bwd_tile, j) for j in (0, 1, 2, 8, 9, 10)],
        out_specs=(out(fwd_tile), out(bwd_tile)),
        out_shape=(y, y),
        scratch_shapes=scr() + scr(),
        compiler_params=_cparams(("parallel", "arbitrary")),
        name="wkv",
    )(*([pack] * 12))


def _mix_out_kernel(att_ref, yf_ref, yb_ref, g_ref, bonus_ref, lg_ref, lb_ref, e_ref, pool_ref, conv_ref,
                    mod_ref, w_ref, g2_ref, *refs, nb, nct, t_off):
    h_refs, (o_ref, xn_ref) = refs[:-2], refs[-2:]
    b = pl.program_id(0)
    t = pl.program_id(1) + t_off
    in_ctx = t < nct
    row = jnp.where(in_ctx, nb, b)
    h = h_refs[0][...] if len(h_refs) == 1 else jnp.where(in_ctx, h_refs[0][...], h_refs[1][...])
    y = yf_ref[...] + yb_ref[...]
    mean = _seg_sum(y, e_ref) * (1.0 / RWKV_HEAD)
    d = y - mean
    var = _seg_sum(d * d, e_ref) * (1.0 / RWKV_HEAD)
    yn = d * lax.rsqrt(var + RWKV_GN_EPS) * lg_ref[...] + lb_ref[...]
    rw = ((yn + bonus_ref[...]) * g_ref[...]).astype(BF16)
    mixed = jnp.concatenate([att_ref[...], rw, pool_ref[...], conv_ref[...]], axis=1)
    acc = jnp.dot(mixed, w_ref[...], preferred_element_type=F32)
    h = h + _mod_row(mod_ref, row, 2) * acc
    o_ref[...] = h
    xn_ref[...] = (_rms(h, g2_ref[...]) * (1.0 + _mod_row(mod_ref, row, 4)) + _mod_row(mod_ref, row, 3)).astype(BF16)


def _mix_out(att, yf, yb, pack, pool, conv, h_parts, mod, w_out, g2, p, nb, nct, with_ctx):
    b_, n_, gw = yf.shape
    t_off = 0 if with_ctx else nct
    n_out = n_ - t_off * ROW_TILE
    blk = lambda j: pl.BlockSpec((None, ROW_TILE, gw), lambda b, t: (b, t + t_off, j))
    full = lambda a: _resident(a.shape, lambda b, t: (0,) * a.ndim)
    h_specs, tile0 = [], 0
    for h in h_parts:
        h_specs.append(_token_tile_spec(h.shape[1], tile0, t_off))
        tile0 += h.shape[1] // ROW_TILE
    out = pl.BlockSpec((None, ROW_TILE, D_MODEL), lambda b, t: (b, t, 0))
    return pl.pallas_call(
        functools.partial(_mix_out_kernel, nb=nb, nct=nct, t_off=t_off),
        grid=(b_, n_out // ROW_TILE),
        in_specs=[pl.BlockSpec((None, ROW_TILE, gw), lambda b, t: (b, t, 0)),
                  blk(0), blk(0), blk(3), blk(4), full(p["ln_g"]), full(p["ln_b"]), full(p["seg"]),
                  blk(0), blk(0), full(mod), full(w_out), full(g2)] + h_specs,
        out_specs=(out, out),
        out_shape=(jax.ShapeDtypeStruct((b_, n_out, D_MODEL), F32), jax.ShapeDtypeStruct((b_, n_out, D_MODEL), BF16)),
        compiler_params=_cparams(("parallel", "parallel")),
        name="mix_out",
    )(att, yf, yb, pack, pack, p["ln_g"], p["ln_b"], p["seg"], pool, conv, mod, w_out, g2, *h_parts)


MLP_HIDDEN_TILE = 512


def _mlp_kernel(h_ref, xn_ref, mod_ref, w1_ref, w2_ref, gf_ref, o_ref, *, nb, ctx_rows, final):
    b = pl.program_id(0)
    t = pl.program_id(1)
    j = pl.program_id(2)

    def hidden_slice(first):
        hid = jnp.dot(xn_ref[...], w1_ref[...], preferred_element_type=F32)
        hid = jnp.square(jnp.maximum(hid, 0.0))
        part = jnp.dot(hid.astype(BF16), w2_ref[...], preferred_element_type=F32)
        o_ref[...] = part if first else o_ref[...] + part

    @pl.when(j == 0)
    def _():
        hidden_slice(True)

    @pl.when(j > 0)
    def _():
        hidden_slice(False)

    @pl.when(j == pl.num_programs(2) - 1)
    def _():
        def put(rows, mrow):
            o = h_ref[rows, :] + _mod_row(mod_ref, mrow, 5) * o_ref[rows, :]
            if final:
                o = _rms(o, gf_ref[...])
            o_ref[rows, :] = o

        first_row = jnp.where(t == 0, nb, b)
        if ctx_rows == 0:
            put(slice(None), b)
        elif ctx_rows == h_ref.shape[0]:
            put(slice(None), first_row)
        else:
            put(slice(0, ctx_rows), first_row)
            put(slice(ctx_rows, None), b)


def _mlp(h, xn, mod, w1, w2, g_final, nb, tm, ctx_rows, final):
    b_, n_, _ = h.shape
    th = MLP_HIDDEN_TILE * (2 if tm < 1024 else 1)
    tile = pl.BlockSpec((None, tm, D_MODEL), lambda b, t, j: (b, t, 0))
    return pl.pallas_call(
        functools.partial(_mlp_kernel, nb=nb, ctx_rows=ctx_rows, final=final),
        grid=(b_, n_ // tm, FF_HIDDEN // th),
        in_specs=[
            tile, tile,
            _resident((MOD_ROWS, 6 * D_MODEL), lambda b, t, j: (0, 0)),
            pl.BlockSpec((D_MODEL, th), lambda b, t, j: (0, j)),
            pl.BlockSpec((th, D_MODEL), lambda b, t, j: (j, 0)),
            _resident((1, D_MODEL), lambda b, t, j: (0, 0)),
        ],
        out_specs=tile,
        out_shape=jax.ShapeDtypeStruct((b_, n_, D_MODEL), F32),
        compiler_params=_cparams(("parallel", "parallel", "arbitrary")),
        name="mlp",
    )(h, xn, mod, w1, w2, g_final.reshape(1, D_MODEL))


def _rope_partner_perm():
    q = MLA_ROPE // 4
    i = np.arange(MLA_ROPE)
    return np.where((i // q) % 2 == 0, i + q, i - q)


def _rope_tables(n_ctx, n_seq):
    f32 = np.float32
    rows = n_seq // GRID_W
    row = np.repeat(np.arange(rows), GRID_W)
    col = np.tile(np.arange(GRID_W), rows)
    pos = np.stack([row, col], axis=-1).astype(f32)
    axis_dim = MLA_ROPE // 2
    inv_freq = (f32(ROPE_BASE) ** (-np.arange(0, axis_dim, 2, dtype=f32) / f32(axis_dim))).astype(f32)
    ang = (pos[:, :, None] * inv_freq).astype(f32)
    cos, sin = np.cos(ang), np.sin(ang)
    cos64 = np.concatenate([cos[:, 0], cos[:, 0], cos[:, 1], cos[:, 1]], axis=-1)
    sin64 = np.concatenate([-sin[:, 0], sin[:, 0], -sin[:, 1], sin[:, 1]], axis=-1)
    pad = np.zeros((n_seq, 128 - MLA_ROPE), f32)
    cs_lat = np.concatenate([cos64, pad], axis=-1)
    sn_lat = np.concatenate([sin64, pad], axis=-1)
    cs_ctx = np.concatenate([np.ones((n_ctx, MLA_ROPE), f32), np.zeros((n_ctx, 128 - MLA_ROPE), f32)], axis=-1)
    cs = np.concatenate([cs_ctx, cs_lat], axis=0).astype(f32)
    sn = np.concatenate([np.zeros((n_ctx, 128), f32), sn_lat], axis=0).astype(f32)
    return jnp.asarray(cs), jnp.asarray(sn)


def _in_proj_relayout_kernel(w_ref, o_ref):
    kr0 = MLA_Q_RANK + MLA_KV_RANK
    n_in = w_ref.shape[1]
    tail0 = (n_in // 128) * 128
    o_ref[:, 0:kr0] = w_ref[:, 0:kr0].astype(BF16)
    head = w_ref[:, kr0:kr0 + 128]
    lane = lax.broadcasted_iota(jnp.int32, head.shape, 1)
    q = MLA_ROPE // 4
    partner = jnp.where((lane // q) % 2 == 0, pltpu.roll(head, 128 - q, 1), pltpu.roll(head, q, 1))
    o_ref[:, kr0:kr0 + 128] = jnp.where(lane < MLA_ROPE, head, pltpu.roll(partner, MLA_ROPE, 1)).astype(BF16)
    body = w_ref[:, kr0:tail0]
    width = tail0 - kr0
    up = pltpu.roll(body, width - MLA_ROPE, 1)
    o_ref[:, kr0 + 128:tail0] = up[:, 0:width - 128].astype(BF16)
    o_ref[:, tail0:tail0 + 128] = jnp.concatenate(
        [up[:, width - 128:width - MLA_ROPE], w_ref[:, tail0:n_in]], axis=1).astype(BF16)


def _in_proj_weights(w_in):
    depth, rows, n_in = w_in.shape
    assert n_in - (n_in // 128) * 128 == MLA_ROPE and n_in + MLA_ROPE == sum(IN_SEGMENTS)
    tr = 256
    return pl.pallas_call(
        _in_proj_relayout_kernel,
        grid=(depth, rows // tr),
        in_specs=[pl.BlockSpec((None, tr, n_in), lambda l, r: (l, r, 0))],
        out_specs=pl.BlockSpec((None, tr, n_in + MLA_ROPE), lambda l, r: (l, r, 0)),
        out_shape=jax.ShapeDtypeStruct((depth, rows, n_in + MLA_ROPE), BF16),
        compiler_params=_cparams(("parallel", "parallel")),
        name="in_proj_relayout",
    )(w_in)


def _layer_params(l, mla_w_uq, mla_w_ukv, rwkv_mu, rwkv_w0, rwkv_w2, rwkv_a0, rwkv_a2, rwkv_g2,
                  rwkv_k_k, rwkv_k_a, rwkv_r_k, rwkv_ln_g, rwkv_ln_b, seg):
    perm = _rope_partner_perm()
    wq = mla_w_uq[l].reshape(MLA_Q_RANK, MLA_HEADS, MLA_NOPE + MLA_ROPE)
    wq = jnp.concatenate([wq, wq[:, :, MLA_NOPE:][:, :, perm]], axis=-1).reshape(MLA_Q_RANK, MLA_HEADS * MLA_QK_PAD)

    wd = RWKV_WIDTH
    z = lambda r: jnp.zeros((r, wd), F32)
    w2 = jnp.concatenate([
        jnp.concatenate([rwkv_w2[l, 0], z(DECAY_RANK), z(256 - 2 * DECAY_RANK)], axis=0),
        jnp.concatenate([z(DECAY_RANK), rwkv_w2[l, 1], z(256 - 2 * DECAY_RANK)], axis=0)], axis=1)
    lead = 256 - 2 * ICL_RANK
    a2 = jnp.concatenate([
        jnp.concatenate([z(lead), rwkv_a2[l, 0], z(ICL_RANK)], axis=0),
        jnp.concatenate([z(lead), z(ICL_RANK), rwkv_a2[l, 1]], axis=0)], axis=1)
    rw = {
        "mu": rwkv_mu[l].reshape(1, RWKV_COLS),
        "w0": rwkv_w0[l].reshape(1, 2 * wd),
        "a0": rwkv_a0[l].reshape(1, 2 * wd),
        "w2": w2.astype(BF16),
        "a2": a2.astype(BF16),
        "g2": rwkv_g2[l].astype(BF16),
        "k_k": rwkv_k_k[l].reshape(1, wd),
        "k_a": rwkv_k_a[l].reshape(1, wd),
        "r_k": rwkv_r_k[l].reshape(1, wd),
        "ln_g": rwkv_ln_g[l].reshape(1, wd),
        "ln_b": rwkv_ln_b[l].reshape(1, wd),
        "seg": seg,
    }
    return wq.astype(BF16), mla_w_ukv[l].astype(BF16), rw


def kernel(x, c, ctx, c_ctx, ada_w, ada_b, norm1_g, norm2_g, w_in, mla_q_norm_g, mla_w_uq, mla_kv_norm_g, mla_w_ukv, rwkv_mu, rwkv_w0, rwkv_w2, rwkv_a0, rwkv_a2, rwkv_g2, rwkv_k_k, rwkv_k_a, rwkv_r_k, rwkv_ln_g, rwkv_ln_b, pool_w, pool_scale, conv_w, w_out, mlp_w1, mlp_w2, final_norm_g):
    nb, n_seq, d = x.shape
    n_ctx = ctx.shape[1]
    depth = ada_w.shape[0]
    assert d == D_MODEL and nb < MOD_ROWS
    assert n_ctx % ROW_TILE == 0 and n_seq % ROW_TILE == 0 and n_seq % GRID_W == 0
    nct = n_ctx // ROW_TILE

    cc = jnp.zeros((MOD_ROWS, D_MODEL), F32).at[:nb].set(c).at[nb].set(c_ctx)
    mod = _modulation(cc, ada_w, ada_b)
    cs, sn = _rope_tables(n_ctx, n_seq)
    head = np.arange(RWKV_WIDTH) // RWKV_HEAD
    seg = jnp.asarray(head[:, None] == head[None, :], dtype=BF16)

    w_in_p = _in_proj_weights(w_in)
    h_parts = (ctx, x)
    for l in range(depth):
        last = l == depth - 1
        with_ctx = not last
        wq, wkv, rw_p = _layer_params(
            l, mla_w_uq, mla_w_ukv, rwkv_mu, rwkv_w0, rwkv_w2, rwkv_a0, rwkv_a2, rwkv_g2,
            rwkv_k_k, rwkv_k_a, rwkv_r_k, rwkv_ln_g, rwkv_ln_b, seg)

        (q, k, v, pack, pool, conv), (w_out_b, w1_b, w2_b) = _front(
            h_parts, l, mod[l], norm1_g[l], w_in_p, cs, sn, mla_q_norm_g[l], mla_kv_norm_g[l], wq, wkv, rw_p,
            pool_w[l].astype(BF16), pool_scale[l].reshape(1, POOL_WIDTH), conv_w[l],
            (w_out, mlp_w1, mlp_w2), nb, nct, n_ctx)
        att = _attention(q, k, v, n_ctx, with_ctx)
        y_f, y_b = _wkv(pack, nct)
        h_mid, xn_mid = _mix_out(att, y_f, y_b, pack, pool, conv, h_parts, mod[l], w_out_b,
                                 norm2_g[l].reshape(1, D_MODEL), rw_p, nb, nct, with_ctx)

        n_rows = h_mid.shape[1]
        if with_ctx:
            tm = 3 * ROW_TILE if n_rows % (3 * ROW_TILE) == 0 else ROW_TILE
            ctx_rows = n_ctx
            assert n_ctx <= tm, "context longer than one MLP row tile"
        else:
            tm = 4 * ROW_TILE if n_rows % (4 * ROW_TILE) == 0 else ROW_TILE
            ctx_rows = 0
        h_parts = (_mlp(h_mid, xn_mid, mod[l], w1_b, w2_b, final_norm_g, nb, tm, ctx_rows, last),)
    return h_parts[0]
```

```python
import functools

import numpy as np
import jax
import jax.numpy as jnp
from jax import lax
from jax.experimental import pallas as pl
from jax.experimental.pallas import tpu as pltpu

F32 = jnp.float32
BF16 = jnp.bfloat16

D_MODEL = 2048
GRID_W = 64
NORM_EPS = 1e-6
GROUP_WIDTH = D_MODEL // 4

MLA_NOPE = 128
MLA_ROPE = 64
MLA_V = 128
MLA_HEADS = GROUP_WIDTH // MLA_V
MLA_Q_RANK = 384
MLA_KV_RANK = 128
ROPE_BASE = 10000.0
MLA_QK_PAD = 256
MLA_FEAT = 640

RWKV_HEAD = 64
RWKV_HEADS = GROUP_WIDTH // RWKV_HEAD
RWKV_WIDTH = RWKV_HEADS * RWKV_HEAD
DECAY_RANK = 96
ICL_RANK = 96
GATE_RANK = 256
RWKV_GN_EPS = 64e-5
RWKV_COLS = 3 * RWKV_WIDTH + 2 * DECAY_RANK + 2 * ICL_RANK + GATE_RANK
WKV_CHUNK = 64
RWKV_PACK_BLOCKS = 11

POOL_WINDOWS = (2, 4, 8, 16)
POOL_WIDTH = GROUP_WIDTH
POOL_GROUP = POOL_WIDTH // len(POOL_WINDOWS)
CONV_WIDTH = GROUP_WIDTH
CONV_COLS = 3 * CONV_WIDTH
FF_HIDDEN = 4 * D_MODEL
MLA_COLS = MLA_Q_RANK + MLA_KV_RANK + MLA_ROPE

ROW_TILE = 256
HALO = 8
MOD_ROWS = 16
VMEM_LIMIT = 62 * 1024 * 1024


def _cparams(sem):
    return pltpu.CompilerParams(dimension_semantics=sem, vmem_limit_bytes=VMEM_LIMIT)


def _resident(shape, index_map):
    return pl.BlockSpec(shape, index_map, pipeline_mode=pl.Buffered(1))


def _rms(x, g):
    return x * lax.rsqrt(jnp.mean(x * x, axis=-1, keepdims=True) + NORM_EPS) * g


def _dot(a, b):
    return jnp.dot(a.astype(BF16), b.astype(BF16), preferred_element_type=F32)


def _dot_nt(a, b):
    return lax.dot_general(a.astype(BF16), b.astype(BF16), (((1,), (1,)), ((), ())), preferred_element_type=F32)


def _split3(x):
    h1 = x.astype(BF16)
    r1 = x - h1.astype(F32)
    h2 = r1.astype(BF16)
    h3 = (r1 - h2.astype(F32)).astype(BF16)
    return h1, h2, h3


def _dot_exact_lhs(a01, x):
    return jnp.dot(jnp.concatenate([a01] * 3, axis=1), jnp.concatenate(_split3(x), axis=0), preferred_element_type=F32)


def _dot_x2(a, b):
    a1 = a.astype(BF16)
    b1 = b.astype(BF16)
    b2 = (b - b1.astype(F32)).astype(BF16)
    return jnp.dot(jnp.concatenate([a1, a1], axis=1), jnp.concatenate([b1, b2], axis=0), preferred_element_type=F32)


def _mod_kernel(c_ref, w_ref, b_ref, o_ref):
    c = c_ref[...]
    s = c * jax.nn.sigmoid(c)
    o_ref[...] = _dot(s, w_ref[...]) + b_ref[...]


def _modulation(cc, ada_w, ada_b):
    depth = ada_w.shape[0]
    tn = 1024
    return pl.pallas_call(
        _mod_kernel,
        grid=(depth, 6 * D_MODEL // tn),
        in_specs=[
            pl.BlockSpec((MOD_ROWS, D_MODEL), lambda l, j: (0, 0)),
            pl.BlockSpec((None, D_MODEL, tn), lambda l, j: (l, 0, j)),
            pl.BlockSpec((None, 1, tn), lambda l, j: (l, 0, j)),
        ],
        out_specs=pl.BlockSpec((None, MOD_ROWS, tn), lambda l, j: (l, 0, j)),
        out_shape=jax.ShapeDtypeStruct((depth, MOD_ROWS, 6 * D_MODEL), F32),
        compiler_params=_cparams(("parallel", "parallel")),
        name="modulation",
    )(cc, ada_w, ada_b.reshape(depth, 1, 6 * D_MODEL))


def _mod_row(mod_ref, row, k):
    return mod_ref[pl.ds(row, 1), k * D_MODEL:(k + 1) * D_MODEL]


def _halo_specs(width, n_rows, tile0=0):
    per = ROW_TILE // HALO
    last_tile = n_rows // ROW_TILE - 1
    last = n_rows // HALO - 1
    clamp = lambda i, hi: jnp.clip(i, 0, hi)
    cur = pl.BlockSpec((None, ROW_TILE, width), lambda b, t: (b, clamp(t - tile0, last_tile), 0))
    prev = pl.BlockSpec((None, HALO, width), lambda b, t: (b, clamp((t - tile0) * per - 1, last), 0))
    nxt = pl.BlockSpec((None, HALO, width), lambda b, t: (b, clamp((t - tile0 + 1) * per, last), 0))
    return cur, prev, nxt


def _token_tile_spec(n_rows, tile0, t_off):
    last_tile = n_rows // ROW_TILE - 1
    return pl.BlockSpec((None, ROW_TILE, D_MODEL), lambda b, t: (b, jnp.clip(t + t_off - tile0, 0, last_tile), 0))


def _seq_edges(t, nct, nt):
    has_prev = jnp.logical_and(t != 0, t != nct)
    has_next = jnp.logical_and(t != nct - 1, t != nt - 1)
    return has_prev, has_next


def _seg_sum(x, e_ref):
    h1 = x.astype(BF16)
    h2 = (x - h1.astype(F32)).astype(BF16)
    e = e_ref[...]
    return jnp.dot(h1, e, preferred_element_type=F32) + jnp.dot(h2, e, preferred_element_type=F32)


IN_SEGMENTS = (MLA_FEAT, RWKV_COLS, POOL_WIDTH, CONV_COLS)
HALO_ROWS = ROW_TILE + 2 * HALO
LO, HI = HALO, HALO + ROW_TILE


def _rope(x, cs, sn):
    return x * cs + pltpu.roll(x, 64, 1) * sn


def _store_with_halo(buf, cols, res):
    buf[LO:HI, cols] = res[0:ROW_TILE]
    buf[0:LO, cols] = res[ROW_TILE:ROW_TILE + HALO]
    buf[HI:HI + HALO, cols] = res[ROW_TILE + HALO:ROW_TILE + 2 * HALO]


def _mla_qkv(f, cs, sn, gq_ref, gkv_ref, wq_ref, wkv_ref, q_ref, k_ref, v_ref):
    scale = (MLA_NOPE + MLA_ROPE) ** -0.5
    q = _dot(_rms(f[:, 0:MLA_Q_RANK], gq_ref[...]), wq_ref[...])
    kv = _dot(_rms(f[:, MLA_Q_RANK:MLA_Q_RANK + MLA_KV_RANK], gkv_ref[...]), wkv_ref[...])
    k_pe = _rope(f[:, 512:640], cs, sn).astype(BF16)
    for h in range(MLA_HEADS):
        o = h * MLA_QK_PAD
        q_ref[:, o:o + 128] = (q[:, o:o + 128] * scale).astype(BF16)
        q_ref[:, o + 128:o + 256] = (_rope(q[:, o + 128:o + 256], cs, sn) * scale).astype(BF16)
        k_ref[:, o:o + 128] = kv[:, o:o + 128].astype(BF16)
        k_ref[:, o + 128:o + 256] = k_pe
        v_ref[:, h * MLA_V:(h + 1) * MLA_V] = kv[:, o + 128:o + 256].astype(BF16)


def _rwkv_features(buf, mu_ref, w0_ref, a0_ref, w2_ref, a2_ref, g2_ref, kk_ref, ka_ref, rk_ref, e_ref, o_ref):
    w = RWKV_WIDTH

    def shifted(cols):
        f = buf[LO:HI, cols]
        return f + mu_ref[:, cols] * (0.5 * (buf[LO - 1:HI - 1, cols] + buf[LO + 1:HI + 1, cols]) - f)

    def put(i, val):
        o_ref[:, i * w:(i + 1) * w] = val

    xw = w0_ref[...] + _dot(jnp.tanh(shifted(slice(1536, 1792))), w2_ref[...])
    lw = -float(np.exp(-0.5)) * jax.nn.sigmoid(xw)
    put(5, lw[:, 0:w])
    put(8, lw[:, w:2 * w])
    yield
    a = jax.nn.sigmoid(a0_ref[...] + _dot(shifted(slice(1664, 1920)), a2_ref[...]))
    put(7, a[:, 0:w])
    put(10, a[:, w:2 * w])
    put(3, _dot(jax.nn.sigmoid(shifted(slice(1920, 2176))), g2_ref[...]))
    yield
    k = shifted(slice(w, 2 * w))
    kkv = k * kk_ref[...]
    put(2, kkv / jnp.maximum(jnp.sqrt(_seg_sum(kkv * kkv, e_ref)), 1e-12))
    yield
    r = shifted(slice(0, w))
    put(0, r)
    rk = r * rk_ref[...]
    ka = ka_ref[...]
    k_f = k * (1.0 + (a[:, 0:w] - 1.0) * ka)
    k_b = k * (1.0 + (a[:, w:2 * w] - 1.0) * ka)
    put(6, k_f)
    put(9, k_b)
    yield
    v = shifted(slice(2 * w, 3 * w))
    put(1, v)
    put(4, _seg_sum(rk * k_f + rk * k_b, e_ref) * v)


def _pool_mixer(buf, pw_ref, ps_ref, o_ref, pos, n_own):
    for gi, win in enumerate(POOL_WINDOWS):
        hw = win // 2
        cols = slice(gi * POOL_GROUP, (gi + 1) * POOL_GROUP)
        s = buf[LO - hw:HI - hw, cols]
        for o in range(-hw + 1, hw):
            s = s + buf[LO + o:HI + o, cols]
        cnt = (jnp.minimum(pos + hw, n_own) - jnp.maximum(pos - hw, 0)).astype(F32)
        z = s / cnt - buf[LO:HI, cols]
        o_ref[:, cols] = (_dot(z, pw_ref[gi]) * ps_ref[:, cols]).astype(o_ref.dtype)


def _pick_rows(refs, in_ctx):
    if len(refs) == 3:
        return [r[...] for r in refs]
    return [jnp.where(in_ctx, c[...], x[...]) for c, x in zip(refs[0:3], refs[3:6])]


FRONT_FIXED_INPUTS = 22


def _front_kernel(*refs, nparts, ncast, nb, nct, nt, n_ctx, n_seq):
    x_refs, refs = refs[:3 * nparts], refs[3 * nparts:]
    (mod_ref, g_ref, w_ref, cs_ref, sn_ref, gq_ref, gkv_ref, wq_ref, wkv_ref,
     mu_ref, w0_ref, a0_ref, w2_ref, a2_ref, g2_ref, kk_ref, ka_ref, rk_ref, e_ref,
     pw_ref, ps_ref, cw_ref) = refs[:FRONT_FIXED_INPUTS]
    cast_in, refs = refs[FRONT_FIXED_INPUTS:FRONT_FIXED_INPUTS + ncast], refs[FRONT_FIXED_INPUTS + ncast:]
    q_ref, k_ref, v_ref, pack_ref, pool_ref, conv_ref = refs[:6]
    cast_out = refs[6:6 + ncast]
    rw_buf, pool_buf, u_buf = refs[6 + ncast:]
    for src, dst in zip(cast_in, cast_out):
        dst[...] = src[...].astype(dst.dtype)
    b = pl.program_id(0)
    t = pl.program_id(1)
    in_ctx = t < nct
    row = jnp.where(in_ctx, nb, b)
    has_prev, has_next = _seq_edges(t, nct, nt)
    shift, scale = _mod_row(mod_ref, row, 0), 1.0 + _mod_row(mod_ref, row, 1)
    norm = lambda x: _rms(x, g_ref[...]) * scale + shift
    x_cur, x_prev, x_next = _pick_rows(x_refs, in_ctx)
    xn = jnp.concatenate([norm(x_cur), jnp.where(has_prev, norm(x_prev), 0.0), jnp.where(has_next, norm(x_next), 0.0)],
                         axis=0).astype(BF16)

    mla0 = 0
    rw0 = IN_SEGMENTS[0]
    pool0 = rw0 + RWKV_COLS
    conv0 = pool0 + POOL_WIDTH
    cw = CONV_WIDTH
    proj = lambda rows, c0, c1: lax.dot_general(rows, w_ref[c0:c1, :], (((1,), (1,)), ((), ())),
                                                preferred_element_type=F32)

    c0 = 0
    while c0 < RWKV_COLS:
        c1 = min(c0 + 512, RWKV_COLS)
        _store_with_halo(rw_buf, slice(c0, c1), proj(xn, rw0 + c0, rw0 + c1))
        c0 = c1
    stages = _rwkv_features(rw_buf, mu_ref, w0_ref, a0_ref, w2_ref, a2_ref, g2_ref, kk_ref, ka_ref, rk_ref, e_ref, pack_ref)
    next(stages)
    f_mla = proj(xn[0:ROW_TILE], mla0, rw0)
    _store_with_halo(pool_buf, slice(None), proj(xn, pool0, conv0))
    next(stages)
    gb = proj(xn[0:ROW_TILE], conv0, conv0 + cw)
    _mla_qkv(f_mla, cs_ref[...], sn_ref[...], gq_ref, gkv_ref, wq_ref, wkv_ref, q_ref, k_ref, v_ref)
    next(stages)
    gc = proj(xn, conv0 + cw, conv0 + 2 * cw)
    pos = lax.broadcasted_iota(jnp.int32, (ROW_TILE, 1), 0) + jnp.where(in_ctx, t, t - nct) * ROW_TILE
    _pool_mixer(pool_buf, pw_ref, ps_ref, pool_ref, pos, jnp.where(in_ctx, n_ctx, n_seq))
    next(stages)
    hx = proj(xn, conv0 + 2 * cw, conv0 + 3 * cw)
    for _ in stages:
        pass
    _store_with_halo(u_buf, slice(None), gc * hx)
    z = cw_ref[0:1, :] * u_buf[LO - 1:HI - 1] + cw_ref[1:2, :] * u_buf[LO:HI] + cw_ref[2:3, :] * u_buf[LO + 1:HI + 1]
    conv_ref[...] = (gb * z).astype(conv_ref.dtype)


def _layer_spec(a, l):
    return _resident((None,) + a.shape[1:], lambda *_: (l,) + (0,) * (a.ndim - 1))


def _cast_slab_spec(w, l, b_, nt):
    _, rows, cols = w.shape
    k = 1
    while k * 2 <= nt and cols % (k * 2 * 128) == 0:
        k *= 2
    assert rows % (b_ * 8) == 0
    shape = (rows // b_, cols // k)
    col = lambda t: jnp.minimum(t, k - 1)
    return (pl.BlockSpec((None,) + shape, lambda b, t: (l, b, col(t))),
            pl.BlockSpec(shape, lambda b, t: (b, col(t))),
            jax.ShapeDtypeStruct((rows, cols), BF16))


def _front(h_parts, l, mod, g, w_p, cs, sn, gq, gkv, wq, wkv, p, pool_w, pool_scale, conv_w, cast_ws, nb, nct, n_ctx):
    b_ = h_parts[0].shape[0]
    n_ = sum(h.shape[1] for h in h_parts)
    nt = n_ // ROW_TILE
    casts = [_cast_slab_spec(w, l, b_, nt) for w in cast_ws]
    qk_w = MLA_HEADS * MLA_QK_PAD
    pack_w = RWKV_PACK_BLOCKS * RWKV_WIDTH
    full = lambda a: _resident(a.shape, lambda b, t: (0,) * a.ndim)
    row = lambda w: pl.BlockSpec((None, ROW_TILE, w), lambda b, t: (b, t, 0))
    tab = pl.BlockSpec((ROW_TILE, 128), lambda b, t: (t, 0))
    x_specs, x_args, tile0 = [], [], 0
    for h in h_parts:
        x_specs += _halo_specs(D_MODEL, h.shape[1], tile0)
        x_args += [h, h, h]
        tile0 += h.shape[1] // ROW_TILE
    consts = (mod, g.reshape(1, D_MODEL))
    mla = (gq.reshape(1, -1), gkv.reshape(1, -1), wq, wkv)
    rwk = (p["mu"], p["w0"], p["a0"], p["w2"], p["a2"], p["g2"], p["k_k"], p["k_a"], p["r_k"], p["seg"])
    mix = (pool_w, pool_scale, conv_w)
    fixed = [full(a) for a in consts] + [_layer_spec(w_p, l), tab, tab] + [full(a) for a in mla + rwk + mix]
    assert len(fixed) == FRONT_FIXED_INPUTS
    outs = pl.pallas_call(
        functools.partial(_front_kernel, nparts=len(h_parts), ncast=len(casts), nb=nb, nct=nct, nt=nt,
                          n_ctx=n_ctx, n_seq=n_ - n_ctx),
        grid=(b_, nt),
        in_specs=x_specs + fixed + [c[0] for c in casts],
        out_specs=(row(qk_w), row(qk_w), row(MLA_HEADS * MLA_V), row(pack_w), row(POOL_WIDTH), row(CONV_WIDTH))
                  + tuple(c[1] for c in casts),
        out_shape=(
            jax.ShapeDtypeStruct((b_, n_, qk_w), BF16),
            jax.ShapeDtypeStruct((b_, n_, qk_w), BF16),
            jax.ShapeDtypeStruct((b_, n_, MLA_HEADS * MLA_V), BF16),
            jax.ShapeDtypeStruct((b_, n_, pack_w), F32),
            jax.ShapeDtypeStruct((b_, n_, POOL_WIDTH), BF16),
            jax.ShapeDtypeStruct((b_, n_, CONV_WIDTH), BF16),
        ) + tuple(c[2] for c in casts),
        scratch_shapes=[pltpu.VMEM((HALO_ROWS, RWKV_COLS), F32), pltpu.VMEM((HALO_ROWS, POOL_WIDTH), F32),
                        pltpu.VMEM((HALO_ROWS, CONV_WIDTH), F32)],
        compiler_params=_cparams(("parallel", "arbitrary")),
        name="front",
    )(*x_args, *consts, w_p, cs, sn, *mla, *rwk, *mix, *cast_ws)
    return outs[:6], outs[6:]


ATTN_Q_TILE = 512
ATTN_SOFTMAX_ROWS = 128


def _attn_kernel(q_ref, k_ref, v_ref, o_ref, *, n_ctx, n_all, tq, with_ctx):
    s = pl.program_id(1)

    def attend(q_rows, o_rows, nk, nq):
        for h in range(MLA_HEADS):
            q = q_ref[q_rows, h * MLA_QK_PAD:(h + 1) * MLA_QK_PAD]
            k = k_ref[0:nk, h * MLA_QK_PAD:(h + 1) * MLA_QK_PAD]
            sc = lax.dot_general(q, k, (((1,), (1,)), ((), ())), preferred_element_type=F32)
            ps, ls = [], []
            for r0 in range(0, nq, ATTN_SOFTMAX_ROWS):
                blk = sc[r0:r0 + ATTN_SOFTMAX_ROWS]
                p = jnp.exp(blk - jnp.max(blk, axis=-1, keepdims=True))
                ls.append(jnp.sum(p, axis=-1, keepdims=True))
                ps.append(p.astype(BF16))
            p = jnp.concatenate(ps, axis=0)
            l = jnp.concatenate(ls, axis=0)
            o = jnp.dot(p, v_ref[0:nk, h * MLA_V:(h + 1) * MLA_V], preferred_element_type=F32)
            o_ref[o_rows, h * MLA_V:(h + 1) * MLA_V] = (o / l).astype(o_ref.dtype)

    if with_ctx:
        @pl.when(s == 0)
        def _():
            attend(slice(0, n_ctx), slice(0, n_ctx), n_ctx, n_ctx)

        @pl.when(s > 0)
        def _():
            r0 = pl.multiple_of(n_ctx + (s - 1) * tq, ROW_TILE)
            attend(pl.ds(r0, tq), pl.ds(r0, tq), n_all, tq)
    else:
        attend(pl.ds(pl.multiple_of(n_ctx + s * tq, ROW_TILE), tq), pl.ds(pl.multiple_of(s * tq, ROW_TILE), tq), n_all, tq)


def _attention(q, k, v, n_ctx, with_ctx):
    b_, n_, qk_w = q.shape
    n_seq = n_ - n_ctx
    tq = ATTN_Q_TILE if n_seq % ATTN_Q_TILE == 0 else ROW_TILE
    n_out = n_ if with_ctx else n_seq
    whole = lambda rows, w: pl.BlockSpec((None, rows, w), lambda b, s: (b, 0, 0))
    return pl.pallas_call(
        functools.partial(_attn_kernel, n_ctx=n_ctx, n_all=n_, tq=tq, with_ctx=with_ctx),
        grid=(b_, n_seq // tq + (1 if with_ctx else 0)),
        in_specs=[whole(n_, qk_w), whole(n_, qk_w), whole(n_, MLA_HEADS * MLA_V)],
        out_specs=whole(n_out, MLA_HEADS * MLA_V),
        out_shape=jax.ShapeDtypeStruct((b_, n_out, MLA_HEADS * MLA_V), BF16),
        compiler_params=_cparams(("parallel", "arbitrary")),
        name="mla_attention",
    )(q, k, v)


def _bd(x, m0):
    return jnp.concatenate([jnp.where(m0, x, 0.0), jnp.where(m0, 0.0, x)], axis=0)


def _wkv_stages(r_ref, v_ref, kk_ref, lw_ref, kd_ref, a_ref, y_ref, s_ref, rw_s, y0_s, m_s, n_s, rev):
    c_ = WKV_CHUNK
    nsub = ROW_TILE // c_
    npair = RWKV_WIDTH // 128

    ri = lax.broadcasted_iota(jnp.int32, (c_, c_), 0)
    ci = lax.broadcasted_iota(jnp.int32, (c_, c_), 1)
    tmat = ((ri <= ci) if rev else (ri >= ci)).astype(BF16)
    rc = lax.broadcasted_iota(jnp.int32, (c_, 128), 0)
    cc = lax.broadcasted_iota(jnp.int32, (c_, 128), 1) % c_
    strict = (rc < cc) if rev else (rc > cc)
    incl = (rc <= cc) if rev else (rc >= cc)
    eye = rc == cc
    m0_128 = lax.broadcasted_iota(jnp.int32, (1, 128), 1) < c_
    m0_256 = (lax.broadcasted_iota(jnp.int32, (1, 256), 1) % 128) < c_
    zeros_cv = jnp.zeros((c_, 128), F32)

    items = [(c, p) for c in range(nsub) for p in range(npair)]
    at, rt, bt, kt, bh, kh, vv, e_tot = {}, {}, {}, {}, {}, {}, {}, {}
    for c in range(nsub):
        rows = slice(c * c_, (c + 1) * c_)
        lw = lw_ref[rows, :]
        lc = _dot_exact_lhs(tmat, lw)
        ltot = jnp.sum(lw, axis=0, keepdims=True)
        e_neg = jnp.exp(-lc)
        e_end = jnp.exp(ltot - lc)
        kk = kk_ref[rows, :]
        kd = kd_ref[rows, :]
        bv = kk * a_ref[rows, :]
        vv[c] = v_ref[rows, :]
        at[c] = -kk * jnp.exp(lc - lw)
        rt[c] = r_ref[rows, :] * jnp.exp(lc)
        bt[c], kt[c] = bv * e_neg, kd * e_neg
        bh[c], kh[c] = bv * e_end, kd * e_end
        e_tot[c] = jnp.exp(ltot)
        yield
    sl = lambda p: slice(p * 128, (p + 1) * 128)

    g = {}
    for c, p in items:
        ar = jnp.concatenate([at[c][:, sl(p)], rt[c][:, sl(p)]], axis=0)
        bd = jnp.concatenate([_bd(bt[c][:, sl(p)], m0_128), _bd(kt[c][:, sl(p)], m0_128)], axis=0)
        g[c, p] = _dot_nt(ar, bd)
    yield
    av, x, pw, a_rb = {}, {}, {}, {}
    for c, p in items:
        a_ak = jnp.where(strict, g[c, p][0:c_, 128:256], 0.0)
        a_rk = jnp.where(incl, g[c, p][c_:2 * c_, 128:256], 0.0)
        av[c, p] = _dot(jnp.concatenate([a_ak, a_rk], axis=0), _bd(vv[c][:, sl(p)], m0_128))
        pw[c, p] = jnp.where(strict, g[c, p][0:c_, 0:128], 0.0)
        a_rb[c, p] = jnp.where(incl, g[c, p][c_:2 * c_, 0:128], 0.0)
    for c, p in items:
        x[c, p] = jnp.concatenate([at[c][:, sl(p)], av[c, p][0:c_]], axis=1)
    yield
    for it in range(6):
        dot = _dot_x2 if it < 2 else _dot
        for c, p in items:
            x[c, p] = x[c, p] + dot(pw[c, p], _bd(x[c, p], m0_256))
        yield
        if it < 5:
            for c, p in items:
                pw[c, p] = _dot(pw[c, p], _bd(pw[c, p], m0_128))
            yield
    for c, p in items:
        ry = _dot(a_rb[c, p], _bd(x[c, p], m0_256))
        rw_s[c, p] = rt[c][:, sl(p)] + ry[:, 0:128]
        y0_s[c, p] = ry[:, 128:256] + av[c, p][c_:2 * c_]
    yield
    for c, p in items:
        bkt = jnp.concatenate([bh[c][:, sl(p)], kh[c][:, sl(p)]], axis=0).T
        rhs = jnp.concatenate([x[c, p], jnp.concatenate([zeros_cv, vv[c][:, sl(p)]], axis=1)], axis=0)
        z = _dot(bkt, rhs)
        m_s[c, p] = (jnp.where(m0_128, z[0:c_, 0:128], z[c_:2 * c_, 0:128])
                     + jnp.where(eye, e_tot[c][:, sl(p)], 0.0))
        n_s[c, p] = jnp.where(m0_128, z[0:c_, 128:256], z[c_:2 * c_, 128:256])
    yield

    for i in range(nsub):
        c = (nsub - 1 - i) if rev else i
        for p in range(npair):
            lhs = jnp.concatenate([rw_s[c, p], m_s[c, p]], axis=0)
            o = _dot_x2(lhs, _bd(s_ref[p], m0_128))
            y_ref[c * c_:(c + 1) * c_, sl(p)] = o[0:c_] + y0_s[c, p]
            s_ref[p] = o[c_:2 * c_] + n_s[c, p]
        yield


WKV_LEAD_STAGES = 9


def _wkv_kernel(*refs):
    ins_f, ins_b, (y_f, y_b), scr_f, scr_b = refs[0:6], refs[6:12], refs[12:14], refs[14:19], refs[19:24]

    @pl.when(pl.program_id(1) == 0)
    def _():
        scr_f[0][...] = jnp.zeros_like(scr_f[0])
        scr_b[0][...] = jnp.zeros_like(scr_b[0])

    fwd = _wkv_stages(*ins_f, y_f, *scr_f, rev=False)
    bwd = _wkv_stages(*ins_b, y_b, *scr_b, rev=True)
    for _ in range(WKV_LEAD_STAGES):
        next(fwd)
    live = [fwd, bwd]
    while live:
        for gen in list(live):
            if next(gen, StopIteration) is StopIteration:
                live.remove(gen)


def _wkv(pack, nct):
    b_, n_, _ = pack.shape
    nt = n_ // ROW_TILE
    w = RWKV_WIDTH
    nsub = ROW_TILE // WKV_CHUNK
    npair = w // 128
    fwd_tile = lambda s: s
    bwd_tile = lambda s: jnp.where(s < nct, nct - 1 - s, nt + nct - 1 - s)
    col = lambda order, j: pl.BlockSpec((None, ROW_TILE, w), lambda b, s: (b, order(s), j))
    out = lambda order: pl.BlockSpec((None, ROW_TILE, w), lambda b, s: (b, order(s), 0))
    scr = lambda: [pltpu.VMEM((npair, WKV_CHUNK, 128), F32)] + [pltpu.VMEM((nsub, npair, WKV_CHUNK, 128), F32)] * 4
    y = jax.ShapeDtypeStruct((b_, n_, w), F32)
    return pl.pallas_call(
        _wkv_kernel,
        grid=(b_, nt),
        in_specs=[col(fwd_tile, j) for j in (0, 1, 2, 5, 6, 7)] + [col(bwd_tile, j) for j in (0, 1, 2, 8, 9, 10)],
        out_specs=(out(fwd_tile), out(bwd_tile)),
        out_shape=(y, y),
        scratch_shapes=scr() + scr(),
        compiler_params=_cparams(("parallel", "arbitrary")),
        name="wkv",
    )(*([pack] * 12))


def _mix_out_kernel(att_ref, yf_ref, yb_ref, g_ref, bonus_ref, lg_ref, lb_ref, e_ref, pool_ref, conv_ref,
                    mod_ref, w_ref, g2_ref, *refs, nb, nct, t_off):
    h_refs, (o_ref, xn_ref) = refs[:-2], refs[-2:]
    b = pl.program_id(0)
    t = pl.program_id(1) + t_off
    in_ctx = t < nct
    row = jnp.where(in_ctx, nb, b)
    h = h_refs[0][...] if len(h_refs) == 1 else jnp.where(in_ctx, h_refs[0][...], h_refs[1][...])
    y = yf_ref[...] + yb_ref[...]
    mean = _seg_sum(y, e_ref) * (1.0 / RWKV_HEAD)
    d = y - mean
    var = _seg_sum(d * d, e_ref) * (1.0 / RWKV_HEAD)
    yn = d * lax.rsqrt(var + RWKV_GN_EPS) * lg_ref[...] + lb_ref[...]
    rw = ((yn + bonus_ref[...]) * g_ref[...]).astype(BF16)
    mixed = jnp.concatenate([att_ref[...], rw, pool_ref[...], conv_ref[...]], axis=1)
    acc = jnp.dot(mixed, w_ref[...], preferred_element_type=F32)
    h = h + _mod_row(mod_ref, row, 2) * acc
    o_ref[...] = h
    xn_ref[...] = (_rms(h, g2_ref[...]) * (1.0 + _mod_row(mod_ref, row, 4)) + _mod_row(mod_ref, row, 3)).astype(BF16)


def _mix_out(att, yf, yb, pack, pool, conv, h_parts, mod, w_out, g2, p, nb, nct, with_ctx):
    b_, n_, gw = yf.shape
    t_off = 0 if with_ctx else nct
    n_out = n_ - t_off * ROW_TILE
    blk = lambda j: pl.BlockSpec((None, ROW_TILE, gw), lambda b, t: (b, t + t_off, j))
    full = lambda a: _resident(a.shape, lambda b, t: (0,) * a.ndim)
    h_specs, tile0 = [], 0
    for h in h_parts:
        h_specs.append(_token_tile_spec(h.shape[1], tile0, t_off))
        tile0 += h.shape[1] // ROW_TILE
    out = pl.BlockSpec((None, ROW_TILE, D_MODEL), lambda b, t: (b, t, 0))
    return pl.pallas_call(
        functools.partial(_mix_out_kernel, nb=nb, nct=nct, t_off=t_off),
        grid=(b_, n_out // ROW_TILE),
        in_specs=[pl.BlockSpec((None, ROW_TILE, gw), lambda b, t: (b, t, 0)),
                  blk(0), blk(0), blk(3), blk(4), full(p["ln_g"]), full(p["ln_b"]), full(p["seg"]),
                  blk(0), blk(0), full(mod), full(w_out), full(g2)] + h_specs,
        out_specs=(out, out),
        out_shape=(jax.ShapeDtypeStruct((b_, n_out, D_MODEL), F32), jax.ShapeDtypeStruct((b_, n_out, D_MODEL), BF16)),
        compiler_params=_cparams(("parallel", "parallel")),
        name="mix_out",
    )(att, yf, yb, pack, pack, p["ln_g"], p["ln_b"], p["seg"], pool, conv, mod, w_out, g2, *h_parts)


MLP_HIDDEN_TILE = 512


def _mlp_kernel(h_ref, xn_ref, mod_ref, w1_ref, w2_ref, gf_ref, o_ref, *, nb, ctx_rows, final):
    b = pl.program_id(0)
    t = pl.program_id(1)
    j = pl.program_id(2)

    def hidden_slice(first):
        hid = jnp.dot(xn_ref[...], w1_ref[...], preferred_element_type=F32)
        hid = jnp.square(jnp.maximum(hid, 0.0))
        part = jnp.dot(hid.astype(BF16), w2_ref[...], preferred_element_type=F32)
        o_ref[...] = part if first else o_ref[...] + part

    @pl.when(j == 0)
    def _():
        hidden_slice(True)

    @pl.when(j > 0)
    def _():
        hidden_slice(False)

    @pl.when(j == pl.num_programs(2) - 1)
    def _():
        def put(rows, mrow):
            o = h_ref[rows, :] + _mod_row(mod_ref, mrow, 5) * o_ref[rows, :]
            if final:
                o = _rms(o, gf_ref[...])
            o_ref[rows, :] = o

        first_row = jnp.where(t == 0, nb, b)
        if ctx_rows == 0:
            put(slice(None), b)
        elif ctx_rows == h_ref.shape[0]:
            put(slice(None), first_row)
        else:
            put(slice(0, ctx_rows), first_row)
            put(slice(ctx_rows, None), b)


def _mlp(h, xn, mod, w1, w2, g_final, nb, tm, ctx_rows, final):
    b_, n_, _ = h.shape
    th = MLP_HIDDEN_TILE * (2 if tm < 1024 else 1)
    tile = pl.BlockSpec((None, tm, D_MODEL), lambda b, t, j: (b, t, 0))
    return pl.pallas_call(
        functools.partial(_mlp_kernel, nb=nb, ctx_rows=ctx_rows, final=final),
        grid=(b_, n_ // tm, FF_HIDDEN // th),
        in_specs=[
            tile, tile,
            _resident((MOD_ROWS, 6 * D_MODEL), lambda b, t, j: (0, 0)),
            pl.BlockSpec((D_MODEL, th), lambda b, t, j: (0, j)),
            pl.BlockSpec((th, D_MODEL), lambda b, t, j: (j, 0)),
            _resident((1, D_MODEL), lambda b, t, j: (0, 0)),
        ],
        out_specs=tile,
        out_shape=jax.ShapeDtypeStruct((b_, n_, D_MODEL), F32),
        compiler_params=_cparams(("parallel", "parallel", "arbitrary")),
        name="mlp",
    )(h, xn, mod, w1, w2, g_final.reshape(1, D_MODEL))


def _rope_partner_perm():
    q = MLA_ROPE // 4
    i = np.arange(MLA_ROPE)
    return np.where((i // q) % 2 == 0, i + q, i - q)


def _rope_tables(n_ctx, n_seq):
    f32 = np.float32
    rows = n_seq // GRID_W
    row = np.repeat(np.arange(rows), GRID_W)
    col = np.tile(np.arange(GRID_W), rows)
    pos = np.stack([row, col], axis=-1).astype(f32)
    axis_dim = MLA_ROPE // 2
    inv_freq = (f32(ROPE_BASE) ** (-np.arange(0, axis_dim, 2, dtype=f32) / f32(axis_dim))).astype(f32)
    ang = (pos[:, :, None] * inv_freq).astype(f32)
    cos, sin = np.cos(ang), np.sin(ang)
    cos64 = np.concatenate([cos[:, 0], cos[:, 0], cos[:, 1], cos[:, 1]], axis=-1)
    sin64 = np.concatenate([-sin[:, 0], sin[:, 0], -sin[:, 1], sin[:, 1]], axis=-1)
    pad = np.zeros((n_seq, 128 - MLA_ROPE), f32)
    cs_lat = np.concatenate([cos64, pad], axis=-1)
    sn_lat = np.concatenate([sin64, pad], axis=-1)
    cs_ctx = np.concatenate([np.ones((n_ctx, MLA_ROPE), f32), np.zeros((n_ctx, 128 - MLA_ROPE), f32)], axis=-1)
    cs = np.concatenate([cs_ctx, cs_lat], axis=0).astype(f32)
    sn = np.concatenate([np.zeros((n_ctx, 128), f32), sn_lat], axis=0).astype(f32)
    return jnp.asarray(cs), jnp.asarray(sn)


W_IN_BLOCK = 256


def _in_proj_relayout_kernel(prev_ref, cur_ref, o_ref):
    j = pl.program_id(1)
    kr_block = (MLA_Q_RANK + MLA_KV_RANK) // W_IN_BLOCK
    kr = (MLA_Q_RANK + MLA_KV_RANK) % W_IN_BLOCK
    q = MLA_ROPE // 4

    @pl.when(j < kr_block)
    def _():
        o_ref[...] = cur_ref[...].astype(BF16)

    @pl.when(j == kr_block)
    def _():
        end = kr + MLA_ROPE
        o_ref[0:end] = cur_ref[0:end].astype(BF16)
        for i in range(4):
            src = kr + (i ^ 1) * q
            o_ref[end + i * q:end + (i + 1) * q] = cur_ref[src:src + q].astype(BF16)
        o_ref[end + MLA_ROPE:W_IN_BLOCK] = cur_ref[end:W_IN_BLOCK - MLA_ROPE].astype(BF16)

    @pl.when(j > kr_block)
    def _():
        o_ref[0:MLA_ROPE] = prev_ref[W_IN_BLOCK - MLA_ROPE:W_IN_BLOCK].astype(BF16)
        o_ref[MLA_ROPE:W_IN_BLOCK] = cur_ref[0:W_IN_BLOCK - MLA_ROPE].astype(BF16)


def _in_proj_weights(w_in):
    depth, d, n_in = w_in.shape
    n_out = n_in + MLA_ROPE
    assert n_out == sum(IN_SEGMENTS) and n_out % W_IN_BLOCK == 0
    w_t = jnp.swapaxes(w_in, 1, 2)
    last = (n_in - 1) // W_IN_BLOCK
    blk = lambda off: pl.BlockSpec((None, W_IN_BLOCK, d), lambda l, j: (l, jnp.clip(j + off, 0, last), 0))
    return pl.pallas_call(
        _in_proj_relayout_kernel,
        grid=(depth, n_out // W_IN_BLOCK),
        in_specs=[blk(-1), blk(0)],
        out_specs=pl.BlockSpec((None, W_IN_BLOCK, d), lambda l, j: (l, j, 0)),
        out_shape=jax.ShapeDtypeStruct((depth, n_out, d), BF16),
        compiler_params=_cparams(("parallel", "arbitrary")),
        name="in_proj_relayout",
    )(w_t, w_t)


def _layer_params(l, mla_w_uq, mla_w_ukv, rwkv_mu, rwkv_w0, rwkv_w2, rwkv_a0, rwkv_a2, rwkv_g2,
                  rwkv_k_k, rwkv_k_a, rwkv_r_k, rwkv_ln_g, rwkv_ln_b, seg):
    perm = _rope_partner_perm()
    wq = mla_w_uq[l].reshape(MLA_Q_RANK, MLA_HEADS, MLA_NOPE + MLA_ROPE)
    wq = jnp.concatenate([wq, wq[:, :, MLA_NOPE:][:, :, perm]], axis=-1).reshape(MLA_Q_RANK, MLA_HEADS * MLA_QK_PAD)

    wd = RWKV_WIDTH
    z = lambda r: jnp.zeros((r, wd), F32)
    w2 = jnp.concatenate([
        jnp.concatenate([rwkv_w2[l, 0], z(DECAY_RANK), z(256 - 2 * DECAY_RANK)], axis=0),
        jnp.concatenate([z(DECAY_RANK), rwkv_w2[l, 1], z(256 - 2 * DECAY_RANK)], axis=0)], axis=1)
    lead = 256 - 2 * ICL_RANK
    a2 = jnp.concatenate([
        jnp.concatenate([z(lead), rwkv_a2[l, 0], z(ICL_RANK)], axis=0),
        jnp.concatenate([z(lead), z(ICL_RANK), rwkv_a2[l, 1]], axis=0)], axis=1)
    rw = {
        "mu": rwkv_mu[l].reshape(1, RWKV_COLS),
        "w0": rwkv_w0[l].reshape(1, 2 * wd),
        "a0": rwkv_a0[l].reshape(1, 2 * wd),
        "w2": w2.astype(BF16),
        "a2": a2.astype(BF16),
        "g2": rwkv_g2[l].astype(BF16),
        "k_k": rwkv_k_k[l].reshape(1, wd),
        "k_a": rwkv_k_a[l].reshape(1, wd),
        "r_k": rwkv_r_k[l].reshape(1, wd),
        "ln_g": rwkv_ln_g[l].reshape(1, wd),
        "ln_b": rwkv_ln_b[l].reshape(1, wd),
        "seg": seg,
    }
    return wq.astype(BF16), mla_w_ukv[l].astype(BF16), rw


def kernel(x, c, ctx, c_ctx, ada_w, ada_b, norm1_g, norm2_g, w_in, mla_q_norm_g, mla_w_uq, mla_kv_norm_g, mla_w_ukv, rwkv_mu, rwkv_w0, rwkv_w2, rwkv_a0, rwkv_a2, rwkv_g2, rwkv_k_k, rwkv_k_a, rwkv_r_k, rwkv_ln_g, rwkv_ln_b, pool_w, pool_scale, conv_w, w_out, mlp_w1, mlp_w2, final_norm_g):
    nb, n_seq, d = x.shape
    n_ctx = ctx.shape[1]
    depth = ada_w.shape[0]
    assert d == D_MODEL and nb < MOD_ROWS
    assert n_ctx % ROW_TILE == 0 and n_seq % ROW_TILE == 0 and n_seq % GRID_W == 0
    nct = n_ctx // ROW_TILE

    cc = jnp.zeros((MOD_ROWS, D_MODEL), F32).at[:nb].set(c).at[nb].set(c_ctx)
    mod = _modulation(cc, ada_w, ada_b)
    cs, sn = _rope_tables(n_ctx, n_seq)
    head = np.arange(RWKV_WIDTH) // RWKV_HEAD
    seg = jnp.asarray(head[:, None] == head[None, :], dtype=BF16)

    w_in_p = _in_proj_weights(w_in)
    h_parts = (ctx, x)
    for l in range(depth):
        last = l == depth - 1
        with_ctx = not last
        wq, wkv, rw_p = _layer_params(
            l, mla_w_uq, mla_w_ukv, rwkv_mu, rwkv_w0, rwkv_w2, rwkv_a0, rwkv_a2, rwkv_g2,
            rwkv_k_k, rwkv_k_a, rwkv_r_k, rwkv_ln_g, rwkv_ln_b, seg)

        (q, k, v, pack, pool, conv), (w_out_b, w1_b, w2_b) = _front(
            h_parts, l, mod[l], norm1_g[l], w_in_p, cs, sn, mla_q_norm_g[l], mla_kv_norm_g[l], wq, wkv, rw_p,
            pool_w[l].astype(BF16), pool_scale[l].reshape(1, POOL_WIDTH), conv_w[l],
            (w_out, mlp_w1, mlp_w2), nb, nct, n_ctx)
        att = _attention(q, k, v, n_ctx, with_ctx)
        y_f, y_b = _wkv(pack, nct)
        h_mid, xn_mid = _mix_out(att, y_f, y_b, pack, pool, conv, h_parts, mod[l], w_out_b,
                                 norm2_g[l].reshape(1, D_MODEL), rw_p, nb, nct, with_ctx)

        n_rows = h_mid.shape[1]
        if with_ctx:
            tm = 3 * ROW_TILE if n_rows % (3 * ROW_TILE) == 0 else ROW_TILE
            ctx_rows = n_ctx
            assert n_ctx <= tm, "context longer than one MLP row tile"
        else:
            tm = 4 * ROW_TILE if n_rows % (4 * ROW_TILE) == 0 else ROW_TILE
            ctx_rows = 0
        h_parts = (_mlp(h_mid, xn_mid, mod[l], w1_b, w2_b, final_norm_g, nb, tm, ctx_rows, last),)
    return h_parts[0]
```

```python
import functools

import numpy as np
import jax
import jax.numpy as jnp
from jax import lax
from jax.experimental import pallas as pl
from jax.experimental.pallas import tpu as pltpu

F32 = jnp.float32
BF16 = jnp.bfloat16

D_MODEL = 2048
GRID_W = 64
NORM_EPS = 1e-6
GROUP_WIDTH = D_MODEL // 4

MLA_NOPE = 128
MLA_ROPE = 64
MLA_V = 128
MLA_HEADS = GROUP_WIDTH // MLA_V
MLA_Q_RANK = 384
MLA_KV_RANK = 128
ROPE_BASE = 10000.0
MLA_QK_PAD = 256
MLA_FEAT = 640

RWKV_HEAD = 64
RWKV_HEADS = GROUP_WIDTH // RWKV_HEAD
RWKV_WIDTH = RWKV_HEADS * RWKV_HEAD
DECAY_RANK = 96
ICL_RANK = 96
GATE_RANK = 256
RWKV_GN_EPS = 64e-5
RWKV_COLS = 3 * RWKV_WIDTH + 2 * DECAY_RANK + 2 * ICL_RANK + GATE_RANK
WKV_CHUNK = 64
RWKV_PACK_BLOCKS = 11

POOL_WINDOWS = (2, 4, 8, 16)
POOL_WIDTH = GROUP_WIDTH
POOL_GROUP = POOL_WIDTH // len(POOL_WINDOWS)
CONV_WIDTH = GROUP_WIDTH
CONV_COLS = 3 * CONV_WIDTH
FF_HIDDEN = 4 * D_MODEL
MLA_COLS = MLA_Q_RANK + MLA_KV_RANK + MLA_ROPE

ROW_TILE = 256
HALO = 8
MOD_ROWS = 16
VMEM_LIMIT = 62 * 1024 * 1024


def _cparams(sem):
    return pltpu.CompilerParams(dimension_semantics=sem, vmem_limit_bytes=VMEM_LIMIT)


def _resident(shape, index_map):
    return pl.BlockSpec(shape, index_map, pipeline_mode=pl.Buffered(1))


def _rms(x, g):
    return x * lax.rsqrt(jnp.mean(x * x, axis=-1, keepdims=True) + NORM_EPS) * g


def _dot(a, b):
    return jnp.dot(a.astype(BF16), b.astype(BF16), preferred_element_type=F32)


def _dot_nt(a, b):
    return lax.dot_general(a.astype(BF16), b.astype(BF16), (((1,), (1,)), ((), ())), preferred_element_type=F32)


def _split3(x):
    h1 = x.astype(BF16)
    r1 = x - h1.astype(F32)
    h2 = r1.astype(BF16)
    h3 = (r1 - h2.astype(F32)).astype(BF16)
    return h1, h2, h3


def _dot_exact_lhs(a01, x):
    return jnp.dot(jnp.concatenate([a01] * 3, axis=1), jnp.concatenate(_split3(x), axis=0), preferred_element_type=F32)


def _dot_x2(a, b):
    a1 = a.astype(BF16)
    b1 = b.astype(BF16)
    b2 = (b - b1.astype(F32)).astype(BF16)
    return jnp.dot(jnp.concatenate([a1, a1], axis=1), jnp.concatenate([b1, b2], axis=0), preferred_element_type=F32)


def _mod_kernel(c_ref, w_ref, b_ref, o_ref):
    c = c_ref[...]
    s = c * jax.nn.sigmoid(c)
    o_ref[...] = _dot(s, w_ref[...]) + b_ref[...]


def _modulation(cc, ada_w, ada_b):
    depth = ada_w.shape[0]
    tn = 1024
    return pl.pallas_call(
        _mod_kernel,
        grid=(depth, 6 * D_MODEL // tn),
        in_specs=[
            pl.BlockSpec((MOD_ROWS, D_MODEL), lambda l, j: (0, 0)),
            pl.BlockSpec((None, D_MODEL, tn), lambda l, j: (l, 0, j)),
            pl.BlockSpec((None, 1, tn), lambda l, j: (l, 0, j)),
        ],
        out_specs=pl.BlockSpec((None, MOD_ROWS, tn), lambda l, j: (l, 0, j)),
        out_shape=jax.ShapeDtypeStruct((depth, MOD_ROWS, 6 * D_MODEL), F32),
        compiler_params=_cparams(("parallel", "parallel")),
        name="modulation",
    )(cc, ada_w, ada_b.reshape(depth, 1, 6 * D_MODEL))


def _mod_row(mod_ref, row, k):
    return mod_ref[pl.ds(row, 1), k * D_MODEL:(k + 1) * D_MODEL]


def _halo_specs(width, n_rows, tile0=0):
    per = ROW_TILE // HALO
    last_tile = n_rows // ROW_TILE - 1
    last = n_rows // HALO - 1
    clamp = lambda i, hi: jnp.clip(i, 0, hi)
    cur = pl.BlockSpec((None, ROW_TILE, width), lambda b, t: (b, clamp(t - tile0, last_tile), 0))
    prev = pl.BlockSpec((None, HALO, width), lambda b, t: (b, clamp((t - tile0) * per - 1, last), 0))
    nxt = pl.BlockSpec((None, HALO, width), lambda b, t: (b, clamp((t - tile0 + 1) * per, last), 0))
    return cur, prev, nxt


def _token_tile_spec(n_rows, tile0, t_off):
    last_tile = n_rows // ROW_TILE - 1
    return pl.BlockSpec((None, ROW_TILE, D_MODEL), lambda b, t: (b, jnp.clip(t + t_off - tile0, 0, last_tile), 0))


def _seq_edges(t, nct, nt):
    has_prev = jnp.logical_and(t != 0, t != nct)
    has_next = jnp.logical_and(t != nct - 1, t != nt - 1)
    return has_prev, has_next


def _seg_sum(x, e_ref):
    h1 = x.astype(BF16)
    h2 = (x - h1.astype(F32)).astype(BF16)
    e = e_ref[...]
    return jnp.dot(h1, e, preferred_element_type=F32) + jnp.dot(h2, e, preferred_element_type=F32)


IN_SEGMENTS = (MLA_FEAT, RWKV_COLS, POOL_WIDTH, CONV_COLS)
HALO_ROWS = ROW_TILE + 2 * HALO
LO, HI = HALO, HALO + ROW_TILE


def _rope(x, cs, sn):
    return x * cs + pltpu.roll(x, 64, 1) * sn


def _store_with_halo(buf, cols, res):
    buf[LO:HI, cols] = res[0:ROW_TILE]
    buf[0:LO, cols] = res[ROW_TILE:ROW_TILE + HALO]
    buf[HI:HI + HALO, cols] = res[ROW_TILE + HALO:ROW_TILE + 2 * HALO]


def _mla_qkv(f, cs, sn, gq_ref, gkv_ref, wq_ref, wkv_ref, q_ref, k_ref, v_ref):
    scale = (MLA_NOPE + MLA_ROPE) ** -0.5
    q = _dot(_rms(f[:, 0:MLA_Q_RANK], gq_ref[...]), wq_ref[...])
    kv = _dot(_rms(f[:, MLA_Q_RANK:MLA_Q_RANK + MLA_KV_RANK], gkv_ref[...]), wkv_ref[...])
    k_pe = _rope(f[:, 512:640], cs, sn).astype(BF16)
    for h in range(MLA_HEADS):
        o = h * MLA_QK_PAD
        q_ref[:, o:o + 128] = (q[:, o:o + 128] * scale).astype(BF16)
        q_ref[:, o + 128:o + 256] = (_rope(q[:, o + 128:o + 256], cs, sn) * scale).astype(BF16)
        k_ref[:, o:o + 128] = kv[:, o:o + 128].astype(BF16)
        k_ref[:, o + 128:o + 256] = k_pe
        v_ref[:, h * MLA_V:(h + 1) * MLA_V] = kv[:, o + 128:o + 256].astype(BF16)


def _rwkv_features(buf, mu_ref, w0_ref, a0_ref, w2_ref, a2_ref, g2_ref, kk_ref, ka_ref, rk_ref, e_ref, o_ref):
    w = RWKV_WIDTH

    def shifted(cols):
        f = buf[LO:HI, cols]
        return f + mu_ref[:, cols] * (0.5 * (buf[LO - 1:HI - 1, cols] + buf[LO + 1:HI + 1, cols]) - f)

    def put(i, val):
        o_ref[:, i * w:(i + 1) * w] = val

    xw = w0_ref[...] + _dot(jnp.tanh(shifted(slice(1536, 1792))), w2_ref[...])
    lw = -float(np.exp(-0.5)) * jax.nn.sigmoid(xw)
    put(5, lw[:, 0:w])
    put(8, lw[:, w:2 * w])
    yield
    a = jax.nn.sigmoid(a0_ref[...] + _dot(shifted(slice(1664, 1920)), a2_ref[...]))
    put(7, a[:, 0:w])
    put(10, a[:, w:2 * w])
    put(3, _dot(jax.nn.sigmoid(shifted(slice(1920, 2176))), g2_ref[...]))
    yield
    k = shifted(slice(w, 2 * w))
    kkv = k * kk_ref[...]
    put(2, kkv / jnp.maximum(jnp.sqrt(_seg_sum(kkv * kkv, e_ref)), 1e-12))
    yield
    r = shifted(slice(0, w))
    put(0, r)
    rk = r * rk_ref[...]
    ka = ka_ref[...]
    k_f = k * (1.0 + (a[:, 0:w] - 1.0) * ka)
    k_b = k * (1.0 + (a[:, w:2 * w] - 1.0) * ka)
    put(6, k_f)
    put(9, k_b)
    yield
    v = shifted(slice(2 * w, 3 * w))
    put(1, v)
    put(4, _seg_sum(rk * k_f + rk * k_b, e_ref) * v)


def _pool_mixer(buf, pw_ref, ps_ref, o_ref, pos, n_own):
    for gi, win in enumerate(POOL_WINDOWS):
        hw = win // 2
        cols = slice(gi * POOL_GROUP, (gi + 1) * POOL_GROUP)
        s = buf[LO - hw:HI - hw, cols]
        for o in range(-hw + 1, hw):
            s = s + buf[LO + o:HI + o, cols]
        cnt = (jnp.minimum(pos + hw, n_own) - jnp.maximum(pos - hw, 0)).astype(F32)
        z = s / cnt - buf[LO:HI, cols]
        o_ref[:, cols] = (_dot(z, pw_ref[gi]) * ps_ref[:, cols]).astype(o_ref.dtype)


def _pick_rows(refs, in_ctx):
    if len(refs) == 3:
        return [r[...] for r in refs]
    return [jnp.where(in_ctx, c[...], x[...]) for c, x in zip(refs[0:3], refs[3:6])]


FRONT_FIXED_INPUTS = 22


def _front_kernel(*refs, nparts, ncast, nb, nct, nt, n_ctx, n_seq):
    x_refs, refs = refs[:3 * nparts], refs[3 * nparts:]
    (mod_ref, g_ref, w_ref, cs_ref, sn_ref, gq_ref, gkv_ref, wq_ref, wkv_ref,
     mu_ref, w0_ref, a0_ref, w2_ref, a2_ref, g2_ref, kk_ref, ka_ref, rk_ref, e_ref,
     pw_ref, ps_ref, cw_ref) = refs[:FRONT_FIXED_INPUTS]
    cast_in, refs = refs[FRONT_FIXED_INPUTS:FRONT_FIXED_INPUTS + ncast], refs[FRONT_FIXED_INPUTS + ncast:]
    q_ref, k_ref, v_ref, pack_ref, pool_ref, conv_ref = refs[:6]
    cast_out = refs[6:6 + ncast]
    rw_buf, pool_buf, u_buf = refs[6 + ncast:]
    for src, dst in zip(cast_in, cast_out):
        dst[...] = src[...].astype(dst.dtype)
    b = pl.program_id(0)
    t = pl.program_id(1)
    in_ctx = t < nct
    row = jnp.where(in_ctx, nb, b)
    has_prev, has_next = _seq_edges(t, nct, nt)
    shift, scale = _mod_row(mod_ref, row, 0), 1.0 + _mod_row(mod_ref, row, 1)
    norm = lambda x: _rms(x, g_ref[...]) * scale + shift
    x_cur, x_prev, x_next = _pick_rows(x_refs, in_ctx)
    xn = jnp.concatenate([norm(x_cur), jnp.where(has_prev, norm(x_prev), 0.0), jnp.where(has_next, norm(x_next), 0.0)],
                         axis=0).astype(BF16)

    mla0 = 0
    rw0 = IN_SEGMENTS[0]
    pool0 = rw0 + RWKV_COLS
    conv0 = pool0 + POOL_WIDTH
    cw = CONV_WIDTH
    proj = lambda rows, c0, c1: lax.dot_general(rows, w_ref[c0:c1, :], (((1,), (1,)), ((), ())),
                                                preferred_element_type=F32)

    c0 = 0
    while c0 < RWKV_COLS:
        c1 = min(c0 + 512, RWKV_COLS)
        _store_with_halo(rw_buf, slice(c0, c1), proj(xn, rw0 + c0, rw0 + c1))
        c0 = c1
    stages = _rwkv_features(rw_buf, mu_ref, w0_ref, a0_ref, w2_ref, a2_ref, g2_ref, kk_ref, ka_ref, rk_ref, e_ref, pack_ref)
    next(stages)
    f_mla = proj(xn[0:ROW_TILE], mla0, rw0)
    _store_with_halo(pool_buf, slice(None), proj(xn, pool0, conv0))
    next(stages)
    gb = proj(xn[0:ROW_TILE], conv0, conv0 + cw)
    _mla_qkv(f_mla, cs_ref[...], sn_ref[...], gq_ref, gkv_ref, wq_ref, wkv_ref, q_ref, k_ref, v_ref)
    next(stages)
    gc = proj(xn, conv0 + cw, conv0 + 2 * cw)
    pos = lax.broadcasted_iota(jnp.int32, (ROW_TILE, 1), 0) + jnp.where(in_ctx, t, t - nct) * ROW_TILE
    _pool_mixer(pool_buf, pw_ref, ps_ref, pool_ref, pos, jnp.where(in_ctx, n_ctx, n_seq))
    next(stages)
    hx = proj(xn, conv0 + 2 * cw, conv0 + 3 * cw)
    for _ in stages:
        pass
    _store_with_halo(u_buf, slice(None), gc * hx)
    z = cw_ref[0:1, :] * u_buf[LO - 1:HI - 1] + cw_ref[1:2, :] * u_buf[LO:HI] + cw_ref[2:3, :] * u_buf[LO + 1:HI + 1]
    conv_ref[...] = (gb * z).astype(conv_ref.dtype)


def _layer_spec(a, l):
    return _resident((None,) + a.shape[1:], lambda *_: (l,) + (0,) * (a.ndim - 1))


def _cast_slab_spec(w, l, b_, nt):
    _, rows, cols = w.shape
    k = 1
    while k * 2 <= nt and cols % (k * 2 * 128) == 0:
        k *= 2
    assert rows % (b_ * 8) == 0
    shape = (rows // b_, cols // k)
    col = lambda t: jnp.minimum(t, k - 1)
    return (pl.BlockSpec((None,) + shape, lambda b, t: (l, b, col(t))),
            pl.BlockSpec(shape, lambda b, t: (b, col(t))),
            jax.ShapeDtypeStruct((rows, cols), BF16))


def _front(h_parts, l, mod, g, w_p, cs, sn, gq, gkv, wq, wkv, p, pool_w, pool_scale, conv_w, cast_ws, nb, nct, n_ctx):
    b_ = h_parts[0].shape[0]
    n_ = sum(h.shape[1] for h in h_parts)
    nt = n_ // ROW_TILE
    casts = [_cast_slab_spec(w, l, b_, nt) for w in cast_ws]
    qk_w = MLA_HEADS * MLA_QK_PAD
    pack_w = RWKV_PACK_BLOCKS * RWKV_WIDTH
    full = lambda a: _resident(a.shape, lambda b, t: (0,) * a.ndim)
    row = lambda w: pl.BlockSpec((None, ROW_TILE, w), lambda b, t: (b, t, 0))
    tab = pl.BlockSpec((ROW_TILE, 128), lambda b, t: (t, 0))
    x_specs, x_args, tile0 = [], [], 0
    for h in h_parts:
        x_specs += _halo_specs(D_MODEL, h.shape[1], tile0)
        x_args += [h, h, h]
        tile0 += h.shape[1] // ROW_TILE
    consts = (mod, g.reshape(1, D_MODEL))
    mla = (gq.reshape(1, -1), gkv.reshape(1, -1), wq, wkv)
    rwk = (p["mu"], p["w0"], p["a0"], p["w2"], p["a2"], p["g2"], p["k_k"], p["k_a"], p["r_k"], p["seg"])
    mix = (pool_w, pool_scale, conv_w)
    fixed = [full(a) for a in consts] + [_layer_spec(w_p, l), tab, tab] + [full(a) for a in mla + rwk + mix]
    assert len(fixed) == FRONT_FIXED_INPUTS
    outs = pl.pallas_call(
        functools.partial(_front_kernel, nparts=len(h_parts), ncast=len(casts), nb=nb, nct=nct, nt=nt,
                          n_ctx=n_ctx, n_seq=n_ - n_ctx),
        grid=(b_, nt),
        in_specs=x_specs + fixed + [c[0] for c in casts],
        out_specs=(row(qk_w), row(qk_w), row(MLA_HEADS * MLA_V), row(pack_w), row(POOL_WIDTH), row(CONV_WIDTH))
                  + tuple(c[1] for c in casts),
        out_shape=(
            jax.ShapeDtypeStruct((b_, n_, qk_w), BF16),
            jax.ShapeDtypeStruct((b_, n_, qk_w), BF16),
            jax.ShapeDtypeStruct((b_, n_, MLA_HEADS * MLA_V), BF16),
            jax.ShapeDtypeStruct((b_, n_, pack_w), F32),
            jax.ShapeDtypeStruct((b_, n_, POOL_WIDTH), BF16),
            jax.ShapeDtypeStruct((b_, n_, CONV_WIDTH), BF16),
        ) + tuple(c[2] for c in casts),
        scratch_shapes=[pltpu.VMEM((HALO_ROWS, RWKV_COLS), F32), pltpu.VMEM((HALO_ROWS, POOL_WIDTH), F32),
                        pltpu.VMEM((HALO_ROWS, CONV_WIDTH), F32)],
        compiler_params=_cparams(("parallel", "arbitrary")),
        name="front",
    )(*x_args, *consts, w_p, cs, sn, *mla, *rwk, *mix, *cast_ws)
    return outs[:6], outs[6:]


ATTN_Q_TILE = 512
ATTN_SOFTMAX_ROWS = 128


def _attn_kernel(q_ref, k_ref, v_ref, o_ref, *, n_ctx, n_all, tq, with_ctx):
    s = pl.program_id(1)

    def attend(q_rows, o_rows, nk, nq):
        for h in range(MLA_HEADS):
            q = q_ref[q_rows, h * MLA_QK_PAD:(h + 1) * MLA_QK_PAD]
            k = k_ref[0:nk, h * MLA_QK_PAD:(h + 1) * MLA_QK_PAD]
            sc = lax.dot_general(q, k, (((1,), (1,)), ((), ())), preferred_element_type=F32)
            ps, ls = [], []
            for r0 in range(0, nq, ATTN_SOFTMAX_ROWS):
                blk = sc[r0:r0 + ATTN_SOFTMAX_ROWS]
                p = jnp.exp(blk - jnp.max(blk, axis=-1, keepdims=True))
                ls.append(jnp.sum(p, axis=-1, keepdims=True))
                ps.append(p.astype(BF16))
            p = jnp.concatenate(ps, axis=0)
            l = jnp.concatenate(ls, axis=0)
            o = jnp.dot(p, v_ref[0:nk, h * MLA_V:(h + 1) * MLA_V], preferred_element_type=F32)
            o_ref[o_rows, h * MLA_V:(h + 1) * MLA_V] = (o / l).astype(o_ref.dtype)

    if with_ctx:
        @pl.when(s == 0)
        def _():
            attend(slice(0, n_ctx), slice(0, n_ctx), n_ctx, n_ctx)

        @pl.when(s > 0)
        def _():
            r0 = pl.multiple_of(n_ctx + (s - 1) * tq, ROW_TILE)
            attend(pl.ds(r0, tq), pl.ds(r0, tq), n_all, tq)
    else:
        attend(pl.ds(pl.multiple_of(n_ctx + s * tq, ROW_TILE), tq), pl.ds(pl.multiple_of(s * tq, ROW_TILE), tq), n_all, tq)


def _attention(q, k, v, n_ctx, with_ctx):
    b_, n_, qk_w = q.shape
    n_seq = n_ - n_ctx
    tq = ATTN_Q_TILE if n_seq % ATTN_Q_TILE == 0 else ROW_TILE
    n_out = n_ if with_ctx else n_seq
    whole = lambda rows, w: pl.BlockSpec((None, rows, w), lambda b, s: (b, 0, 0))
    return pl.pallas_call(
        functools.partial(_attn_kernel, n_ctx=n_ctx, n_all=n_, tq=tq, with_ctx=with_ctx),
        grid=(b_, n_seq // tq + (1 if with_ctx else 0)),
        in_specs=[whole(n_, qk_w), whole(n_, qk_w), whole(n_, MLA_HEADS * MLA_V)],
        out_specs=whole(n_out, MLA_HEADS * MLA_V),
        out_shape=jax.ShapeDtypeStruct((b_, n_out, MLA_HEADS * MLA_V), BF16),
        compiler_params=_cparams(("parallel", "arbitrary")),
        name="mla_attention",
    )(q, k, v)


def _bd(x, m0):
    return jnp.concatenate([jnp.where(m0, x, 0.0), jnp.where(m0, 0.0, x)], axis=0)


def _wkv_stages(r_ref, v_ref, kk_ref, lw_ref, kd_ref, a_ref, y_ref, s_ref, rw_s, y0_s, m_s, n_s, rev):
    c_ = WKV_CHUNK
    nsub = ROW_TILE // c_
    npair = RWKV_WIDTH // 128

    ri = lax.broadcasted_iota(jnp.int32, (c_, c_), 0)
    ci = lax.broadcasted_iota(jnp.int32, (c_, c_), 1)
    tmat = ((ri <= ci) if rev else (ri >= ci)).astype(BF16)
    rc = lax.broadcasted_iota(jnp.int32, (c_, 128), 0)
    cc = lax.broadcasted_iota(jnp.int32, (c_, 128), 1) % c_
    strict = (rc < cc) if rev else (rc > cc)
    incl = (rc <= cc) if rev else (rc >= cc)
    eye = rc == cc
    m0_128 = lax.broadcasted_iota(jnp.int32, (1, 128), 1) < c_
    m0_256 = (lax.broadcasted_iota(jnp.int32, (1, 256), 1) % 128) < c_
    zeros_cv = jnp.zeros((c_, 128), F32)

    items = [(c, p) for c in range(nsub) for p in range(npair)]
    at, rt, bt, kt, bh, kh, vv, e_tot = {}, {}, {}, {}, {}, {}, {}, {}
    for c in range(nsub):
        rows = slice(c * c_, (c + 1) * c_)
        lw = lw_ref[rows, :]
        lc = _dot_exact_lhs(tmat, lw)
        ltot = jnp.sum(lw, axis=0, keepdims=True)
        e_neg = jnp.exp(-lc)
        e_end = jnp.exp(ltot - lc)
        kk = kk_ref[rows, :]
        kd = kd_ref[rows, :]
        bv = kk * a_ref[rows, :]
        vv[c] = v_ref[rows, :]
        at[c] = -kk * jnp.exp(lc - lw)
        rt[c] = r_ref[rows, :] * jnp.exp(lc)
        bt[c], kt[c] = bv * e_neg, kd * e_neg
        bh[c], kh[c] = bv * e_end, kd * e_end
        e_tot[c] = jnp.exp(ltot)
        yield
    sl = lambda p: slice(p * 128, (p + 1) * 128)

    g = {}
    for c, p in items:
        ar = jnp.concatenate([at[c][:, sl(p)], rt[c][:, sl(p)]], axis=0)
        bd = jnp.concatenate([_bd(bt[c][:, sl(p)], m0_128), _bd(kt[c][:, sl(p)], m0_128)], axis=0)
        g[c, p] = _dot_nt(ar, bd)
    yield
    av, x, pw, a_rb = {}, {}, {}, {}
    for c, p in items:
        a_ak = jnp.where(strict, g[c, p][0:c_, 128:256], 0.0)
        a_rk = jnp.where(incl, g[c, p][c_:2 * c_, 128:256], 0.0)
        av[c, p] = _dot(jnp.concatenate([a_ak, a_rk], axis=0), _bd(vv[c][:, sl(p)], m0_128))
        pw[c, p] = jnp.where(strict, g[c, p][0:c_, 0:128], 0.0)
        a_rb[c, p] = jnp.where(incl, g[c, p][c_:2 * c_, 0:128], 0.0)
    for c, p in items:
        x[c, p] = jnp.concatenate([at[c][:, sl(p)], av[c, p][0:c_]], axis=1)
    yield
    for it in range(6):
        dot = _dot_x2 if it < 2 else _dot
        for c, p in items:
            x[c, p] = x[c, p] + dot(pw[c, p], _bd(x[c, p], m0_256))
        yield
        if it < 5:
            for c, p in items:
                pw[c, p] = _dot(pw[c, p], _bd(pw[c, p], m0_128))
            yield
    for c, p in items:
        ry = _dot(a_rb[c, p], _bd(x[c, p], m0_256))
        rw_s[c, p] = rt[c][:, sl(p)] + ry[:, 0:128]
        y0_s[c, p] = ry[:, 128:256] + av[c, p][c_:2 * c_]
    yield
    for c, p in items:
        bkt = jnp.concatenate([bh[c][:, sl(p)], kh[c][:, sl(p)]], axis=0).T
        rhs = jnp.concatenate([x[c, p], jnp.concatenate([zeros_cv, vv[c][:, sl(p)]], axis=1)], axis=0)
        z = _dot(bkt, rhs)
        m_s[c, p] = (jnp.where(m0_128, z[0:c_, 0:128], z[c_:2 * c_, 0:128])
                     + jnp.where(eye, e_tot[c][:, sl(p)], 0.0))
        n_s[c, p] = jnp.where(m0_128, z[0:c_, 128:256], z[c_:2 * c_, 128:256])
    yield

    for i in range(nsub):
        c = (nsub - 1 - i) if rev else i
        for p in range(npair):
            lhs = jnp.concatenate([rw_s[c, p], m_s[c, p]], axis=0)
            o = _dot_x2(lhs, _bd(s_ref[p], m0_128))
            y_ref[c * c_:(c + 1) * c_, sl(p)] = o[0:c_] + y0_s[c, p]
            s_ref[p] = o[c_:2 * c_] + n_s[c, p]
        yield


WKV_LEAD_STAGES = 9


def _wkv_kernel(*refs):
    ins_f, ins_b, (y_f, y_b), scr_f, scr_b = refs[0:6], refs[6:12], refs[12:14], refs[14:19], refs[19:24]

    @pl.when(pl.program_id(1) == 0)
    def _():
        scr_f[0][...] = jnp.zeros_like(scr_f[0])
        scr_b[0][...] = jnp.zeros_like(scr_b[0])

    fwd = _wkv_stages(*ins_f, y_f, *scr_f, rev=False)
    bwd = _wkv_stages(*ins_b, y_b, *scr_b, rev=True)
    for _ in range(WKV_LEAD_STAGES):
        next(fwd)
    live = [fwd, bwd]
    while live:
        for gen in list(live):
            if next(gen, StopIteration) is StopIteration:
                live.remove(gen)


def _wkv(pack, nct):
    b_, n_, _ = pack.shape
    nt = n_ // ROW_TILE
    w = RWKV_WIDTH
    nsub = ROW_TILE // WKV_CHUNK
    npair = w // 128
    fwd_tile = lambda s: s
    bwd_tile = lambda s: jnp.where(s < nct, nct - 1 - s, nt + nct - 1 - s)
    col = lambda order, j: pl.BlockSpec((None, ROW_TILE, w), lambda b, s: (b, order(s), j))
    out = lambda order: pl.BlockSpec((None, ROW_TILE, w), lambda b, s: (b, order(s), 0))
    scr = lambda: [pltpu.VMEM((npair, WKV_CHUNK, 128), F32)] + [pltpu.VMEM((nsub, npair, WKV_CHUNK, 128), F32)] * 4
    y = jax.ShapeDtypeStruct((b_, n_, w), F32)
    return pl.pallas_call(
        _wkv_kernel,
        grid=(b_, nt),
        in_specs=[col(fwd_tile, j) for j in (0, 1, 2, 5, 6, 7)] + [col(bwd_tile, j) for j in (0, 1, 2, 8, 9, 10)],
        out_specs=(out(fwd_tile), out(bwd_tile)),
        out_shape=(y, y),
        scratch_shapes=scr() + scr(),
        compiler_params=_cparams(("parallel", "arbitrary")),
        name="wkv",
    )(*([pack] * 12))


def _mix_out_kernel(att_ref, yf_ref, yb_ref, g_ref, bonus_ref, lg_ref, lb_ref, e_ref, pool_ref, conv_ref,
                    mod_ref, w_ref, g2_ref, *refs, nb, nct, t_off):
    h_refs, (o_ref, xn_ref) = refs[:-2], refs[-2:]
    b = pl.program_id(0)
    t = pl.program_id(1) + t_off
    in_ctx = t < nct
    row = jnp.where(in_ctx, nb, b)
    h = h_refs[0][...] if len(h_refs) == 1 else jnp.where(in_ctx, h_refs[0][...], h_refs[1][...])
    y = yf_ref[...] + yb_ref[...]
    mean = _seg_sum(y, e_ref) * (1.0 / RWKV_HEAD)
    d = y - mean
    var = _seg_sum(d * d, e_ref) * (1.0 / RWKV_HEAD)
    yn = d * lax.rsqrt(var + RWKV_GN_EPS) * lg_ref[...] + lb_ref[...]
    rw = ((yn + bonus_ref[...]) * g_ref[...]).astype(BF16)
    mixed = jnp.concatenate([att_ref[...], rw, pool_ref[...], conv_ref[...]], axis=1)
    acc = jnp.dot(mixed, w_ref[...], preferred_element_type=F32)
    h = h + _mod_row(mod_ref, row, 2) * acc
    o_ref[...] = h
    xn_ref[...] = (_rms(h, g2_ref[...]) * (1.0 + _mod_row(mod_ref, row, 4)) + _mod_row(mod_ref, row, 3)).astype(BF16)


def _mix_out(att, yf, yb, pack, pool, conv, h_parts, mod, w_out, g2, p, nb, nct, with_ctx):
    b_, n_, gw = yf.shape
    t_off = 0 if with_ctx else nct
    n_out = n_ - t_off * ROW_TILE
    blk = lambda j: pl.BlockSpec((None, ROW_TILE, gw), lambda b, t: (b, t + t_off, j))
    full = lambda a: _resident(a.shape, lambda b, t: (0,) * a.ndim)
    h_specs, tile0 = [], 0
    for h in h_parts:
        h_specs.append(_token_tile_spec(h.shape[1], tile0, t_off))
        tile0 += h.shape[1] // ROW_TILE
    out = pl.BlockSpec((None, ROW_TILE, D_MODEL), lambda b, t: (b, t, 0))
    return pl.pallas_call(
        functools.partial(_mix_out_kernel, nb=nb, nct=nct, t_off=t_off),
        grid=(b_, n_out // ROW_TILE),
        in_specs=[pl.BlockSpec((None, ROW_TILE, gw), lambda b, t: (b, t, 0)),
                  blk(0), blk(0), blk(3), blk(4), full(p["ln_g"]), full(p["ln_b"]), full(p["seg"]),
                  blk(0), blk(0), full(mod), full(w_out), full(g2)] + h_specs,
        out_specs=(out, out),
        out_shape=(jax.ShapeDtypeStruct((b_, n_out, D_MODEL), F32), jax.ShapeDtypeStruct((b_, n_out, D_MODEL), BF16)),
        compiler_params=_cparams(("parallel", "parallel")),
        name="mix_out",
    )(att, yf, yb, pack, pack, p["ln_g"], p["ln_b"], p["seg"], pool, conv, mod, w_out, g2, *h_parts)


MLP_HIDDEN_TILE = 512


def _mlp_kernel(h_ref, xn_ref, mod_ref, w1_ref, w2_ref, gf_ref, o_ref, *, nb, ctx_rows, final):
    b = pl.program_id(0)
    t = pl.program_id(1)
    j = pl.program_id(2)

    def hidden_slice(first):
        hid = jnp.dot(xn_ref[...], w1_ref[...], preferred_element_type=F32)
        hid = jnp.square(jnp.maximum(hid, 0.0))
        part = jnp.dot(hid.astype(BF16), w2_ref[...], preferred_element_type=F32)
        o_ref[...] = part if first else o_ref[...] + part

    @pl.when(j == 0)
    def _():
        hidden_slice(True)

    @pl.when(j > 0)
    def _():
        hidden_slice(False)

    @pl.when(j == pl.num_programs(2) - 1)
    def _():
        def put(rows, mrow):
            o = h_ref[rows, :] + _mod_row(mod_ref, mrow, 5) * o_ref[rows, :]
            if final:
                o = _rms(o, gf_ref[...])
            o_ref[rows, :] = o

        first_row = jnp.where(t == 0, nb, b)
        if ctx_rows == 0:
            put(slice(None), b)
        elif ctx_rows == h_ref.shape[0]:
            put(slice(None), first_row)
        else:
            put(slice(0, ctx_rows), first_row)
            put(slice(ctx_rows, None), b)


def _mlp(h, xn, mod, w1, w2, g_final, nb, tm, ctx_rows, final):
    b_, n_, _ = h.shape
    th = MLP_HIDDEN_TILE * (2 if tm < 1024 else 1)
    tile = pl.BlockSpec((None, tm, D_MODEL), lambda b, t, j: (b, t, 0))
    return pl.pallas_call(
        functools.partial(_mlp_kernel, nb=nb, ctx_rows=ctx_rows, final=final),
        grid=(b_, n_ // tm, FF_HIDDEN // th),
        in_specs=[
            tile, tile,
            _resident((MOD_ROWS, 6 * D_MODEL), lambda b, t, j: (0, 0)),
            pl.BlockSpec((D_MODEL, th), lambda b, t, j: (0, j)),
            pl.BlockSpec((th, D_MODEL), lambda b, t, j: (j, 0)),
            _resident((1, D_MODEL), lambda b, t, j: (0, 0)),
        ],
        out_specs=tile,
        out_shape=jax.ShapeDtypeStruct((b_, n_, D_MODEL), F32),
        compiler_params=_cparams(("parallel", "parallel", "arbitrary")),
        name="mlp",
    )(h, xn, mod, w1, w2, g_final.reshape(1, D_MODEL))


def _rope_partner_perm():
    q = MLA_ROPE // 4
    i = np.arange(MLA_ROPE)
    return np.where((i // q) % 2 == 0, i + q, i - q)


def _rope_tables(n_ctx, n_seq):
    f32 = np.float32
    rows = n_seq // GRID_W
    row = np.repeat(np.arange(rows), GRID_W)
    col = np.tile(np.arange(GRID_W), rows)
    pos = np.stack([row, col], axis=-1).astype(f32)
    axis_dim = MLA_ROPE // 2
    inv_freq = (f32(ROPE_BASE) ** (-np.arange(0, axis_dim, 2, dtype=f32) / f32(axis_dim))).astype(f32)
    ang = (pos[:, :, None] * inv_freq).astype(f32)
    cos, sin = np.cos(ang), np.sin(ang)
    cos64 = np.concatenate([cos[:, 0], cos[:, 0], cos[:, 1], cos[:, 1]], axis=-1)
    sin64 = np.concatenate([-sin[:, 0], sin[:, 0], -sin[:, 1], sin[:, 1]], axis=-1)
    pad = np.zeros((n_seq, 128 - MLA_ROPE), f32)
    cs_lat = np.concatenate([cos64, pad], axis=-1)
    sn_lat = np.concatenate([sin64, pad], axis=-1)
    cs_ctx = np.concatenate([np.ones((n_ctx, MLA_ROPE), f32), np.zeros((n_ctx, 128 - MLA_ROPE), f32)], axis=-1)
    cs = np.concatenate([cs_ctx, cs_lat], axis=0).astype(f32)
    sn = np.concatenate([np.zeros((n_ctx, 128), f32), sn_lat], axis=0).astype(f32)
    return jnp.asarray(cs), jnp.asarray(sn)


W_IN_BLOCK = 256


def _in_proj_relayout_kernel(cur_ref, o_ref, tail_s):
    j = pl.program_id(1)
    kr_block = (MLA_Q_RANK + MLA_KV_RANK) // W_IN_BLOCK
    kr = (MLA_Q_RANK + MLA_KV_RANK) % W_IN_BLOCK
    q = MLA_ROPE // 4

    @pl.when(j < kr_block)
    def _():
        o_ref[...] = cur_ref[...].astype(BF16)

    @pl.when(j == kr_block)
    def _():
        end = kr + MLA_ROPE
        o_ref[0:end] = cur_ref[0:end].astype(BF16)
        for i in range(4):
            src = kr + (i ^ 1) * q
            o_ref[end + i * q:end + (i + 1) * q] = cur_ref[src:src + q].astype(BF16)
        o_ref[end + MLA_ROPE:W_IN_BLOCK] = cur_ref[end:W_IN_BLOCK - MLA_ROPE].astype(BF16)

    @pl.when(j > kr_block)
    def _():
        o_ref[0:MLA_ROPE] = tail_s[...].astype(BF16)
        o_ref[MLA_ROPE:W_IN_BLOCK] = cur_ref[0:W_IN_BLOCK - MLA_ROPE].astype(BF16)

    @pl.when(j >= kr_block)
    def _():
        tail_s[...] = cur_ref[W_IN_BLOCK - MLA_ROPE:W_IN_BLOCK]


def _in_proj_weights(w_in):
    depth, d, n_in = w_in.shape
    n_out = n_in + MLA_ROPE
    assert n_out == sum(IN_SEGMENTS) and n_out % W_IN_BLOCK == 0
    w_t = jnp.swapaxes(w_in, 1, 2)
    last = (n_in - 1) // W_IN_BLOCK
    return pl.pallas_call(
        _in_proj_relayout_kernel,
        grid=(depth, n_out // W_IN_BLOCK),
        in_specs=[pl.BlockSpec((None, W_IN_BLOCK, d), lambda l, j: (l, jnp.minimum(j, last), 0))],
        out_specs=pl.BlockSpec((None, W_IN_BLOCK, d), lambda l, j: (l, j, 0)),
        out_shape=jax.ShapeDtypeStruct((depth, n_out, d), BF16),
        scratch_shapes=[pltpu.VMEM((MLA_ROPE, d), F32)],
        compiler_params=_cparams(("parallel", "arbitrary")),
        name="in_proj_relayout",
    )(w_t)


def _layer_params(l, mla_w_uq, mla_w_ukv, rwkv_mu, rwkv_w0, rwkv_w2, rwkv_a0, rwkv_a2, rwkv_g2,
                  rwkv_k_k, rwkv_k_a, rwkv_r_k, rwkv_ln_g, rwkv_ln_b, seg):
    perm = _rope_partner_perm()
    wq = mla_w_uq[l].reshape(MLA_Q_RANK, MLA_HEADS, MLA_NOPE + MLA_ROPE)
    wq = jnp.concatenate([wq, wq[:, :, MLA_NOPE:][:, :, perm]], axis=-1).reshape(MLA_Q_RANK, MLA_HEADS * MLA_QK_PAD)

    wd = RWKV_WIDTH
    z = lambda r: jnp.zeros((r, wd), F32)
    w2 = jnp.concatenate([
        jnp.concatenate([rwkv_w2[l, 0], z(DECAY_RANK), z(256 - 2 * DECAY_RANK)], axis=0),
        jnp.concatenate([z(DECAY_RANK), rwkv_w2[l, 1], z(256 - 2 * DECAY_RANK)], axis=0)], axis=1)
    lead = 256 - 2 * ICL_RANK
    a2 = jnp.concatenate([
        jnp.concatenate([z(lead), rwkv_a2[l, 0], z(ICL_RANK)], axis=0),
        jnp.concatenate([z(lead), z(ICL_RANK), rwkv_a2[l, 1]], axis=0)], axis=1)
    rw = {
        "mu": rwkv_mu[l].reshape(1, RWKV_COLS),
        "w0": rwkv_w0[l].reshape(1, 2 * wd),
        "a0": rwkv_a0[l].reshape(1, 2 * wd),
        "w2": w2.astype(BF16),
        "a2": a2.astype(BF16),
        "g2": rwkv_g2[l].astype(BF16),
        "k_k": rwkv_k_k[l].reshape(1, wd),
        "k_a": rwkv_k_a[l].reshape(1, wd),
        "r_k": rwkv_r_k[l].reshape(1, wd),
        "ln_g": rwkv_ln_g[l].reshape(1, wd),
        "ln_b": rwkv_ln_b[l].reshape(1, wd),
        "seg": seg,
    }
    return wq.astype(BF16), mla_w_ukv[l].astype(BF16), rw


def kernel(x, c, ctx, c_ctx, ada_w, ada_b, norm1_g, norm2_g, w_in, mla_q_norm_g, mla_w_uq, mla_kv_norm_g, mla_w_ukv, rwkv_mu, rwkv_w0, rwkv_w2, rwkv_a0, rwkv_a2, rwkv_g2, rwkv_k_k, rwkv_k_a, rwkv_r_k, rwkv_ln_g, rwkv_ln_b, pool_w, pool_scale, conv_w, w_out, mlp_w1, mlp_w2, final_norm_g):
    nb, n_seq, d = x.shape
    n_ctx = ctx.shape[1]
    depth = ada_w.shape[0]
    assert d == D_MODEL and nb < MOD_ROWS
    assert n_ctx % ROW_TILE == 0 and n_seq % ROW_TILE == 0 and n_seq % GRID_W == 0
    nct = n_ctx // ROW_TILE

    cc = jnp.zeros((MOD_ROWS, D_MODEL), F32).at[:nb].set(c).at[nb].set(c_ctx)
    mod = _modulation(cc, ada_w, ada_b)
    cs, sn = _rope_tables(n_ctx, n_seq)
    head = np.arange(RWKV_WIDTH) // RWKV_HEAD
    seg = jnp.asarray(head[:, None] == head[None, :], dtype=BF16)

    w_in_p = _in_proj_weights(w_in)
    h_parts = (ctx, x)
    for l in range(depth):
        last = l == depth - 1
        with_ctx = not last
        wq, wkv, rw_p = _layer_params(
            l, mla_w_uq, mla_w_ukv, rwkv_mu, rwkv_w0, rwkv_w2, rwkv_a0, rwkv_a2, rwkv_g2,
            rwkv_k_k, rwkv_k_a, rwkv_r_k, rwkv_ln_g, rwkv_ln_b, seg)

        (q, k, v, pack, pool, conv), (w_out_b, w1_b, w2_b) = _front(
            h_parts, l, mod[l], norm1_g[l], w_in_p, cs, sn, mla_q_norm_g[l], mla_kv_norm_g[l], wq, wkv, rw_p,
            pool_w[l].astype(BF16), pool_scale[l].reshape(1, POOL_WIDTH), conv_w[l],
            (w_out, mlp_w1, mlp_w2), nb, nct, n_ctx)
        att = _attention(q, k, v, n_ctx, with_ctx)
        y_f, y_b = _wkv(pack, nct)
        h_mid, xn_mid = _mix_out(att, y_f, y_b, pack, pool, conv, h_parts, mod[l], w_out_b,
                                 norm2_g[l].reshape(1, D_MODEL), rw_p, nb, nct, with_ctx)

        n_rows = h_mid.shape[1]
        if with_ctx:
            tm = 3 * ROW_TILE if n_rows % (3 * ROW_TILE) == 0 else ROW_TILE
            ctx_rows = n_ctx
            assert n_ctx <= tm, "context longer than one MLP row tile"
        else:
            tm = 4 * ROW_TILE if n_rows % (4 * ROW_TILE) == 0 else ROW_TILE
            ctx_rows = 0
        h_parts = (_mlp(h_mid, xn_mid, mod[l], w1_b, w2_b, final_norm_g, nb, tm, ctx_rows, last),)
    return h_parts[0]
```

```python
import functools

import numpy as np
import jax
import jax.numpy as jnp
from jax import lax
from jax.experimental import pallas as pl
from jax.experimental.pallas import tpu as pltpu

F32 = jnp.float32
BF16 = jnp.bfloat16

D_MODEL = 2048
GRID_W = 64
NORM_EPS = 1e-6
GROUP_WIDTH = D_MODEL // 4

MLA_NOPE = 128
MLA_ROPE = 64
MLA_V = 128
MLA_HEADS = GROUP_WIDTH // MLA_V
MLA_Q_RANK = 384
MLA_KV_RANK = 128
ROPE_BASE = 10000.0
MLA_QK_PAD = 256
MLA_FEAT = 640

RWKV_HEAD = 64
RWKV_HEADS = GROUP_WIDTH // RWKV_HEAD
RWKV_WIDTH = RWKV_HEADS * RWKV_HEAD
DECAY_RANK = 96
ICL_RANK = 96
GATE_RANK = 256
RWKV_GN_EPS = 64e-5
RWKV_COLS = 3 * RWKV_WIDTH + 2 * DECAY_RANK + 2 * ICL_RANK + GATE_RANK
WKV_CHUNK = 64
RWKV_PACK_BLOCKS = 11
PACK_R, PACK_V, PACK_KK, PACK_DIR, PACK_G, PACK_BONUS = 0, 1, 2, (3, 6), 9, 10

POOL_WINDOWS = (2, 4, 8, 16)
POOL_WIDTH = GROUP_WIDTH
POOL_GROUP = POOL_WIDTH // len(POOL_WINDOWS)
CONV_WIDTH = GROUP_WIDTH
CONV_COLS = 3 * CONV_WIDTH
FF_HIDDEN = 4 * D_MODEL
MLA_COLS = MLA_Q_RANK + MLA_KV_RANK + MLA_ROPE

ROW_TILE = 256
HALO = 8
MOD_ROWS = 16
VMEM_LIMIT = 62 * 1024 * 1024


def _cparams(sem):
    return pltpu.CompilerParams(dimension_semantics=sem, vmem_limit_bytes=VMEM_LIMIT)


def _resident(shape, index_map):
    return pl.BlockSpec(shape, index_map, pipeline_mode=pl.Buffered(1))


def _rms(x, g):
    return x * lax.rsqrt(jnp.mean(x * x, axis=-1, keepdims=True) + NORM_EPS) * g


def _dot(a, b):
    return jnp.dot(a.astype(BF16), b.astype(BF16), preferred_element_type=F32)


def _dot_nt(a, b):
    return lax.dot_general(a.astype(BF16), b.astype(BF16), (((1,), (1,)), ((), ())), preferred_element_type=F32)


def _split3(x):
    h1 = x.astype(BF16)
    r1 = x - h1.astype(F32)
    h2 = r1.astype(BF16)
    h3 = (r1 - h2.astype(F32)).astype(BF16)
    return h1, h2, h3


def _dot_exact_lhs(a01, x):
    return jnp.dot(jnp.concatenate([a01] * 3, axis=1), jnp.concatenate(_split3(x), axis=0), preferred_element_type=F32)


def _dot_x2(a, b):
    a1 = a.astype(BF16)
    b1 = b.astype(BF16)
    b2 = (b - b1.astype(F32)).astype(BF16)
    return jnp.dot(jnp.concatenate([a1, a1], axis=1), jnp.concatenate([b1, b2], axis=0), preferred_element_type=F32)


def _mod_kernel(c_ref, w_ref, b_ref, o_ref):
    c = c_ref[...]
    s = c * jax.nn.sigmoid(c)
    o_ref[...] = _dot(s, w_ref[...]) + b_ref[...]


def _modulation(cc, ada_w, ada_b):
    depth = ada_w.shape[0]
    tn = 1024
    return pl.pallas_call(
        _mod_kernel,
        grid=(depth, 6 * D_MODEL // tn),
        in_specs=[
            pl.BlockSpec((MOD_ROWS, D_MODEL), lambda l, j: (0, 0)),
            pl.BlockSpec((None, D_MODEL, tn), lambda l, j: (l, 0, j)),
            pl.BlockSpec((None, 1, tn), lambda l, j: (l, 0, j)),
        ],
        out_specs=pl.BlockSpec((None, MOD_ROWS, tn), lambda l, j: (l, 0, j)),
        out_shape=jax.ShapeDtypeStruct((depth, MOD_ROWS, 6 * D_MODEL), F32),
        compiler_params=_cparams(("parallel", "parallel")),
        name="modulation",
    )(cc, ada_w, ada_b.reshape(depth, 1, 6 * D_MODEL))


def _mod_row(mod_ref, row, k):
    return mod_ref[pl.ds(row, 1), k * D_MODEL:(k + 1) * D_MODEL]


def _halo_specs(width, n_rows, tile0=0):
    per = ROW_TILE // HALO
    last_tile = n_rows // ROW_TILE - 1
    last = n_rows // HALO - 1
    clamp = lambda i, hi: jnp.clip(i, 0, hi)
    cur = pl.BlockSpec((None, ROW_TILE, width), lambda b, t: (b, clamp(t - tile0, last_tile), 0))
    prev = pl.BlockSpec((None, HALO, width), lambda b, t: (b, clamp((t - tile0) * per - 1, last), 0))
    nxt = pl.BlockSpec((None, HALO, width), lambda b, t: (b, clamp((t - tile0 + 1) * per, last), 0))
    return cur, prev, nxt


def _token_tile_spec(n_rows, tile0, t_off):
    last_tile = n_rows // ROW_TILE - 1
    return pl.BlockSpec((None, ROW_TILE, D_MODEL), lambda b, t: (b, jnp.clip(t + t_off - tile0, 0, last_tile), 0))


def _seq_edges(t, nct, nt):
    has_prev = jnp.logical_and(t != 0, t != nct)
    has_next = jnp.logical_and(t != nct - 1, t != nt - 1)
    return has_prev, has_next


def _seg_sum(x, e_ref):
    h1 = x.astype(BF16)
    h2 = (x - h1.astype(F32)).astype(BF16)
    e = e_ref[...]
    return jnp.dot(h1, e, preferred_element_type=F32) + jnp.dot(h2, e, preferred_element_type=F32)


IN_SEGMENTS = (MLA_FEAT, RWKV_COLS, POOL_WIDTH, CONV_COLS)
HALO_ROWS = ROW_TILE + 2 * HALO
LO, HI = HALO, HALO + ROW_TILE


def _rope(x, cs, sn):
    return x * cs + pltpu.roll(x, 64, 1) * sn


def _store_with_halo(buf, cols, res):
    buf[LO:HI, cols] = res[0:ROW_TILE]
    buf[0:LO, cols] = res[ROW_TILE:ROW_TILE + HALO]
    buf[HI:HI + HALO, cols] = res[ROW_TILE + HALO:ROW_TILE + 2 * HALO]


def _mla_qkv(f, cs, sn, gq_ref, gkv_ref, wq_ref, wkv_ref, q_ref, k_ref, v_ref):
    scale = (MLA_NOPE + MLA_ROPE) ** -0.5
    q = _dot(_rms(f[:, 0:MLA_Q_RANK], gq_ref[...]), wq_ref[...])
    kv = _dot(_rms(f[:, MLA_Q_RANK:MLA_Q_RANK + MLA_KV_RANK], gkv_ref[...]), wkv_ref[...])
    k_pe = _rope(f[:, 512:640], cs, sn).astype(BF16)
    for h in range(MLA_HEADS):
        o = h * MLA_QK_PAD
        q_ref[:, o:o + 128] = (q[:, o:o + 128] * scale).astype(BF16)
        q_ref[:, o + 128:o + 256] = (_rope(q[:, o + 128:o + 256], cs, sn) * scale).astype(BF16)
        k_ref[:, o:o + 128] = kv[:, o:o + 128].astype(BF16)
        k_ref[:, o + 128:o + 256] = k_pe
        v_ref[:, h * MLA_V:(h + 1) * MLA_V] = kv[:, o + 128:o + 256].astype(BF16)


def _rwkv_features(buf, mu_ref, w0_ref, a0_ref, w2_ref, a2_ref, g2_ref, kk_ref, ka_ref, rk_ref, e_ref, o_ref):
    w = RWKV_WIDTH

    def shifted(cols):
        f = buf[LO:HI, cols]
        return f + mu_ref[:, cols] * (0.5 * (buf[LO - 1:HI - 1, cols] + buf[LO + 1:HI + 1, cols]) - f)

    def put(i, val):
        o_ref[:, i * w:(i + 1) * w] = val

    xw = w0_ref[...] + _dot(jnp.tanh(shifted(slice(1536, 1792))), w2_ref[...])
    lw = -float(np.exp(-0.5)) * jax.nn.sigmoid(xw)
    fwd, bwd = PACK_DIR
    put(fwd, lw[:, 0:w])
    put(bwd, lw[:, w:2 * w])
    yield
    a = jax.nn.sigmoid(a0_ref[...] + _dot(shifted(slice(1664, 1920)), a2_ref[...]))
    put(fwd + 2, a[:, 0:w])
    put(bwd + 2, a[:, w:2 * w])
    put(PACK_G, _dot(jax.nn.sigmoid(shifted(slice(1920, 2176))), g2_ref[...]))
    yield
    k = shifted(slice(w, 2 * w))
    kkv = k * kk_ref[...]
    put(PACK_KK, kkv / jnp.maximum(jnp.sqrt(_seg_sum(kkv * kkv, e_ref)), 1e-12))
    yield
    r = shifted(slice(0, w))
    put(PACK_R, r)
    rk = r * rk_ref[...]
    ka = ka_ref[...]
    k_f = k * (1.0 + (a[:, 0:w] - 1.0) * ka)
    k_b = k * (1.0 + (a[:, w:2 * w] - 1.0) * ka)
    put(fwd + 1, k_f)
    put(bwd + 1, k_b)
    yield
    v = shifted(slice(2 * w, 3 * w))
    put(PACK_V, v)
    put(PACK_BONUS, _seg_sum(rk * k_f + rk * k_b, e_ref) * v)


def _pool_mixer(buf, pw_ref, ps_ref, o_ref, pos, n_own):
    for gi, win in enumerate(POOL_WINDOWS):
        hw = win // 2
        cols = slice(gi * POOL_GROUP, (gi + 1) * POOL_GROUP)
        s = buf[LO - hw:HI - hw, cols]
        for o in range(-hw + 1, hw):
            s = s + buf[LO + o:HI + o, cols]
        cnt = (jnp.minimum(pos + hw, n_own) - jnp.maximum(pos - hw, 0)).astype(F32)
        z = s / cnt - buf[LO:HI, cols]
        o_ref[:, cols] = (_dot(z, pw_ref[gi]) * ps_ref[:, cols]).astype(o_ref.dtype)


def _pick_rows(refs, in_ctx):
    if len(refs) == 3:
        return [r[...] for r in refs]
    return [jnp.where(in_ctx, c[...], x[...]) for c, x in zip(refs[0:3], refs[3:6])]


FRONT_FIXED_INPUTS = 22


def _front_kernel(*refs, nparts, ncast, nb, nct, nt, n_ctx, n_seq):
    x_refs, refs = refs[:3 * nparts], refs[3 * nparts:]
    (mod_ref, g_ref, w_ref, cs_ref, sn_ref, gq_ref, gkv_ref, wq_ref, wkv_ref,
     mu_ref, w0_ref, a0_ref, w2_ref, a2_ref, g2_ref, kk_ref, ka_ref, rk_ref, e_ref,
     pw_ref, ps_ref, cw_ref) = refs[:FRONT_FIXED_INPUTS]
    cast_in, refs = refs[FRONT_FIXED_INPUTS:FRONT_FIXED_INPUTS + ncast], refs[FRONT_FIXED_INPUTS + ncast:]
    q_ref, k_ref, v_ref, pack_ref, pool_ref, conv_ref = refs[:6]
    cast_out = refs[6:6 + ncast]
    rw_buf, pool_buf, u_buf = refs[6 + ncast:]
    for src, dst in zip(cast_in, cast_out):
        dst[...] = src[...].astype(dst.dtype)
    b = pl.program_id(0)
    t = pl.program_id(1)
    in_ctx = t < nct
    row = jnp.where(in_ctx, nb, b)
    has_prev, has_next = _seq_edges(t, nct, nt)
    shift, scale = _mod_row(mod_ref, row, 0), 1.0 + _mod_row(mod_ref, row, 1)
    norm = lambda x: _rms(x, g_ref[...]) * scale + shift
    x_cur, x_prev, x_next = _pick_rows(x_refs, in_ctx)
    xn = jnp.concatenate([norm(x_cur), jnp.where(has_prev, norm(x_prev), 0.0), jnp.where(has_next, norm(x_next), 0.0)],
                         axis=0).astype(BF16)

    mla0 = 0
    rw0 = IN_SEGMENTS[0]
    pool0 = rw0 + RWKV_COLS
    conv0 = pool0 + POOL_WIDTH
    cw = CONV_WIDTH
    proj = lambda rows, c0, c1: lax.dot_general(rows, w_ref[c0:c1, :], (((1,), (1,)), ((), ())),
                                                preferred_element_type=F32)

    c0 = 0
    while c0 < RWKV_COLS:
        c1 = min(c0 + 512, RWKV_COLS)
        _store_with_halo(rw_buf, slice(c0, c1), proj(xn, rw0 + c0, rw0 + c1))
        c0 = c1
    stages = _rwkv_features(rw_buf, mu_ref, w0_ref, a0_ref, w2_ref, a2_ref, g2_ref, kk_ref, ka_ref, rk_ref, e_ref, pack_ref)
    next(stages)
    f_mla = proj(xn[0:ROW_TILE], mla0, rw0)
    _store_with_halo(pool_buf, slice(None), proj(xn, pool0, conv0))
    next(stages)
    gb = proj(xn[0:ROW_TILE], conv0, conv0 + cw)
    _mla_qkv(f_mla, cs_ref[...], sn_ref[...], gq_ref, gkv_ref, wq_ref, wkv_ref, q_ref, k_ref, v_ref)
    next(stages)
    gc = proj(xn, conv0 + cw, conv0 + 2 * cw)
    pos = lax.broadcasted_iota(jnp.int32, (ROW_TILE, 1), 0) + jnp.where(in_ctx, t, t - nct) * ROW_TILE
    _pool_mixer(pool_buf, pw_ref, ps_ref, pool_ref, pos, jnp.where(in_ctx, n_ctx, n_seq))
    next(stages)
    hx = proj(xn, conv0 + 2 * cw, conv0 + 3 * cw)
    for _ in stages:
        pass
    _store_with_halo(u_buf, slice(None), gc * hx)
    z = cw_ref[0:1, :] * u_buf[LO - 1:HI - 1] + cw_ref[1:2, :] * u_buf[LO:HI] + cw_ref[2:3, :] * u_buf[LO + 1:HI + 1]
    conv_ref[...] = (gb * z).astype(conv_ref.dtype)


def _layer_spec(a, l):
    return _resident((None,) + a.shape[1:], lambda *_: (l,) + (0,) * (a.ndim - 1))


def _cast_slab_spec(w, l, b_, nt):
    _, rows, cols = w.shape
    k = 1
    while k * 2 <= nt and cols % (k * 2 * 128) == 0:
        k *= 2
    assert rows % (b_ * 8) == 0
    shape = (rows // b_, cols // k)
    col = lambda t: jnp.minimum(t, k - 1)
    return (pl.BlockSpec((None,) + shape, lambda b, t: (l, b, col(t))),
            pl.BlockSpec(shape, lambda b, t: (b, col(t))),
            jax.ShapeDtypeStruct((rows, cols), BF16))


def _front(h_parts, l, mod, g, w_p, cs, sn, gq, gkv, wq, wkv, p, pool_w, pool_scale, conv_w, cast_ws, nb, nct, n_ctx):
    b_ = h_parts[0].shape[0]
    n_ = sum(h.shape[1] for h in h_parts)
    nt = n_ // ROW_TILE
    casts = [_cast_slab_spec(w, l, b_, nt) for w in cast_ws]
    qk_w = MLA_HEADS * MLA_QK_PAD
    pack_w = RWKV_PACK_BLOCKS * RWKV_WIDTH
    full = lambda a: _resident(a.shape, lambda b, t: (0,) * a.ndim)
    row = lambda w: pl.BlockSpec((None, ROW_TILE, w), lambda b, t: (b, t, 0))
    tab = pl.BlockSpec((ROW_TILE, 128), lambda b, t: (t, 0))
    x_specs, x_args, tile0 = [], [], 0
    for h in h_parts:
        x_specs += _halo_specs(D_MODEL, h.shape[1], tile0)
        x_args += [h, h, h]
        tile0 += h.shape[1] // ROW_TILE
    consts = (mod, g.reshape(1, D_MODEL))
    mla = (gq.reshape(1, -1), gkv.reshape(1, -1), wq, wkv)
    rwk = (p["mu"], p["w0"], p["a0"], p["w2"], p["a2"], p["g2"], p["k_k"], p["k_a"], p["r_k"], p["seg"])
    mix = (pool_w, pool_scale, conv_w)
    fixed = [full(a) for a in consts] + [_layer_spec(w_p, l), tab, tab] + [full(a) for a in mla + rwk + mix]
    assert len(fixed) == FRONT_FIXED_INPUTS
    outs = pl.pallas_call(
        functools.partial(_front_kernel, nparts=len(h_parts), ncast=len(casts), nb=nb, nct=nct, nt=nt,
                          n_ctx=n_ctx, n_seq=n_ - n_ctx),
        grid=(b_, nt),
        in_specs=x_specs + fixed + [c[0] for c in casts],
        out_specs=(row(qk_w), row(qk_w), row(MLA_HEADS * MLA_V), row(pack_w), row(POOL_WIDTH), row(CONV_WIDTH))
                  + tuple(c[1] for c in casts),
        out_shape=(
            jax.ShapeDtypeStruct((b_, n_, qk_w), BF16),
            jax.ShapeDtypeStruct((b_, n_, qk_w), BF16),
            jax.ShapeDtypeStruct((b_, n_, MLA_HEADS * MLA_V), BF16),
            jax.ShapeDtypeStruct((b_, n_, pack_w), F32),
            jax.ShapeDtypeStruct((b_, n_, POOL_WIDTH), BF16),
            jax.ShapeDtypeStruct((b_, n_, CONV_WIDTH), BF16),
        ) + tuple(c[2] for c in casts),
        scratch_shapes=[pltpu.VMEM((HALO_ROWS, RWKV_COLS), F32), pltpu.VMEM((HALO_ROWS, POOL_WIDTH), F32),
                        pltpu.VMEM((HALO_ROWS, CONV_WIDTH), F32)],
        compiler_params=_cparams(("parallel", "arbitrary")),
        name="front",
    )(*x_args, *consts, w_p, cs, sn, *mla, *rwk, *mix, *cast_ws)
    return outs[:6], outs[6:]


ATTN_Q_TILE = 512
ATTN_SOFTMAX_ROWS = 128


def _attn_kernel(q_ref, k_ref, v_ref, o_ref, *, n_ctx, n_all, tq, with_ctx):
    s = pl.program_id(1)

    def attend(q_rows, o_rows, nk, nq):
        for h in range(MLA_HEADS):
            q = q_ref[q_rows, h * MLA_QK_PAD:(h + 1) * MLA_QK_PAD]
            k = k_ref[0:nk, h * MLA_QK_PAD:(h + 1) * MLA_QK_PAD]
            sc = lax.dot_general(q, k, (((1,), (1,)), ((), ())), preferred_element_type=F32)
            ps, ls = [], []
            for r0 in range(0, nq, ATTN_SOFTMAX_ROWS):
                blk = sc[r0:r0 + ATTN_SOFTMAX_ROWS]
                p = jnp.exp(blk - jnp.max(blk, axis=-1, keepdims=True))
                ls.append(jnp.sum(p, axis=-1, keepdims=True))
                ps.append(p.astype(BF16))
            p = jnp.concatenate(ps, axis=0)
            l = jnp.concatenate(ls, axis=0)
            o = jnp.dot(p, v_ref[0:nk, h * MLA_V:(h + 1) * MLA_V], preferred_element_type=F32)
            o_ref[o_rows, h * MLA_V:(h + 1) * MLA_V] = (o / l).astype(o_ref.dtype)

    if with_ctx:
        @pl.when(s == 0)
        def _():
            attend(slice(0, n_ctx), slice(0, n_ctx), n_ctx, n_ctx)

        @pl.when(s > 0)
        def _():
            r0 = pl.multiple_of(n_ctx + (s - 1) * tq, ROW_TILE)
            attend(pl.ds(r0, tq), pl.ds(r0, tq), n_all, tq)
    else:
        attend(pl.ds(pl.multiple_of(n_ctx + s * tq, ROW_TILE), tq), pl.ds(pl.multiple_of(s * tq, ROW_TILE), tq), n_all, tq)


def _attention(q, k, v, n_ctx, with_ctx):
    b_, n_, qk_w = q.shape
    n_seq = n_ - n_ctx
    tq = ATTN_Q_TILE if n_seq % ATTN_Q_TILE == 0 else ROW_TILE
    n_out = n_ if with_ctx else n_seq
    whole = lambda rows, w: pl.BlockSpec((None, rows, w), lambda b, s: (b, 0, 0))
    return pl.pallas_call(
        functools.partial(_attn_kernel, n_ctx=n_ctx, n_all=n_, tq=tq, with_ctx=with_ctx),
        grid=(b_, n_seq // tq + (1 if with_ctx else 0)),
        in_specs=[whole(n_, qk_w), whole(n_, qk_w), whole(n_, MLA_HEADS * MLA_V)],
        out_specs=whole(n_out, MLA_HEADS * MLA_V),
        out_shape=jax.ShapeDtypeStruct((b_, n_out, MLA_HEADS * MLA_V), BF16),
        compiler_params=_cparams(("parallel", "arbitrary")),
        name="mla_attention",
    )(q, k, v)


def _bd(x, m0):
    return jnp.concatenate([jnp.where(m0, x, 0.0), jnp.where(m0, 0.0, x)], axis=0)


def _wkv_stages(r_ref, v_ref, kk_ref, lw_ref, kd_ref, a_ref, y_ref, s_ref, rw_s, y0_s, m_s, n_s, rev):
    c_ = WKV_CHUNK
    nsub = ROW_TILE // c_
    npair = RWKV_WIDTH // 128

    ri = lax.broadcasted_iota(jnp.int32, (c_, c_), 0)
    ci = lax.broadcasted_iota(jnp.int32, (c_, c_), 1)
    tmat = ((ri <= ci) if rev else (ri >= ci)).astype(BF16)
    rc = lax.broadcasted_iota(jnp.int32, (c_, 128), 0)
    cc = lax.broadcasted_iota(jnp.int32, (c_, 128), 1) % c_
    strict = (rc < cc) if rev else (rc > cc)
    incl = (rc <= cc) if rev else (rc >= cc)
    eye = rc == cc
    m0_128 = lax.broadcasted_iota(jnp.int32, (1, 128), 1) < c_
    m0_256 = (lax.broadcasted_iota(jnp.int32, (1, 256), 1) % 128) < c_
    zeros_cv = jnp.zeros((c_, 128), F32)

    items = [(c, p) for c in range(nsub) for p in range(npair)]
    at, rt, bt, kt, bh, kh, vv, e_tot = {}, {}, {}, {}, {}, {}, {}, {}
    for c in range(nsub):
        rows = slice(c * c_, (c + 1) * c_)
        lw = lw_ref[rows, :]
        lc = _dot_exact_lhs(tmat, lw)
        ltot = jnp.sum(lw, axis=0, keepdims=True)
        e_neg = jnp.exp(-lc)
        e_end = jnp.exp(ltot - lc)
        kk = kk_ref[rows, :]
        kd = kd_ref[rows, :]
        bv = kk * a_ref[rows, :]
        vv[c] = v_ref[rows, :]
        at[c] = -kk * jnp.exp(lc - lw)
        rt[c] = r_ref[rows, :] * jnp.exp(lc)
        bt[c], kt[c] = bv * e_neg, kd * e_neg
        bh[c], kh[c] = bv * e_end, kd * e_end
        e_tot[c] = jnp.exp(ltot)
        yield
    sl = lambda p: slice(p * 128, (p + 1) * 128)

    g = {}
    for c, p in items:
        ar = jnp.concatenate([at[c][:, sl(p)], rt[c][:, sl(p)]], axis=0)
        bd = jnp.concatenate([_bd(bt[c][:, sl(p)], m0_128), _bd(kt[c][:, sl(p)], m0_128)], axis=0)
        g[c, p] = _dot_nt(ar, bd)
    yield
    av, x, pw, a_rb = {}, {}, {}, {}
    for c, p in items:
        a_ak = jnp.where(strict, g[c, p][0:c_, 128:256], 0.0)
        a_rk = jnp.where(incl, g[c, p][c_:2 * c_, 128:256], 0.0)
        av[c, p] = _dot(jnp.concatenate([a_ak, a_rk], axis=0), _bd(vv[c][:, sl(p)], m0_128))
        pw[c, p] = jnp.where(strict, g[c, p][0:c_, 0:128], 0.0)
        a_rb[c, p] = jnp.where(incl, g[c, p][c_:2 * c_, 0:128], 0.0)
    for c, p in items:
        x[c, p] = jnp.concatenate([at[c][:, sl(p)], av[c, p][0:c_]], axis=1)
    yield
    for it in range(6):
        dot = _dot_x2 if it < 2 else _dot
        for c, p in items:
            x[c, p] = x[c, p] + dot(pw[c, p], _bd(x[c, p], m0_256))
        yield
        if it < 5:
            for c, p in items:
                pw[c, p] = _dot(pw[c, p], _bd(pw[c, p], m0_128))
            yield
    for c, p in items:
        ry = _dot(a_rb[c, p], _bd(x[c, p], m0_256))
        rw_s[c, p] = rt[c][:, sl(p)] + ry[:, 0:128]
        y0_s[c, p] = ry[:, 128:256] + av[c, p][c_:2 * c_]
    yield
    for c, p in items:
        bkt = jnp.concatenate([bh[c][:, sl(p)], kh[c][:, sl(p)]], axis=0).T
        rhs = jnp.concatenate([x[c, p], jnp.concatenate([zeros_cv, vv[c][:, sl(p)]], axis=1)], axis=0)
        z = _dot(bkt, rhs)
        m_s[c, p] = (jnp.where(m0_128, z[0:c_, 0:128], z[c_:2 * c_, 0:128])
                     + jnp.where(eye, e_tot[c][:, sl(p)], 0.0))
        n_s[c, p] = jnp.where(m0_128, z[0:c_, 128:256], z[c_:2 * c_, 128:256])
    yield

    for i in range(nsub):
        c = (nsub - 1 - i) if rev else i
        for p in range(npair):
            lhs = jnp.concatenate([rw_s[c, p], m_s[c, p]], axis=0)
            o = _dot_x2(lhs, _bd(s_ref[p], m0_128))
            y_ref[c * c_:(c + 1) * c_, sl(p)] = o[0:c_] + y0_s[c, p]
            s_ref[p] = o[c_:2 * c_] + n_s[c, p]
        yield


WKV_LEAD_STAGES = 9


def _wkv_kernel(*refs):
    (tok_f, dir_f, tok_b, dir_b), (y_f, y_b), scr_f, scr_b = refs[0:4], refs[4:6], refs[6:11], refs[11:16]
    w = RWKV_WIDTH
    cols = lambda ref, i: ref.at[:, i * w:(i + 1) * w]
    ins_f = [cols(tok_f, PACK_R), cols(tok_f, PACK_V), cols(tok_f, PACK_KK), cols(dir_f, 0), cols(dir_f, 1), cols(dir_f, 2)]
    ins_b = [cols(tok_b, PACK_R), cols(tok_b, PACK_V), cols(tok_b, PACK_KK), cols(dir_b, 0), cols(dir_b, 1), cols(dir_b, 2)]

    @pl.when(pl.program_id(1) == 0)
    def _():
        scr_f[0][...] = jnp.zeros_like(scr_f[0])
        scr_b[0][...] = jnp.zeros_like(scr_b[0])

    fwd = _wkv_stages(*ins_f, y_f, *scr_f, rev=False)
    bwd = _wkv_stages(*ins_b, y_b, *scr_b, rev=True)
    for _ in range(WKV_LEAD_STAGES):
        next(fwd)
    live = [fwd, bwd]
    while live:
        for gen in list(live):
            if next(gen, StopIteration) is StopIteration:
                live.remove(gen)


def _wkv(pack, nct):
    b_, n_, _ = pack.shape
    nt = n_ // ROW_TILE
    w = RWKV_WIDTH
    nsub = ROW_TILE // WKV_CHUNK
    npair = w // 128
    fwd_tile = lambda s: s
    bwd_tile = lambda s: jnp.where(s < nct, nct - 1 - s, nt + nct - 1 - s)
    grp = lambda order, j: pl.BlockSpec((None, ROW_TILE, 3 * w), lambda b, s: (b, order(s), j))
    assert PACK_DIR == (3, 6) and (PACK_R, PACK_V, PACK_KK) == (0, 1, 2)
    out = lambda order: pl.BlockSpec((None, ROW_TILE, w), lambda b, s: (b, order(s), 0))
    scr = lambda: [pltpu.VMEM((npair, WKV_CHUNK, 128), F32)] + [pltpu.VMEM((nsub, npair, WKV_CHUNK, 128), F32)] * 4
    y = jax.ShapeDtypeStruct((b_, n_, w), F32)
    return pl.pallas_call(
        _wkv_kernel,
        grid=(b_, nt),
        in_specs=[grp(fwd_tile, 0), grp(fwd_tile, 1), grp(bwd_tile, 0), grp(bwd_tile, 2)],
        out_specs=(out(fwd_tile), out(bwd_tile)),
        out_shape=(y, y),
        scratch_shapes=scr() + scr(),
        compiler_params=_cparams(("parallel", "arbitrary")),
        name="wkv",
    )(*([pack] * 4))


def _mix_out_kernel(att_ref, yf_ref, yb_ref, g_ref, bonus_ref, lg_ref, lb_ref, e_ref, pool_ref, conv_ref,
                    mod_ref, w_ref, g2_ref, *refs, nb, nct, t_off):
    h_refs, (o_ref, xn_ref) = refs[:-2], refs[-2:]
    b = pl.program_id(0)
    t = pl.program_id(1) + t_off
    in_ctx = t < nct
    row = jnp.where(in_ctx, nb, b)
    h = h_refs[0][...] if len(h_refs) == 1 else jnp.where(in_ctx, h_refs[0][...], h_refs[1][...])
    y = yf_ref[...] + yb_ref[...]
    mean = _seg_sum(y, e_ref) * (1.0 / RWKV_HEAD)
    d = y - mean
    var = _seg_sum(d * d, e_ref) * (1.0 / RWKV_HEAD)
    yn = d * lax.rsqrt(var + RWKV_GN_EPS) * lg_ref[...] + lb_ref[...]
    rw = ((yn + bonus_ref[...]) * g_ref[...]).astype(BF16)
    mixed = jnp.concatenate([att_ref[...], rw, pool_ref[...], conv_ref[...]], axis=1)
    acc = jnp.dot(mixed, w_ref[...], preferred_element_type=F32)
    h = h + _mod_row(mod_ref, row, 2) * acc
    o_ref[...] = h
    xn_ref[...] = (_rms(h, g2_ref[...]) * (1.0 + _mod_row(mod_ref, row, 4)) + _mod_row(mod_ref, row, 3)).astype(BF16)


def _mix_out(att, yf, yb, pack, pool, conv, h_parts, mod, w_out, g2, p, nb, nct, with_ctx):
    b_, n_, gw = yf.shape
    t_off = 0 if with_ctx else nct
    n_out = n_ - t_off * ROW_TILE
    blk = lambda j: pl.BlockSpec((None, ROW_TILE, gw), lambda b, t: (b, t + t_off, j))
    full = lambda a: _resident(a.shape, lambda b, t: (0,) * a.ndim)
    h_specs, tile0 = [], 0
    for h in h_parts:
        h_specs.append(_token_tile_spec(h.shape[1], tile0, t_off))
        tile0 += h.shape[1] // ROW_TILE
    out = pl.BlockSpec((None, ROW_TILE, D_MODEL), lambda b, t: (b, t, 0))
    return pl.pallas_call(
        functools.partial(_mix_out_kernel, nb=nb, nct=nct, t_off=t_off),
        grid=(b_, n_out // ROW_TILE),
        in_specs=[pl.BlockSpec((None, ROW_TILE, gw), lambda b, t: (b, t, 0)),
                  blk(0), blk(0), blk(PACK_G), blk(PACK_BONUS), full(p["ln_g"]), full(p["ln_b"]), full(p["seg"]),
                  blk(0), blk(0), full(mod), full(w_out), full(g2)] + h_specs,
        out_specs=(out, out),
        out_shape=(jax.ShapeDtypeStruct((b_, n_out, D_MODEL), F32), jax.ShapeDtypeStruct((b_, n_out, D_MODEL), BF16)),
        compiler_params=_cparams(("parallel", "parallel")),
        name="mix_out",
    )(att, yf, yb, pack, pack, p["ln_g"], p["ln_b"], p["seg"], pool, conv, mod, w_out, g2, *h_parts)


MLP_HIDDEN_TILE = 512


def _mlp_kernel(h_ref, xn_ref, mod_ref, w1_ref, w2_ref, gf_ref, o_ref, *, nb, ctx_rows, final):
    b = pl.program_id(0)
    t = pl.program_id(1)
    j = pl.program_id(2)

    def hidden_slice(first):
        hid = jnp.dot(xn_ref[...], w1_ref[...], preferred_element_type=F32)
        hid = jnp.square(jnp.maximum(hid, 0.0))
        part = jnp.dot(hid.astype(BF16), w2_ref[...], preferred_element_type=F32)
        o_ref[...] = part if first else o_ref[...] + part

    @pl.when(j == 0)
    def _():
        hidden_slice(True)

    @pl.when(j > 0)
    def _():
        hidden_slice(False)

    @pl.when(j == pl.num_programs(2) - 1)
    def _():
        def put(rows, mrow):
            o = h_ref[rows, :] + _mod_row(mod_ref, mrow, 5) * o_ref[rows, :]
            if final:
                o = _rms(o, gf_ref[...])
            o_ref[rows, :] = o

        first_row = jnp.where(t == 0, nb, b)
        if ctx_rows == 0:
            put(slice(None), b)
        elif ctx_rows == h_ref.shape[0]:
            put(slice(None), first_row)
        else:
            put(slice(0, ctx_rows), first_row)
            put(slice(ctx_rows, None), b)


def _mlp(h, xn, mod, w1, w2, g_final, nb, tm, ctx_rows, final):
    b_, n_, _ = h.shape
    th = MLP_HIDDEN_TILE * (2 if tm < 1024 else 1)
    tile = pl.BlockSpec((None, tm, D_MODEL), lambda b, t, j: (b, t, 0))
    return pl.pallas_call(
        functools.partial(_mlp_kernel, nb=nb, ctx_rows=ctx_rows, final=final),
        grid=(b_, n_ // tm, FF_HIDDEN // th),
        in_specs=[
            tile, tile,
            _resident((MOD_ROWS, 6 * D_MODEL), lambda b, t, j: (0, 0)),
            pl.BlockSpec((D_MODEL, th), lambda b, t, j: (0, j)),
            pl.BlockSpec((th, D_MODEL), lambda b, t, j: (j, 0)),
            _resident((1, D_MODEL), lambda b, t, j: (0, 0)),
        ],
        out_specs=tile,
        out_shape=jax.ShapeDtypeStruct((b_, n_, D_MODEL), F32),
        compiler_params=_cparams(("parallel", "parallel", "arbitrary")),
        name="mlp",
    )(h, xn, mod, w1, w2, g_final.reshape(1, D_MODEL))


def _rope_partner_perm():
    q = MLA_ROPE // 4
    i = np.arange(MLA_ROPE)
    return np.where((i // q) % 2 == 0, i + q, i - q)


def _rope_tables(n_ctx, n_seq):
    f32 = np.float32
    rows = n_seq // GRID_W
    row = np.repeat(np.arange(rows), GRID_W)
    col = np.tile(np.arange(GRID_W), rows)
    pos = np.stack([row, col], axis=-1).astype(f32)
    axis_dim = MLA_ROPE // 2
    inv_freq = (f32(ROPE_BASE) ** (-np.arange(0, axis_dim, 2, dtype=f32) / f32(axis_dim))).astype(f32)
    ang = (pos[:, :, None] * inv_freq).astype(f32)
    cos, sin = np.cos(ang), np.sin(ang)
    cos64 = np.concatenate([cos[:, 0], cos[:, 0], cos[:, 1], cos[:, 1]], axis=-1)
    sin64 = np.concatenate([-sin[:, 0], sin[:, 0], -sin[:, 1], sin[:, 1]], axis=-1)
    pad = np.zeros((n_seq, 128 - MLA_ROPE), f32)
    cs_lat = np.concatenate([cos64, pad], axis=-1)
    sn_lat = np.concatenate([sin64, pad], axis=-1)
    cs_ctx = np.concatenate([np.ones((n_ctx, MLA_ROPE), f32), np.zeros((n_ctx, 128 - MLA_ROPE), f32)], axis=-1)
    cs = np.concatenate([cs_ctx, cs_lat], axis=0).astype(f32)
    sn = np.concatenate([np.zeros((n_ctx, 128), f32), sn_lat], axis=0).astype(f32)
    return jnp.asarray(cs), jnp.asarray(sn)


W_IN_BLOCK = 256


def _in_proj_relayout_kernel(cur_ref, o_ref, tail_s):
    j = pl.program_id(1)
    kr_block = (MLA_Q_RANK + MLA_KV_RANK) // W_IN_BLOCK
    kr = (MLA_Q_RANK + MLA_KV_RANK) % W_IN_BLOCK
    q = MLA_ROPE // 4

    @pl.when(j < kr_block)
    def _():
        o_ref[...] = cur_ref[...].astype(BF16)

    @pl.when(j == kr_block)
    def _():
        end = kr + MLA_ROPE
        o_ref[0:end] = cur_ref[0:end].astype(BF16)
        for i in range(4):
            src = kr + (i ^ 1) * q
            o_ref[end + i * q:end + (i + 1) * q] = cur_ref[src:src + q].astype(BF16)
        o_ref[end + MLA_ROPE:W_IN_BLOCK] = cur_ref[end:W_IN_BLOCK - MLA_ROPE].astype(BF16)

    @pl.when(j > kr_block)
    def _():
        o_ref[0:MLA_ROPE] = tail_s[...].astype(BF16)
        o_ref[MLA_ROPE:W_IN_BLOCK] = cur_ref[0:W_IN_BLOCK - MLA_ROPE].astype(BF16)

    @pl.when(j >= kr_block)
    def _():
        tail_s[...] = cur_ref[W_IN_BLOCK - MLA_ROPE:W_IN_BLOCK]


def _in_proj_weights(w_in):
    depth, d, n_in = w_in.shape
    n_out = n_in + MLA_ROPE
    assert n_out == sum(IN_SEGMENTS) and n_out % W_IN_BLOCK == 0
    w_t = jnp.swapaxes(w_in, 1, 2)
    last = (n_in - 1) // W_IN_BLOCK
    return pl.pallas_call(
        _in_proj_relayout_kernel,
        grid=(depth, n_out // W_IN_BLOCK),
        in_specs=[pl.BlockSpec((None, W_IN_BLOCK, d), lambda l, j: (l, jnp.minimum(j, last), 0))],
        out_specs=pl.BlockSpec((None, W_IN_BLOCK, d), lambda l, j: (l, j, 0)),
        out_shape=jax.ShapeDtypeStruct((depth, n_out, d), BF16),
        scratch_shapes=[pltpu.VMEM((MLA_ROPE, d), F32)],
        compiler_params=_cparams(("parallel", "arbitrary")),
        name="in_proj_relayout",
    )(w_t)


def _layer_params(l, mla_w_uq, mla_w_ukv, rwkv_mu, rwkv_w0, rwkv_w2, rwkv_a0, rwkv_a2, rwkv_g2,
                  rwkv_k_k, rwkv_k_a, rwkv_r_k, rwkv_ln_g, rwkv_ln_b, seg):
    perm = _rope_partner_perm()
    wq = mla_w_uq[l].reshape(MLA_Q_RANK, MLA_HEADS, MLA_NOPE + MLA_ROPE)
    wq = jnp.concatenate([wq, wq[:, :, MLA_NOPE:][:, :, perm]], axis=-1).reshape(MLA_Q_RANK, MLA_HEADS * MLA_QK_PAD)

    wd = RWKV_WIDTH
    z = lambda r: jnp.zeros((r, wd), F32)
    w2 = jnp.concatenate([
        jnp.concatenate([rwkv_w2[l, 0], z(DECAY_RANK), z(256 - 2 * DECAY_RANK)], axis=0),
        jnp.concatenate([z(DECAY_RANK), rwkv_w2[l, 1], z(256 - 2 * DECAY_RANK)], axis=0)], axis=1)
    lead = 256 - 2 * ICL_RANK
    a2 = jnp.concatenate([
        jnp.concatenate([z(lead), rwkv_a2[l, 0], z(ICL_RANK)], axis=0),
        jnp.concatenate([z(lead), z(ICL_RANK), rwkv_a2[l, 1]], axis=0)], axis=1)
    rw = {
        "mu": rwkv_mu[l].reshape(1, RWKV_COLS),
        "w0": rwkv_w0[l].reshape(1, 2 * wd),
        "a0": rwkv_a0[l].reshape(1, 2 * wd),
        "w2": w2.astype(BF16),
        "a2": a2.astype(BF16),
        "g2": rwkv_g2[l].astype(BF16),
        "k_k": rwkv_k_k[l].reshape(1, wd),
        "k_a": rwkv_k_a[l].reshape(1, wd),
        "r_k": rwkv_r_k[l].reshape(1, wd),
        "ln_g": rwkv_ln_g[l].reshape(1, wd),
        "ln_b": rwkv_ln_b[l].reshape(1, wd),
        "seg": seg,
    }
    return wq.astype(BF16), mla_w_ukv[l].astype(BF16), rw


def kernel(x, c, ctx, c_ctx, ada_w, ada_b, norm1_g, norm2_g, w_in, mla_q_norm_g, mla_w_uq, mla_kv_norm_g, mla_w_ukv, rwkv_mu, rwkv_w0, rwkv_w2, rwkv_a0, rwkv_a2, rwkv_g2, rwkv_k_k, rwkv_k_a, rwkv_r_k, rwkv_ln_g, rwkv_ln_b, pool_w, pool_scale, conv_w, w_out, mlp_w1, mlp_w2, final_norm_g):
    nb, n_seq, d = x.shape
    n_ctx = ctx.shape[1]
    depth = ada_w.shape[0]
    assert d == D_MODEL and nb < MOD_ROWS
    assert n_ctx % ROW_TILE == 0 and n_seq % ROW_TILE == 0 and n_seq % GRID_W == 0
    nct = n_ctx // ROW_TILE

    cc = jnp.zeros((MOD_ROWS, D_MODEL), F32).at[:nb].set(c).at[nb].set(c_ctx)
    mod = _modulation(cc, ada_w, ada_b)
    cs, sn = _rope_tables(n_ctx, n_seq)
    head = np.arange(RWKV_WIDTH) // RWKV_HEAD
    seg = jnp.asarray(head[:, None] == head[None, :], dtype=BF16)

    w_in_p = _in_proj_weights(w_in)
    h_parts = (ctx, x)
    for l in range(depth):
        last = l == depth - 1
        with_ctx = not last
        wq, wkv, rw_p = _layer_params(
            l, mla_w_uq, mla_w_ukv, rwkv_mu, rwkv_w0, rwkv_w2, rwkv_a0, rwkv_a2, rwkv_g2,
            rwkv_k_k, rwkv_k_a, rwkv_r_k, rwkv_ln_g, rwkv_ln_b, seg)

        (q, k, v, pack, pool, conv), (w_out_b, w1_b, w2_b) = _front(
            h_parts, l, mod[l], norm1_g[l], w_in_p, cs, sn, mla_q_norm_g[l], mla_kv_norm_g[l], wq, wkv, rw_p,
            pool_w[l].astype(BF16), pool_scale[l].reshape(1, POOL_WIDTH), conv_w[l],
            (w_out, mlp_w1, mlp_w2), nb, nct, n_ctx)
        att = _attention(q, k, v, n_ctx, with_ctx)
        y_f, y_b = _wkv(pack, nct)
        h_mid, xn_mid = _mix_out(att, y_f, y_b, pack, pool, conv, h_parts, mod[l], w_out_b,
                                 norm2_g[l].reshape(1, D_MODEL), rw_p, nb, nct, with_ctx)

        n_rows = h_mid.shape[1]
        if with_ctx:
            tm = 3 * ROW_TILE if n_rows % (3 * ROW_TILE) == 0 else ROW_TILE
            ctx_rows = n_ctx
            assert n_ctx <= tm, "context longer than one MLP row tile"
        else:
            tm = 4 * ROW_TILE if n_rows % (4 * ROW_TILE) == 0 else ROW_TILE
            ctx_rows = 0
        h_parts = (_mlp(h_mid, xn_mid, mod[l], w1_b, w2_b, final_norm_g, nb, tm, ctx_rows, last),)
    return h_parts[0]
```
